```python
import jax, jax.numpy as jnp
from jax import lax
import numpy as np

D_MODEL = 1024
BATCH = 1
SEQ = 16384
DEPTH = 2
DEC_BATCH = 2
DEC_SEQ = 8192
PAST_LEN = 128

N_EVEN = (DEPTH + 1) // 2
N_ODD = DEPTH // 2
D_A = D_MODEL // 2
A_GROUPS = 4
A_HEAD_DIM = D_A // A_GROUPS
CHUNK = 128
D_B = D_MODEL // 2
B_GROUPS = 4
CONV_WIDTH = 31
CONV_PAD = (CONV_WIDTH - 1) // 2
D_IN = 2 * D_A + 2 * D_B
C_GROUPS = 4
C_GROUP_DIM = D_MODEL // C_GROUPS
N_EXPERTS = 16
CAPACITY_FACTOR = 2
D_EXPERT = D_MODEL
RMS_EPS = 1e-6
LN_EPS = 1e-5

kernel_name = "hybrid_gmlp_conformer_fnet_ec_moe_encoder"


def _rmsnorm(x, g):
    xf = x.astype(jnp.float32)
    y = xf * lax.rsqrt(jnp.mean(xf * xf, axis=-1, keepdims=True) + RMS_EPS)
    return (y * g.astype(jnp.float32)).astype(x.dtype)


def _layernorm(x, g, b):
    xf = x.astype(jnp.float32)
    mu = jnp.mean(xf, axis=-1, keepdims=True)
    var = jnp.mean(jnp.square(xf - mu), axis=-1, keepdims=True)
    y = (xf - mu) * lax.rsqrt(var + LN_EPS)
    return (y * g.astype(jnp.float32) + b.astype(jnp.float32)).astype(x.dtype)


def _mixer_ab(xn, w_in, v_norm_g, v_norm_b, w_spatial, b_spatial, conv_w, conv_b,
              conv_norm_g, conv_norm_b, w_out):
    bsz, s, _ = xn.shape
    h = xn @ w_in
    z = jax.nn.gelu(h[..., :2 * D_A], approximate=False)
    u, v = z[..., :D_A], z[..., D_A:]
    v = _layernorm(v, v_norm_g, v_norm_b)
    vc = v.reshape(bsz, s // CHUNK, CHUNK, A_GROUPS, A_HEAD_DIM)
    mixed = jnp.einsum('hqp,bnphd->bnqhd', w_spatial, vc) \
        + jnp.transpose(b_spatial)[None, None, :, :, None]
    out_a = u * mixed.reshape(bsz, s, D_A)
    a_in = h[..., 2 * D_A:2 * D_A + D_B]
    gate = h[..., 2 * D_A + D_B:]
    glu = a_in * jax.nn.sigmoid(gate)
    conv = lax.conv_general_dilated(
        glu, conv_w[:, None, :].astype(glu.dtype), window_strides=(1,),
        padding=[(CONV_PAD, CONV_PAD)], dimension_numbers=('NWC', 'WIO', 'NWC'),
        feature_group_count=D_B) + conv_b
    out_b = jax.nn.silu(_layernorm(conv, conv_norm_g, conv_norm_b))
    return jnp.concatenate([out_a, out_b], axis=-1) @ w_out


def _mixer_c(xn, w_out):
    bsz, s, _ = xn.shape
    xg = xn.reshape(bsz, s, C_GROUPS, C_GROUP_DIM).astype(jnp.float32)
    f = jnp.fft.fft2(xg, axes=(1, 3), norm='ortho').real
    return f.reshape(bsz, s, D_MODEL).astype(xn.dtype) @ w_out


def _ec_moe(x, w_router, w_gate, w_up, w_down):
    bsz, s, d = x.shape
    t = bsz * s
    cap = CAPACITY_FACTOR * t // N_EXPERTS
    xt = x.reshape(t, d)
    logits = xt.astype(jnp.float32) @ w_router.astype(jnp.float32)
    aff = jax.nn.softmax(logits, axis=-1)
    g, idx = lax.top_k(jnp.transpose(aff), cap)
    xe = xt[idx]
    hid = jax.nn.silu(jnp.einsum('ecd,edf->ecf', xe, w_gate)) * jnp.einsum('ecd,edf->ecf', xe, w_up)
    ye = jnp.einsum('ecf,efd->ecd', hid, w_down) * g[..., None].astype(x.dtype)
    out = jnp.zeros((t, d), x.dtype).at[idx.reshape(-1)].add(ye.reshape(-1, d))
    return out.reshape(bsz, s, d)


def _trunk(x, mix_norm, ab_w_in, ab_v_norm_g, ab_v_norm_b, ab_w_spatial, ab_b_spatial,
           ab_conv_w, ab_conv_b, ab_conv_norm_g, ab_conv_norm_b, ab_w_out, c_w_out,
           moe_norm, moe_router, moe_w_gate, moe_w_up, moe_w_down, final_norm):
    for i in range(DEPTH):
        xn = _rmsnorm(x, mix_norm[i])
        if i % 2 == 0:
            j = i // 2
            x = x + _mixer_ab(xn, ab_w_in[j], ab_v_norm_g[j], ab_v_norm_b[j], ab_w_spatial[j],
                              ab_b_spatial[j], ab_conv_w[j], ab_conv_b[j], ab_conv_norm_g[j],
                              ab_conv_norm_b[j], ab_w_out[j])
        else:
            x = x + _mixer_c(xn, c_w_out[i // 2])
        x = x + _ec_moe(_rmsnorm(x, moe_norm[i]), moe_router[i], moe_w_gate[i],
                        moe_w_up[i], moe_w_down[i])
    return _rmsnorm(x, final_norm)


def setup_inputs(seed: int = 0) -> dict:
    key = jax.random.key(seed)
    ks = jax.random.split(key, 24)
    f32 = jnp.float32
    nrm = lambda k, shape, scale: jax.random.normal(k, shape, f32) * scale
    gain = lambda k, shape: 1.0 + 0.01 * jax.random.normal(k, shape, f32)
    return {
        "x_prompt": jax.random.normal(ks[0], (BATCH, SEQ, D_MODEL), f32),
        "x_sample": jax.random.normal(ks[1], (DEC_BATCH, DEC_SEQ, D_MODEL), f32),
        "mix_norm": gain(ks[2], (DEPTH, D_MODEL)),
        "ab_w_in": nrm(ks[3], (N_EVEN, D_MODEL, D_IN), D_MODEL ** -0.5),
        "ab_v_norm_g": gain(ks[4], (N_EVEN, D_A)),
        "ab_v_norm_b": nrm(ks[5], (N_EVEN, D_A), 0.01),
        "ab_w_spatial": nrm(ks[6], (N_EVEN, A_GROUPS, CHUNK, CHUNK), CHUNK ** -0.5),
        "ab_b_spatial": gain(ks[7], (N_EVEN, A_GROUPS, CHUNK)),
        "ab_conv_w": nrm(ks[8], (N_EVEN, CONV_WIDTH, D_B), CONV_WIDTH ** -0.5),
        "ab_conv_b": nrm(ks[9], (N_EVEN, D_B), 0.01),
        "ab_conv_norm_g": gain(ks[10], (N_EVEN, D_B)),
        "ab_conv_norm_b": nrm(ks[11], (N_EVEN, D_B), 0.01),
        "ab_w_out": nrm(ks[12], (N_EVEN, D_A + D_B, D_MODEL), (D_A + D_B) ** -0.5),
        "c_w_out": nrm(ks[13], (N_ODD, D_MODEL, D_MODEL), D_MODEL ** -0.5),
        "moe_norm": gain(ks[14], (DEPTH, D_MODEL)),
        "moe_router": nrm(ks[15], (DEPTH, D_MODEL, N_EXPERTS), D_MODEL ** -0.5),
        "moe_w_gate": nrm(ks[16], (DEPTH, N_EXPERTS, D_MODEL, D_EXPERT), D_MODEL ** -0.5),
        "moe_w_up": nrm(ks[17], (DEPTH, N_EXPERTS, D_MODEL, D_EXPERT), D_MODEL ** -0.5),
        "moe_w_down": nrm(ks[18], (DEPTH, N_EXPERTS, D_EXPERT, D_MODEL), D_EXPERT ** -0.5),
        "final_norm": gain(ks[19], (D_MODEL,)),
    }


def reference(x_prompt, x_sample, mix_norm, ab_w_in, ab_v_norm_g, ab_v_norm_b, ab_w_spatial,
              ab_b_spatial, ab_conv_w, ab_conv_b, ab_conv_norm_g, ab_conv_norm_b, ab_w_out,
              c_w_out, moe_norm, moe_router, moe_w_gate, moe_w_up, moe_w_down, final_norm):
    y_prompt = _trunk(x_prompt, mix_norm, ab_w_in, ab_v_norm_g, ab_v_norm_b, ab_w_spatial,
                      ab_b_spatial, ab_conv_w, ab_conv_b, ab_conv_norm_g, ab_conv_norm_b,
                      ab_w_out, c_w_out, moe_norm, moe_router, moe_w_gate, moe_w_up,
                      moe_w_down, final_norm)
    y_sample = _trunk(x_sample, mix_norm, ab_w_in, ab_v_norm_g, ab_v_norm_b, ab_w_spatial,
                      ab_b_spatial, ab_conv_w, ab_conv_b, ab_conv_norm_g, ab_conv_norm_b,
                      ab_w_out, c_w_out, moe_norm, moe_router, moe_w_gate, moe_w_up,
                      moe_w_down, final_norm)
    return (y_prompt, y_sample)
```

```python
import functools
import math

import numpy as np
import jax
import jax.numpy as jnp
from jax import lax
from jax.experimental import pallas as pl
from jax.experimental.pallas import tpu as pltpu

F32, BF16, I32 = jnp.float32, jnp.bfloat16, jnp.int32

D_MODEL = 1024
D_A = 512
D_B = 512
A_GROUPS = 4
A_HEAD_DIM = 128
CHUNK = 128
CONV_WIDTH = 31
CONV_PAD = 15
C_GROUPS = 4
C_GROUP_DIM = 256
N_EXPERTS = 16
CAPACITY_FACTOR = 2
RMS_EPS = 1e-6
LN_EPS = 1e-5

LANES = 128
HALO = 16
TM = 512
RB = 256
TB = 512
WIN = 128
FFT_N2 = 128
FFT_J = 8
FFT_K1C = 4
VMEM_LIMIT = 48 * 1024 * 1024
BISECT_BITS_STEPS = 31
BISECT_VALUE_STEPS = 30


def _cparams(sem):
    return pltpu.CompilerParams(dimension_semantics=sem, vmem_limit_bytes=VMEM_LIMIT)


def _dot(a, b):
    return jnp.dot(a, b, preferred_element_type=F32)


def _dot_nt(a, b, precision=None):
    return lax.dot_general(a, b, (((1,), (1,)), ((), ())), precision=precision,
                           preferred_element_type=F32)


def _rms(x, g):
    return x * lax.rsqrt(jnp.mean(x * x, axis=-1, keepdims=True) + RMS_EPS) * g


def _ln(x, g, b):
    mu = jnp.mean(x, axis=-1, keepdims=True)
    xc = x - mu
    var = jnp.mean(xc * xc, axis=-1, keepdims=True)
    return xc * lax.rsqrt(var + LN_EPS) * g + b


def _gelu(x):
    return 0.5 * x * (1.0 + lax.erf(x * np.float32(math.sqrt(0.5))))


def _silu(x):
    return x * jax.nn.sigmoid(x)


def _front_kernel(x_ref, g_ref, win_ref, vg_ref, vb_ref, wsp_ref, bsp_ref,
                  outa_ref, glu_ref, zu_scr, vn_scr):
    xn = _rms(x_ref[...], g_ref[...]).astype(BF16)
    zu_scr[...] = _gelu(_dot(xn, win_ref[:, 0:D_A]))
    zv = _gelu(_dot(xn, win_ref[:, D_A:2 * D_A]))
    vn_scr[...] = _ln(zv, vg_ref[...], vb_ref[...]).astype(BF16)
    a_in = _dot(xn, win_ref[:, 2 * D_A:2 * D_A + D_B])
    gate = _dot(xn, win_ref[:, 2 * D_A + D_B:])
    glu_ref[...] = a_in * jax.nn.sigmoid(gate)
    for c in range(TM // CHUNK):
        rows = slice(c * CHUNK, (c + 1) * CHUNK)
        for h in range(A_GROUPS):
            cols = slice(h * A_HEAD_DIM, (h + 1) * A_HEAD_DIM)
            mixed = _dot(wsp_ref[h], vn_scr[rows, cols]) + bsp_ref[:, cols]
            outa_ref[rows, cols] = (zu_scr[rows, cols] * mixed).astype(BF16)


def _front(x, g, w_in, vg, vb, wsp, bsp):
    t = x.shape[0]
    full = lambda shape: pl.BlockSpec(shape, lambda i: (0,) * len(shape))
    return pl.pallas_call(
        _front_kernel,
        grid=(t // TM,),
        in_specs=[
            pl.BlockSpec((TM, D_MODEL), lambda i: (i, 0)),
            full((1, D_MODEL)),
            full((D_MODEL, 2 * D_A + 2 * D_B)),
            full((1, D_A)), full((1, D_A)),
            full((A_GROUPS, CHUNK, CHUNK)),
            full((CHUNK, D_A)),
        ],
        out_specs=[pl.BlockSpec((TM, D_A), lambda i: (i, 0)),
                   pl.BlockSpec((TM, D_B), lambda i: (i, 0))],
        out_shape=[jax.ShapeDtypeStruct((t, D_A), BF16),
                   jax.ShapeDtypeStruct((t, D_B), F32)],
        scratch_shapes=[pltpu.VMEM((TM, D_A), F32), pltpu.VMEM((TM, D_A), BF16)],
        compiler_params=_cparams(("arbitrary",)),
        name="front",
    )(x, g, w_in, vg, vb, wsp, bsp)


CONV_ROWS = 64


def _router_epilogue(x1, mg_ref, wr_ref, x1_out, xn2_out, lg_out):
    x1_out[...] = x1
    xn2 = _rms(x1, mg_ref[...])
    xn2_out[...] = xn2
    lg_out[...] = _dot_nt(wr_ref[...], xn2, precision=lax.Precision.HIGHEST)


def _back_kernel(x_ref, outa_ref, glu_ref, prev_ref, next_ref, cw_ref, cb_ref, ng_ref, nb_ref,
                 wout_ref, mg_ref, wr_ref, x1_ref, xn2_ref, lg_ref, ext_scr, ob_scr,
                 *, blocks_per_seq):
    i = pl.program_id(0)
    pos = lax.rem(i, blocks_per_seq)
    ext_scr[0:HALO, :] = jnp.where(pos == 0, 0.0, prev_ref[...])
    ext_scr[HALO:HALO + TM, :] = glu_ref[...]
    ext_scr[HALO + TM:, :] = jnp.where(pos == blocks_per_seq - 1, 0.0, next_ref[...])
    for r in range(TM // CONV_ROWS):
        base = r * CONV_ROWS + HALO - CONV_PAD
        acc = jnp.zeros((CONV_ROWS, D_B), F32) + cb_ref[...]
        for k in range(CONV_WIDTH):
            acc = acc + cw_ref[k:k + 1, :] * ext_scr[base + k:base + k + CONV_ROWS, :]
        ob = _silu(_ln(acc, ng_ref[...], nb_ref[...]))
        ob_scr[r * CONV_ROWS:(r + 1) * CONV_ROWS, :] = ob.astype(BF16)
    y = _dot(outa_ref[...], wout_ref[0:D_A, :]) + _dot(ob_scr[...], wout_ref[D_A:, :])
    _router_epilogue(x_ref[...] + y, mg_ref, wr_ref, x1_ref, xn2_ref, lg_ref)


def _back(x, outa, glu, cw, cb, ng, nb_, wout, mg, wr_t, seq_len):
    t = x.shape[0]
    full = lambda shape: pl.BlockSpec(shape, lambda i: (0,) * len(shape))
    hb = TM // HALO
    return pl.pallas_call(
        functools.partial(_back_kernel, blocks_per_seq=seq_len // TM),
        grid=(t // TM,),
        in_specs=[
            pl.BlockSpec((TM, D_MODEL), lambda i: (i, 0)),
            pl.BlockSpec((TM, D_A), lambda i: (i, 0)),
            pl.BlockSpec((TM, D_B), lambda i: (i, 0)),
            pl.BlockSpec((HALO, D_B), lambda i: (jnp.maximum(i * hb - 1, 0), 0)),
            pl.BlockSpec((HALO, D_B), lambda i: (jnp.minimum((i + 1) * hb, t // HALO - 1), 0)),
            full((CONV_WIDTH, D_B)), full((1, D_B)), full((1, D_B)), full((1, D_B)),
            full((D_A + D_B, D_MODEL)),
            full((1, D_MODEL)),
            full((N_EXPERTS, D_MODEL)),
        ],
        out_specs=[pl.BlockSpec((TM, D_MODEL), lambda i: (i, 0)),
                   pl.BlockSpec((TM, D_MODEL), lambda i: (i, 0)),
                   pl.BlockSpec((N_EXPERTS, TM), lambda i: (0, i))],
        out_shape=[jax.ShapeDtypeStruct((t, D_MODEL), F32),
                   jax.ShapeDtypeStruct((t, D_MODEL), F32),
                   jax.ShapeDtypeStruct((N_EXPERTS, t), F32)],
        scratch_shapes=[pltpu.VMEM((TM + 2 * HALO, D_B), F32), pltpu.VMEM((TM, D_B), BF16)],
        compiler_params=_cparams(("arbitrary",)),
        name="back",
    )(x, outa, glu, glu, glu, cw, cb, ng, nb_, wout, mg, wr_t)


def _split3(a):
    a1 = a.astype(BF16)
    r1 = a - a1.astype(F32)
    a2 = r1.astype(BF16)
    a3 = (r1 - a2.astype(F32)).astype(BF16)
    return a1, a2, a3


def _route_kernel(lg_ref, idx_ref, gate_ref, cb_ref, lm_scr, w_scr, a1_scr, a2_scr, a3_scr,
                  cnt_scr, cbr_scr, *, nbk, cap):
    ne = N_EXPERTS
    lg = lg_ref[...]
    ex = jnp.exp(lg - jnp.max(lg, axis=0, keepdims=True))
    aff = ex / jnp.sum(ex, axis=0, keepdims=True)

    def count(m):
        s = jnp.sum(m.astype(F32), axis=1, keepdims=True)
        return jnp.sum(s, axis=2, keepdims=True)

    def update(mid, lo, hi):
        ge = count(aff >= mid) >= cap
        return jnp.where(ge, mid, lo), jnp.where(ge, hi, mid)

    def bits_body(_, c):
        lo, hi = c
        lo_i = lax.bitcast_convert_type(lo, I32)
        hi_i = lax.bitcast_convert_type(hi, I32)
        mid = lax.bitcast_convert_type(lo_i + lax.shift_right_logical(hi_i - lo_i, 1), F32)
        return update(mid, lo, hi)

    def value_body(_, c):
        lo, hi = c
        return update(lo + (hi - lo) * 0.5, lo, hi)

    lo0 = jnp.zeros((ne, 1, 1), F32)
    hi0 = jnp.full((ne, 1, 1), jnp.inf, F32)
    c = lax.fori_loop(0, BISECT_BITS_STEPS, bits_body, (lo0, hi0))
    lo, hi = lax.fori_loop(0, BISECT_VALUE_STEPS, value_body, c)

    li = lax.broadcasted_iota(I32, (LANES, LANES), 0)
    lj = lax.broadcasted_iota(I32, (LANES, LANES), 1)
    ut = (li <= lj).astype(BF16)
    ones = jnp.ones((LANES, LANES), BF16)
    bi = lax.broadcasted_iota(I32, (nbk, nbk), 0)
    bj = lax.broadcasted_iota(I32, (nbk, nbk), 1)
    slt = (bj < bi).astype(BF16)
    utb = (bi <= bj).astype(BF16)
    ones8 = jnp.ones((8, LANES), BF16)

    def block_cum(m):
        m2 = m.astype(BF16).reshape(ne * nbk, LANES)
        loc = _dot(m2, ut).reshape(ne, nbk, LANES)
        tot = _dot(m2, ones).astype(BF16).reshape(ne, nbk, LANES)
        off = jnp.stack([_dot(slt, tot[e]) for e in range(ne)])
        return loc, tot, off

    gt = aff >= hi
    eq = (aff >= lo) & (aff < hi)
    need = np.float32(cap) - count(gt)
    eloc, _, eoff = block_cum(eq)
    mask = gt | (eq & (eloc + eoff <= need))

    mloc, mtot, _ = block_cum(mask)
    lm_scr[...] = mloc.astype(BF16)
    w_scr[...] = mtot
    a1, a2, a3 = _split3(aff)
    a1_scr[...] = a1
    a2_scr[...] = a2
    a3_scr[...] = a3
    maskb = mask.astype(BF16)
    for e in range(ne):
        cnt = _dot_nt(ones8, maskb[e])
        cnt_scr[e] = cnt
        cbr_scr[e] = _dot(cnt.astype(BF16), utb)
    cb_ref[...] = cbr_scr[:, 0, :].astype(I32)

    s_col = lax.broadcasted_iota(I32, (cap, 1), 0).astype(F32)
    lane_b = lax.broadcasted_iota(I32, (cap, nbk), 1).astype(F32)
    lane_l = lax.broadcasted_iota(I32, (cap, LANES), 1).astype(F32)
    ones_b = jnp.ones((nbk, LANES), BF16)
    ones8b = jnp.ones((8, nbk), BF16)

    def per_expert(e, _):
        cbrow = cbr_scr[e][0:1, :]
        le = (cbrow <= s_col).astype(BF16)
        blk_c = _dot(le, ones_b)
        start_c = _dot(le, w_scr[e])
        oh = (lane_b == blk_c[:, 0:nbk]).astype(BF16)
        g_loc = _dot(oh, lm_scr[e])
        le2 = (g_loc <= s_col - start_c).astype(BF16)
        within_c = _dot(le2, ones)
        blk_r = _dot_nt(ones8b, le)
        within_r = _dot_nt(ones8, le2)
        idx_ref[pl.ds(e, 1), :] = (blk_r * LANES + within_r)[0:1, :].astype(I32)
        pick = lane_l == within_c
        gsum = None
        for a_scr in (a1_scr, a2_scr, a3_scr):
            sel = jnp.where(pick, _dot(oh, a_scr[e]), 0.0).astype(BF16)
            part = _dot_nt(ones8, sel)
            gsum = part if gsum is None else gsum + part
        gate_ref[pl.ds(e, 1), :] = gsum[0:1, :]
        return 0

    lax.fori_loop(0, ne, per_expert, 0)


def _route(logits_t, cap):
    ne, t = logits_t.shape
    nbk = t // LANES
    lg3 = logits_t.reshape(ne, nbk, LANES)
    full = lambda shape: pl.BlockSpec(shape, lambda i: (0,) * len(shape))
    tab = lambda dt: pltpu.VMEM((ne, nbk, LANES), dt)
    return pl.pallas_call(
        functools.partial(_route_kernel, nbk=nbk, cap=cap),
        grid=(1,),
        in_specs=[full((ne, nbk, LANES))],
        out_specs=[full((ne, cap)), full((ne, cap)), full((ne, nbk))],
        out_shape=[jax.ShapeDtypeStruct((ne, cap), I32),
                   jax.ShapeDtypeStruct((ne, cap), F32),
                   jax.ShapeDtypeStruct((ne, nbk), I32)],
        scratch_shapes=[tab(BF16), tab(BF16), tab(BF16), tab(BF16), tab(BF16),
                        pltpu.VMEM((ne, 8, nbk), F32), pltpu.VMEM((ne, 8, nbk), F32)],
        compiler_params=_cparams(("arbitrary",)),
        name="route",
    )(lg3)


def _ffn_kernel(idx_ref, idxn_ref, gate_ref, x_hbm, wg_ref, wu_ref, wd_ref, ye_ref,
                xbuf, sem, *, nsteps):
    step = pl.program_id(0) * pl.num_programs(1) + pl.program_id(1)
    slot = lax.rem(step, 2)

    def issue(ids_ref, s):
        def body(r, _):
            tok = ids_ref[0, 0, r]
            pltpu.make_async_copy(x_hbm.at[pl.ds(tok, 1)], xbuf.at[s, pl.ds(r, 1)],
                                  sem.at[s]).start()
            return 0
        lax.fori_loop(0, RB, body, 0, unroll=8)

    @pl.when(step == 0)
    def _():
        issue(idx_ref, 0)

    @pl.when(step + 1 < nsteps)
    def _():
        issue(idxn_ref, 1 - slot)

    pltpu.make_async_copy(x_hbm.at[pl.ds(0, RB)], xbuf.at[slot], sem.at[slot]).wait()

    xe = xbuf[slot].astype(BF16)
    hid = (_silu(_dot(xe, wg_ref[0])) * _dot(xe, wu_ref[0])).astype(BF16)
    ye = _dot(hid, wd_ref[0])
    ri = lax.broadcasted_iota(I32, (RB, RB), 0)
    ci = lax.broadcasted_iota(I32, (RB, RB), 1)
    gcol = jnp.sum(jnp.where(ri == ci, gate_ref[0], 0.0), axis=1, keepdims=True)
    ye_ref[...] = (ye * gcol).astype(BF16)


def _ffn(idx, gate, xn2, wg, wu, wd):
    ne, cap = idx.shape
    nblk = cap // RB
    nsteps = ne * nblk
    idx3 = idx.reshape(nsteps, 1, RB)
    gate3 = gate.reshape(nsteps, 1, RB)
    wspec = pl.BlockSpec((1, D_MODEL, D_MODEL), lambda e, j: (e, 0, 0))
    return pl.pallas_call(
        functools.partial(_ffn_kernel, nsteps=nsteps),
        grid=(ne, nblk),
        in_specs=[
            pl.BlockSpec((1, 1, RB), lambda e, j: (e * nblk + j, 0, 0), memory_space=pltpu.SMEM),
            pl.BlockSpec((1, 1, RB), lambda e, j: (jnp.minimum(e * nblk + j + 1, nsteps - 1), 0, 0),
                         memory_space=pltpu.SMEM),
            pl.BlockSpec((1, 1, RB), lambda e, j: (e * nblk + j, 0, 0)),
            pl.BlockSpec(memory_space=pl.ANY),
            wspec, wspec, wspec,
        ],
        out_specs=pl.BlockSpec((RB, D_MODEL), lambda e, j: (e * nblk + j, 0)),
        out_shape=jax.ShapeDtypeStruct((ne * cap, D_MODEL), BF16),
        scratch_shapes=[pltpu.VMEM((2, RB, D_MODEL), F32), pltpu.SemaphoreType.DMA((2,))],
        compiler_params=_cparams(("arbitrary", "arbitrary")),
        name="ffn",
    )(idx3, idx3, gate3, xn2, wg, wu, wd)


MAX_ITEMS = N_EXPERTS * (TB // WIN + 1)


def _combine_kernel(cb_ref, x1_ref, idx2_ref, ye_hbm, g_ref, *rest, nbk, cap, final):
    if final:
        y_ref, acc_scr, ybuf, sem, ie_scr, iw_scr = rest
    else:
        x2_ref, xn_ref, acc_scr, ybuf, sem, ie_scr, iw_scr = rest
    tb = pl.program_id(0)
    bpt = TB // LANES

    n = jnp.int32(0)
    for e in range(N_EXPERTS):
        b0 = e * nbk + tb * bpt
        lo = jnp.where(tb == 0, 0, cb_ref[jnp.maximum(b0 - 1, 0)])
        hi = cb_ref[b0 + bpt - 1]
        w0 = lax.shift_right_logical(lo, 7)
        w1 = jnp.where(hi > lo, lax.shift_right_logical(hi + (WIN - 1), 7), w0)

        def push(w, k, e=e):
            ie_scr[k] = e
            iw_scr[k] = w
            return k + 1

        n = lax.fori_loop(w0, w1, push, n)

    def copy(i, s):
        row0 = pl.multiple_of(ie_scr[i] * cap + iw_scr[i] * WIN, WIN)
        return pltpu.make_async_copy(ye_hbm.at[pl.ds(row0, WIN)], ybuf.at[s], sem.at[s])

    acc_scr[...] = jnp.zeros_like(acc_scr)

    @pl.when(n > 0)
    def _():
        copy(0, 0).start()

    tcol = tb * TB + lax.broadcasted_iota(I32, (TB, 1), 0)

    def body(i, _):
        s = lax.rem(i, 2)

        @pl.when(i + 1 < n)
        def _():
            copy(i + 1, 1 - s).start()

        copy(i, s).wait()
        row = ie_scr[i] * (cap // WIN) + iw_scr[i]
        onehot = (idx2_ref[pl.ds(row, 1), :] == tcol).astype(BF16)
        acc_scr[...] += _dot(onehot, ybuf[s])
        return 0

    lax.fori_loop(0, n, body, 0)
    x2 = x1_ref[...] + acc_scr[...]
    if final:
        y_ref[...] = _rms(x2, g_ref[...])
    else:
        x2_ref[...] = x2
        xn_ref[...] = _rms(x2, g_ref[...]).astype(BF16)


def _combine(cb, x1, idx, ye, g, final):
    t = x1.shape[0]
    ne, cap = idx.shape
    nbk = t // LANES
    idx2 = idx.reshape(ne * cap // WIN, WIN)
    blk = pl.BlockSpec((TB, D_MODEL), lambda i, cb: (i, 0))
    if final:
        out_specs = blk
        out_shape = jax.ShapeDtypeStruct((t, D_MODEL), F32)
    else:
        out_specs = [blk, blk]
        out_shape = [jax.ShapeDtypeStruct((t, D_MODEL), F32),
                     jax.ShapeDtypeStruct((t, D_MODEL), BF16)]
    grid_spec = pltpu.PrefetchScalarGridSpec(
        num_scalar_prefetch=1,
        grid=(t // TB,),
        in_specs=[
            blk,
            pl.BlockSpec((ne * cap // WIN, WIN), lambda i, cb: (0, 0)),
            pl.BlockSpec(memory_space=pl.ANY),
            pl.BlockSpec((1, D_MODEL), lambda i, cb: (0, 0)),
        ],
        out_specs=out_specs,
        scratch_shapes=[pltpu.VMEM((TB, D_MODEL), F32), pltpu.VMEM((2, WIN, D_MODEL), BF16),
                        pltpu.SemaphoreType.DMA((2,)),
                        pltpu.SMEM((MAX_ITEMS,), I32), pltpu.SMEM((MAX_ITEMS,), I32)],
    )
    return pl.pallas_call(
        functools.partial(_combine_kernel, nbk=nbk, cap=cap, final=final),
        grid_spec=grid_spec,
        out_shape=out_shape,
        compiler_params=_cparams(("arbitrary",)),
        name="combine_final" if final else "combine",
    )(cb.reshape(-1), x1, idx2, ye, g)


def _dft_mats(n, scale):
    k = np.arange(n)
    ang = 2.0 * np.pi * ((k[:, None] * k[None, :]) % n) / n
    return (jnp.asarray(np.cos(ang) * scale, F32).astype(BF16),
            jnp.asarray(np.sin(ang) * scale, F32).astype(BF16))


def _fft1_kernel(x_ref, c1_ref, s1_ref, twc_ref, tws_ref, br_ref, bi_ref):
    for j in range(FFT_J):
        cols = slice(j * D_MODEL, (j + 1) * D_MODEL)
        xj = x_ref[:, cols]
        ar = _dot(c1_ref[...], xj)
        ai = -_dot(s1_ref[...], xj)
        c = twc_ref[0, :, j:j + 1]
        s = tws_ref[0, :, j:j + 1]
        br_ref[:, cols] = (ar * c + ai * s).astype(BF16)
        bi_ref[:, cols] = (ai * c - ar * s).astype(BF16)


def _fft1(xn, nb, n1):
    t = xn.shape[0]
    n = n1 * FFT_N2
    c1, s1 = _dft_mats(n1, 1.0 / math.sqrt(n1))
    k1 = np.arange(n1)[:, None]
    n2 = np.arange(FFT_N2)[None, :]
    ang = 2.0 * np.pi * ((k1 * n2) % n) / n
    nj = FFT_N2 // FFT_J
    to_tab = lambda a: jnp.asarray(a.reshape(n1, nj, FFT_J).transpose(1, 0, 2), F32)
    twc, tws = to_tab(np.cos(ang)), to_tab(np.sin(ang))
    xv = xn.reshape(nb * n1, FFT_N2 * D_MODEL)
    blk = pl.BlockSpec((n1, FFT_J * D_MODEL), lambda b, j: (b, j))
    tws_spec = pl.BlockSpec((1, n1, FFT_J), lambda b, j: (j, 0, 0))
    mat = pl.BlockSpec((n1, n1), lambda b, j: (0, 0))
    br, bi = pl.pallas_call(
        _fft1_kernel,
        grid=(nb, nj),
        in_specs=[blk, mat, mat, tws_spec, tws_spec],
        out_specs=[blk, blk],
        out_shape=[jax.ShapeDtypeStruct(xv.shape, BF16)] * 2,
        compiler_params=_cparams(("arbitrary", "arbitrary")),
        name="fft1",
    )(xv, c1, s1, twc, tws)
    return br.reshape(t, D_MODEL), bi.reshape(t, D_MODEL)


def _fft2_kernel(br_ref, bi_ref, x_ref, c2_ref, s2_ref, cc_ref, sc_ref, wout_ref, mg_ref, wr_ref,
                 x1_ref, xn2_ref, lg_ref, f_scr, xs_scr, x1s_scr, xn2s_scr):
    n2 = FFT_N2
    for j in range(FFT_K1C):
        rows = slice(j * n2, (j + 1) * n2)
        bjr = br_ref[rows, :]
        bji = bi_ref[rows, :]
        yr = (_dot(c2_ref[...], bjr) + _dot(s2_ref[...], bji)).astype(BF16)
        yi = (_dot(c2_ref[...], bji) - _dot(s2_ref[...], bjr)).astype(BF16)
        for g in range(C_GROUPS):
            cols = slice(g * C_GROUP_DIM, (g + 1) * C_GROUP_DIM)
            f = _dot(yr[:, cols], cc_ref[...]) + _dot(yi[:, cols], sc_ref[...])
            f_scr[rows, cols] = f.astype(BF16)
        xs_scr[rows, :] = x_ref[:, j * D_MODEL:(j + 1) * D_MODEL]
    x1 = xs_scr[...] + _dot(f_scr[...], wout_ref[...])
    _router_epilogue(x1, mg_ref, wr_ref, x1s_scr, xn2s_scr, lg_ref)
    for j in range(FFT_K1C):
        rows = slice(j * n2, (j + 1) * n2)
        x1_ref[:, j * D_MODEL:(j + 1) * D_MODEL] = x1s_scr[rows, :]
        xn2_ref[:, j * D_MODEL:(j + 1) * D_MODEL] = xn2s_scr[rows, :]


def _fft2(br, bi, x, nb, n1, wout, mg, wr_t):
    t = x.shape[0]
    n2 = FFT_N2
    c2, s2 = _dft_mats(n2, 1.0 / math.sqrt(n2))
    cc, sc = _dft_mats(C_GROUP_DIM, 1.0 / math.sqrt(C_GROUP_DIM))
    nkc = n1 // FFT_K1C
    rows = FFT_K1C * n2
    xv = x.reshape(nb * n2, n1 * D_MODEL)
    full = lambda shape: pl.BlockSpec(shape, lambda b, k: (0,) * len(shape))
    bspec = pl.BlockSpec((rows, D_MODEL), lambda b, k: (b * nkc + k, 0))
    xspec = pl.BlockSpec((n2, FFT_K1C * D_MODEL), lambda b, k: (b, k))
    x1, xn2, lg = pl.pallas_call(
        _fft2_kernel,
        grid=(nb, nkc),
        in_specs=[bspec, bspec, xspec, full((n2, n2)), full((n2, n2)),
                  full((C_GROUP_DIM, C_GROUP_DIM)), full((C_GROUP_DIM, C_GROUP_DIM)),
                  full((D_MODEL, D_MODEL)), full((1, D_MODEL)), full((N_EXPERTS, D_MODEL))],
        out_specs=[xspec, xspec, pl.BlockSpec((N_EXPERTS, rows), lambda b, k: (0, b * nkc + k))],
        out_shape=[jax.ShapeDtypeStruct(xv.shape, F32), jax.ShapeDtypeStruct(xv.shape, F32),
                   jax.ShapeDtypeStruct((N_EXPERTS, t), F32)],
        scratch_shapes=[pltpu.VMEM((rows, D_MODEL), BF16), pltpu.VMEM((rows, D_MODEL), F32),
                        pltpu.VMEM((rows, D_MODEL), F32), pltpu.VMEM((rows, D_MODEL), F32)],
        compiler_params=_cparams(("arbitrary", "arbitrary")),
        name="fft2",
    )(br, bi, xv, c2, s2, cc, sc, wout, mg, wr_t)
    lg = lg.reshape(N_EXPERTS, nb, n1, n2).transpose(0, 1, 3, 2).reshape(N_EXPERTS, t)
    return x1.reshape(t, D_MODEL), xn2.reshape(t, D_MODEL), lg


def _moe(x1, xn2, logits_t, wg, wu, wd, g_next, final):
    t = x1.shape[0]
    cap = CAPACITY_FACTOR * t // N_EXPERTS
    idx, gate, cb = _route(logits_t, cap)
    ye = _ffn(idx, gate, xn2, wg, wu, wd)
    return _combine(cb, x1, idx, ye, g_next, final)


def _trunk(x3, p):
    nb, s, _ = x3.shape
    x = x3.reshape(nb * s, D_MODEL)
    row = lambda v: v.reshape(1, -1)
    outa, glu = _front(x, row(p["mix_norm"][0]), p["w_in"], row(p["ab_v_norm_g"][0]),
                       row(p["ab_v_norm_b"][0]), p["w_spatial"], p["b_spatial"])
    x1, xn2, lg = _back(x, outa, glu, p["ab_conv_w"][0], row(p["ab_conv_b"][0]),
                        row(p["ab_conv_norm_g"][0]), row(p["ab_conv_norm_b"][0]), p["ab_w_out"],
                        row(p["moe_norm"][0]), p["router_t"][0], s)
    x2, xn = _moe(x1, xn2, lg, p["wg"][0], p["wu"][0], p["wd"][0], row(p["mix_norm"][1]), False)
    n1 = s // FFT_N2
    br, bi = _fft1(xn, nb, n1)
    x1, xn2, lg = _fft2(br, bi, x2, nb, n1, p["c_w_out"], row(p["moe_norm"][1]), p["router_t"][1])
    y = _moe(x1, xn2, lg, p["wg"][1], p["wu"][1], p["wd"][1], row(p["final_norm"]), True)
    return y.reshape(nb, s, D_MODEL)


def kernel(x_prompt, x_sample, mix_norm, ab_w_in, ab_v_norm_g, ab_v_norm_b, ab_w_spatial,
           ab_b_spatial, ab_conv_w, ab_conv_b, ab_conv_norm_g, ab_conv_norm_b, ab_w_out,
           c_w_out, moe_norm, moe_router, moe_w_gate, moe_w_up, moe_w_down, final_norm):
    p = dict(
        mix_norm=mix_norm, moe_norm=moe_norm, final_norm=final_norm,
        w_in=ab_w_in[0].astype(BF16),
        ab_v_norm_g=ab_v_norm_g, ab_v_norm_b=ab_v_norm_b,
        w_spatial=ab_w_spatial[0].astype(BF16),
        b_spatial=jnp.repeat(jnp.transpose(ab_b_spatial[0]), A_HEAD_DIM, axis=1),
        ab_conv_w=ab_conv_w, ab_conv_b=ab_conv_b,
        ab_conv_norm_g=ab_conv_norm_g, ab_conv_norm_b=ab_conv_norm_b,
        ab_w_out=ab_w_out[0].astype(BF16),
        c_w_out=c_w_out[0].astype(BF16),
        router_t=jnp.transpose(moe_router, (0, 2, 1)),
        wg=moe_w_gate.astype(BF16), wu=moe_w_up.astype(BF16), wd=moe_w_down.astype(BF16),
    )
    return (_trunk(x_prompt, p), _trunk(x_sample, p))
```

```python
import functools
import math

import numpy as np
import jax
import jax.numpy as jnp
from jax import lax
from jax.experimental import pallas as pl
from jax.experimental.pallas import tpu as pltpu

F32, BF16, I32 = jnp.float32, jnp.bfloat16, jnp.int32

D_MODEL = 1024
D_A = 512
D_B = 512
A_GROUPS = 4
A_HEAD_DIM = 128
CHUNK = 128
CONV_WIDTH = 31
CONV_PAD = 15
C_GROUPS = 4
C_GROUP_DIM = 256
N_EXPERTS = 16
CAPACITY_FACTOR = 2
RMS_EPS = 1e-6
LN_EPS = 1e-5

LANES = 128
HALO = 16
TM = 512
RB = 512
TB = 512
FFT_N2 = 128
FFT_J = 8
FFT_K1C = 8
VMEM_LIMIT = 48 * 1024 * 1024
BISECT_BITS_STEPS = 31
BISECT_VALUE_STEPS = 30


def _cparams(sem):
    return pltpu.CompilerParams(dimension_semantics=sem, vmem_limit_bytes=VMEM_LIMIT)


def _dot(a, b):
    return jnp.dot(a, b, preferred_element_type=F32)


def _dot_nt(a, b, precision=None):
    return lax.dot_general(a, b, (((1,), (1,)), ((), ())), precision=precision,
                           preferred_element_type=F32)


def _rms(x, g):
    return x * lax.rsqrt(jnp.mean(x * x, axis=-1, keepdims=True) + RMS_EPS) * g


def _ln(x, g, b):
    mu = jnp.mean(x, axis=-1, keepdims=True)
    xc = x - mu
    var = jnp.mean(xc * xc, axis=-1, keepdims=True)
    return xc * lax.rsqrt(var + LN_EPS) * g + b


def _gelu(x):
    return 0.5 * x * (1.0 + lax.erf(x * np.float32(math.sqrt(0.5))))


def _silu(x):
    return x * jax.nn.sigmoid(x)


def _front_kernel(x_ref, g_ref, win_ref, vg_ref, vb_ref, wsp_ref, bsp_ref,
                  outa_ref, glu_ref, zu_scr, vn_scr):
    xn = _rms(x_ref[...], g_ref[...]).astype(BF16)
    zu_scr[...] = _gelu(_dot(xn, win_ref[:, 0:D_A]))
    zv = _gelu(_dot(xn, win_ref[:, D_A:2 * D_A]))
    vn_scr[...] = _ln(zv, vg_ref[...], vb_ref[...]).astype(BF16)
    a_in = _dot(xn, win_ref[:, 2 * D_A:2 * D_A + D_B])
    gate = _dot(xn, win_ref[:, 2 * D_A + D_B:])
    glu_ref[...] = a_in * jax.nn.sigmoid(gate)
    for c in range(TM // CHUNK):
        rows = slice(c * CHUNK, (c + 1) * CHUNK)
        for h in range(A_GROUPS):
            cols = slice(h * A_HEAD_DIM, (h + 1) * A_HEAD_DIM)
            mixed = _dot(wsp_ref[h], vn_scr[rows, cols]) + bsp_ref[:, cols]
            outa_ref[rows, cols] = (zu_scr[rows, cols] * mixed).astype(BF16)


def _front(x, g, w_in, vg, vb, wsp, bsp):
    t = x.shape[0]
    full = lambda shape: pl.BlockSpec(shape, lambda i: (0,) * len(shape))
    return pl.pallas_call(
        _front_kernel,
        grid=(t // TM,),
        in_specs=[
            pl.BlockSpec((TM, D_MODEL), lambda i: (i, 0)),
            full((1, D_MODEL)),
            full((D_MODEL, 2 * D_A + 2 * D_B)),
            full((1, D_A)), full((1, D_A)),
            full((A_GROUPS, CHUNK, CHUNK)),
            full((CHUNK, D_A)),
        ],
        out_specs=[pl.BlockSpec((TM, D_A), lambda i: (i, 0)),
                   pl.BlockSpec((TM, D_B), lambda i: (i, 0))],
        out_shape=[jax.ShapeDtypeStruct((t, D_A), BF16),
                   jax.ShapeDtypeStruct((t, D_B), F32)],
        scratch_shapes=[pltpu.VMEM((TM, D_A), F32), pltpu.VMEM((TM, D_A), BF16)],
        compiler_params=_cparams(("arbitrary",)),
        name="front",
    )(x, g, w_in, vg, vb, wsp, bsp)


CONV_ROWS = 64


def _router_logits(x1, mg_ref, wr_ref):
    return _dot_nt(wr_ref[...], _rms(x1, mg_ref[...]), precision=lax.Precision.HIGHEST)


def _back_kernel(x_ref, outa_ref, glu_ref, prev_ref, next_ref, cw_ref, cb_ref, ng_ref, nb_ref,
                 wout_ref, mg_ref, wr_ref, x1_ref, lg_ref, ext_scr, ob_scr,
                 *, blocks_per_seq):
    i = pl.program_id(0)
    pos = lax.rem(i, blocks_per_seq)
    ext_scr[0:HALO, :] = jnp.where(pos == 0, 0.0, prev_ref[...])
    ext_scr[HALO:HALO + TM, :] = glu_ref[...]
    ext_scr[HALO + TM:, :] = jnp.where(pos == blocks_per_seq - 1, 0.0, next_ref[...])
    for r in range(TM // CONV_ROWS):
        base = r * CONV_ROWS + HALO - CONV_PAD
        acc = jnp.zeros((CONV_ROWS, D_B), F32) + cb_ref[...]
        for k in range(CONV_WIDTH):
            acc = acc + cw_ref[k:k + 1, :] * ext_scr[base + k:base + k + CONV_ROWS, :]
        ob = _silu(_ln(acc, ng_ref[...], nb_ref[...]))
        ob_scr[r * CONV_ROWS:(r + 1) * CONV_ROWS, :] = ob.astype(BF16)
    y = _dot(outa_ref[...], wout_ref[0:D_A, :]) + _dot(ob_scr[...], wout_ref[D_A:, :])
    x1 = x_ref[...] + y
    x1_ref[...] = x1
    lg_ref[...] = _router_logits(x1, mg_ref, wr_ref)


def _back(x, outa, glu, cw, cb, ng, nb_, wout, mg, wr_t, seq_len):
    t = x.shape[0]
    full = lambda shape: pl.BlockSpec(shape, lambda i: (0,) * len(shape))
    hb = TM // HALO
    return pl.pallas_call(
        functools.partial(_back_kernel, blocks_per_seq=seq_len // TM),
        grid=(t // TM,),
        in_specs=[
            pl.BlockSpec((TM, D_MODEL), lambda i: (i, 0)),
            pl.BlockSpec((TM, D_A), lambda i: (i, 0)),
            pl.BlockSpec((TM, D_B), lambda i: (i, 0)),
            pl.BlockSpec((HALO, D_B), lambda i: (jnp.maximum(i * hb - 1, 0), 0)),
            pl.BlockSpec((HALO, D_B), lambda i: (jnp.minimum((i + 1) * hb, t // HALO - 1), 0)),
            full((CONV_WIDTH, D_B)), full((1, D_B)), full((1, D_B)), full((1, D_B)),
            full((D_A + D_B, D_MODEL)),
            full((1, D_MODEL)),
            full((N_EXPERTS, D_MODEL)),
        ],
        out_specs=[pl.BlockSpec((TM, D_MODEL), lambda i: (i, 0)),
                   pl.BlockSpec((N_EXPERTS, TM), lambda i: (0, i))],
        out_shape=[jax.ShapeDtypeStruct((t, D_MODEL), F32),
                   jax.ShapeDtypeStruct((N_EXPERTS, t), F32)],
        scratch_shapes=[pltpu.VMEM((TM + 2 * HALO, D_B), F32), pltpu.VMEM((TM, D_B), BF16)],
        compiler_params=_cparams(("arbitrary",)),
        name="back",
    )(x, outa, glu, glu, glu, cw, cb, ng, nb_, wout, mg, wr_t)


def _split3(a):
    a1 = a.astype(BF16)
    r1 = a - a1.astype(F32)
    a2 = r1.astype(BF16)
    a3 = (r1 - a2.astype(F32)).astype(BF16)
    return a1, a2, a3


def _route_kernel(lg_ref, idx_ref, gate_ref, cb_ref, pos_ref, lm_scr, w_scr, a1_scr, a2_scr, a3_scr,
                  cnt_scr, cbr_scr, *, nbk, cap):
    ne = N_EXPERTS
    lg = lg_ref[...]
    ex = jnp.exp(lg - jnp.max(lg, axis=0, keepdims=True))
    aff = ex / jnp.sum(ex, axis=0, keepdims=True)

    def count(m):
        s = jnp.sum(m.astype(F32), axis=1, keepdims=True)
        return jnp.sum(s, axis=2, keepdims=True)

    def update(mid, lo, hi):
        ge = count(aff >= mid) >= cap
        return jnp.where(ge, mid, lo), jnp.where(ge, hi, mid)

    def bits_body(_, c):
        lo, hi = c
        lo_i = lax.bitcast_convert_type(lo, I32)
        hi_i = lax.bitcast_convert_type(hi, I32)
        mid = lax.bitcast_convert_type(lo_i + lax.shift_right_logical(hi_i - lo_i, 1), F32)
        return update(mid, lo, hi)

    def value_body(_, c):
        lo, hi = c
        return update(lo + (hi - lo) * 0.5, lo, hi)

    lo0 = jnp.zeros((ne, 1, 1), F32)
    hi0 = jnp.full((ne, 1, 1), jnp.inf, F32)
    c = lax.fori_loop(0, BISECT_BITS_STEPS, bits_body, (lo0, hi0))
    lo, hi = lax.fori_loop(0, BISECT_VALUE_STEPS, value_body, c)

    li = lax.broadcasted_iota(I32, (LANES, LANES), 0)
    lj = lax.broadcasted_iota(I32, (LANES, LANES), 1)
    ut = (li <= lj).astype(BF16)
    ones = jnp.ones((LANES, LANES), BF16)
    bi = lax.broadcasted_iota(I32, (nbk, nbk), 0)
    bj = lax.broadcasted_iota(I32, (nbk, nbk), 1)
    slt = (bj < bi).astype(BF16)
    utb = (bi <= bj).astype(BF16)
    ones8 = jnp.ones((8, LANES), BF16)

    def block_cum(m):
        m2 = m.astype(BF16).reshape(ne * nbk, LANES)
        loc = _dot(m2, ut).reshape(ne, nbk, LANES)
        tot = _dot(m2, ones).astype(BF16).reshape(ne, nbk, LANES)
        off = jnp.stack([_dot(slt, tot[e]) for e in range(ne)])
        return loc, tot, off

    gt = aff >= hi
    eq = (aff >= lo) & (aff < hi)
    need = np.float32(cap) - count(gt)
    eloc, _, eoff = block_cum(eq)
    mask = gt | (eq & (eloc + eoff <= need))

    mloc, mtot, moff = block_cum(mask)
    pos_ref[...] = jnp.where(mask, mloc + moff - 1.0, -1.0).astype(I32)
    lm_scr[...] = mloc.astype(BF16)
    w_scr[...] = mtot
    a1, a2, a3 = _split3(aff)
    a1_scr[...] = a1
    a2_scr[...] = a2
    a3_scr[...] = a3
    maskb = mask.astype(BF16)
    for e in range(ne):
        cnt = _dot_nt(ones8, maskb[e])
        cnt_scr[e] = cnt
        cbr_scr[e] = _dot(cnt.astype(BF16), utb)
    cb_ref[...] = cbr_scr[:, 0, :].astype(I32)

    s_col = lax.broadcasted_iota(I32, (cap, 1), 0).astype(F32)
    lane_b = lax.broadcasted_iota(I32, (cap, nbk), 1).astype(F32)
    lane_l = lax.broadcasted_iota(I32, (cap, LANES), 1).astype(F32)
    ones_b = jnp.ones((nbk, LANES), BF16)
    ones8b = jnp.ones((8, nbk), BF16)

    def per_expert(e, _):
        cbrow = cbr_scr[e][0:1, :]
        le = (cbrow <= s_col).astype(BF16)
        blk_c = _dot(le, ones_b)
        start_c = _dot(le, w_scr[e])
        oh = (lane_b == blk_c[:, 0:nbk]).astype(BF16)
        g_loc = _dot(oh, lm_scr[e])
        le2 = (g_loc <= s_col - start_c).astype(BF16)
        within_c = _dot(le2, ones)
        blk_r = _dot_nt(ones8b, le)
        within_r = _dot_nt(ones8, le2)
        idx_ref[pl.ds(e, 1), :] = (blk_r * LANES + within_r)[0:1, :].astype(I32)
        pick = lane_l == within_c
        gsum = None
        for a_scr in (a1_scr, a2_scr, a3_scr):
            sel = jnp.where(pick, _dot(oh, a_scr[e]), 0.0).astype(BF16)
            part = _dot_nt(ones8, sel)
            gsum = part if gsum is None else gsum + part
        gate_ref[pl.ds(e, 1), :] = gsum[0:1, :]
        return 0

    lax.fori_loop(0, ne, per_expert, 0)


def _route(logits_t, cap):
    ne, t = logits_t.shape
    nbk = t // LANES
    lg3 = logits_t.reshape(ne, nbk, LANES)
    full = lambda shape: pl.BlockSpec(shape, lambda i: (0,) * len(shape))
    tab = lambda dt: pltpu.VMEM((ne, nbk, LANES), dt)
    return pl.pallas_call(
        functools.partial(_route_kernel, nbk=nbk, cap=cap),
        grid=(1,),
        in_specs=[full((ne, nbk, LANES))],
        out_specs=[full((ne, cap)), full((ne, cap)), full((ne, nbk)), full((ne, nbk, LANES))],
        out_shape=[jax.ShapeDtypeStruct((ne, cap), I32),
                   jax.ShapeDtypeStruct((ne, cap), F32),
                   jax.ShapeDtypeStruct((ne, nbk), I32),
                   jax.ShapeDtypeStruct((ne, nbk, LANES), I32)],
        scratch_shapes=[tab(BF16), tab(BF16), tab(BF16), tab(BF16), tab(BF16),
                        pltpu.VMEM((ne, 8, nbk), F32), pltpu.VMEM((ne, 8, nbk), F32)],
        compiler_params=_cparams(("arbitrary",)),
        name="route",
    )(lg3)


def _ffn_kernel(idx_ref, idxn_ref, gate_ref, x_hbm, mg_ref, wg_ref, wu_ref, wd_ref, ye_ref,
                xbuf, sem, *, nsteps, rb):
    step = pl.program_id(0) * pl.num_programs(1) + pl.program_id(1)
    slot = lax.rem(step, 2)

    def issue(ids_ref, s):
        for r in range(rb):
            pltpu.make_async_copy(x_hbm.at[pl.ds(ids_ref[0, 0, r], 1)], xbuf.at[s, pl.ds(r, 1)],
                                  sem.at[s]).start()

    def wait(s):
        pltpu.make_async_copy(x_hbm.at[pl.ds(0, rb)], xbuf.at[s], sem.at[s]).wait()

    @pl.when(step == 0)
    def _():
        issue(idx_ref, 0)

    issue(idxn_ref, 1 - slot)
    wait(slot)

    xe = _rms(xbuf[slot], mg_ref[...]).astype(BF16)
    hid = (_silu(_dot(xe, wg_ref[0])) * _dot(xe, wu_ref[0])).astype(BF16)
    ye = _dot(hid, wd_ref[0])
    ri = lax.broadcasted_iota(I32, (rb, rb), 0)
    ci = lax.broadcasted_iota(I32, (rb, rb), 1)
    gcol = jnp.sum(jnp.where(ri == ci, gate_ref[0], 0.0), axis=1, keepdims=True)
    ye_ref[...] = (ye * gcol).astype(BF16)

    @pl.when(step == nsteps - 1)
    def _():
        wait(1 - slot)


def _ffn(idx, gate, x1, mg, wg, wu, wd):
    ne, cap = idx.shape
    rb = min(RB, cap)
    nblk = cap // rb
    nsteps = ne * nblk
    idx3 = idx.reshape(nsteps, 1, rb)
    gate3 = gate.reshape(nsteps, 1, rb)
    wspec = pl.BlockSpec((1, D_MODEL, D_MODEL), lambda e, j: (e, 0, 0))
    return pl.pallas_call(
        functools.partial(_ffn_kernel, nsteps=nsteps, rb=rb),
        grid=(ne, nblk),
        in_specs=[
            pl.BlockSpec((1, 1, rb), lambda e, j: (e * nblk + j, 0, 0), memory_space=pltpu.SMEM),
            pl.BlockSpec((1, 1, rb), lambda e, j: (jnp.minimum(e * nblk + j + 1, nsteps - 1), 0, 0),
                         memory_space=pltpu.SMEM),
            pl.BlockSpec((1, 1, rb), lambda e, j: (e * nblk + j, 0, 0)),
            pl.BlockSpec(memory_space=pl.ANY),
            pl.BlockSpec((1, D_MODEL), lambda e, j: (0, 0)),
            wspec, wspec, wspec,
        ],
        out_specs=pl.BlockSpec((rb, D_MODEL), lambda e, j: (e * nblk + j, 0)),
        out_shape=jax.ShapeDtypeStruct((ne * cap, D_MODEL), BF16),
        scratch_shapes=[pltpu.VMEM((2, rb, D_MODEL), F32), pltpu.SemaphoreType.DMA((2,))],
        compiler_params=_cparams(("arbitrary", "arbitrary")),
        name="ffn",
    )(idx3, idx3, gate3, x1, mg, wg, wu, wd)


CHUNK_ROWS = 32
GROUP_CHUNKS = 8
GROUP_ROWS = CHUNK_ROWS * GROUP_CHUNKS
MAX_CHUNKS = N_EXPERTS * TB // CHUNK_ROWS + 2 * N_EXPERTS + GROUP_CHUNKS
NO_SLOT = 1 << 30


def _combine_kernel(cb_ref, x1_ref, pos_ref, ye_hbm, *rest, nbk, cap, final):
    if final:
        g_ref, out_ref, acc_scr, ybuf, sem, ie_scr, ir_scr, ic_scr = rest
    else:
        out_ref, acc_scr, ybuf, sem, ie_scr, ir_scr, ic_scr = rest
    tb = pl.program_id(0)
    bpt = TB // LANES

    n = jnp.int32(0)
    for e in range(N_EXPERTS):
        b0 = e * nbk + tb * bpt
        lo = jnp.where(tb == 0, 0, cb_ref[jnp.maximum(b0 - 1, 0)])
        hi = cb_ref[b0 + bpt - 1]
        start = lax.shift_left(lax.shift_right_logical(lo, 5), 5)
        nch = jnp.where(hi > lo, lax.shift_right_logical(hi - start + (CHUNK_ROWS - 1), 5), 0)

        def push(c, k, e=e, start=start):
            ie_scr[k] = e
            ir_scr[k] = start + c * CHUNK_ROWS
            ic_scr[k] = start + c * CHUNK_ROWS
            return k + 1

        n = lax.fori_loop(0, nch, push, n)
    ngroups = lax.shift_right_logical(n + (GROUP_CHUNKS - 1), 3)

    def pad(k, _):
        ie_scr[k] = 0
        ir_scr[k] = 0
        ic_scr[k] = NO_SLOT
        return 0

    lax.fori_loop(n, ngroups * GROUP_CHUNKS, pad, 0)

    def copies(g, s):
        out = []
        for c in range(GROUP_CHUNKS):
            k = g * GROUP_CHUNKS + c
            row0 = pl.multiple_of(ie_scr[k] * cap + ir_scr[k], CHUNK_ROWS)
            out.append(pltpu.make_async_copy(ye_hbm.at[pl.ds(row0, CHUNK_ROWS)],
                                             ybuf.at[s, pl.ds(c * CHUNK_ROWS, CHUNK_ROWS)],
                                             sem.at[s]))
        return out

    acc_scr[...] = jnp.zeros_like(acc_scr)

    @pl.when(ngroups > 0)
    def _():
        for cp in copies(0, 0):
            cp.start()

    jrow = lax.broadcasted_iota(I32, (CHUNK_ROWS, TB), 0)

    def body(g, _):
        s = lax.rem(g, 2)

        @pl.when(g + 1 < ngroups)
        def _():
            for cp in copies(g + 1, 1 - s):
                cp.start()

        pltpu.make_async_copy(ye_hbm.at[pl.ds(0, GROUP_ROWS)], ybuf.at[s], sem.at[s]).wait()
        pieces = []
        for c in range(GROUP_CHUNKS):
            k = g * GROUP_CHUNKS + c
            prow = pos_ref[pl.ds(ie_scr[k], 1), :]
            pieces.append((prow == jrow + ic_scr[k]).astype(BF16))
        onehot_t = jnp.concatenate(pieces, axis=0)
        acc_scr[...] += lax.dot_general(onehot_t, ybuf[s], (((0,), (0,)), ((), ())),
                                        preferred_element_type=F32)
        return 0

    lax.fori_loop(0, ngroups, body, 0)
    x2 = x1_ref[...] + acc_scr[...]
    out_ref[...] = _rms(x2, g_ref[...]) if final else x2


def _combine(cb, pos, x1, ye, g_final=None):
    t = x1.shape[0]
    ne, nbk = cb.shape
    cap = ye.shape[0] // ne
    final = g_final is not None
    blk = pl.BlockSpec((TB, D_MODEL), lambda i, cb: (i, 0))
    in_specs = [blk, pl.BlockSpec((ne, TB), lambda i, cb: (0, i)), pl.BlockSpec(memory_space=pl.ANY)]
    args = [cb.reshape(-1), x1, pos.reshape(ne, t), ye]
    if final:
        in_specs.append(pl.BlockSpec((1, D_MODEL), lambda i, cb: (0, 0)))
        args.append(g_final)
    grid_spec = pltpu.PrefetchScalarGridSpec(
        num_scalar_prefetch=1,
        grid=(t // TB,),
        in_specs=in_specs,
        out_specs=blk,
        scratch_shapes=[pltpu.VMEM((TB, D_MODEL), F32),
                        pltpu.VMEM((2, GROUP_ROWS, D_MODEL), BF16),
                        pltpu.SemaphoreType.DMA((2,)),
                        pltpu.SMEM((MAX_CHUNKS,), I32), pltpu.SMEM((MAX_CHUNKS,), I32),
                        pltpu.SMEM((MAX_CHUNKS,), I32)],
    )
    return pl.pallas_call(
        functools.partial(_combine_kernel, nbk=nbk, cap=cap, final=final),
        grid_spec=grid_spec,
        out_shape=jax.ShapeDtypeStruct((t, D_MODEL), F32),
        compiler_params=_cparams(("arbitrary",)),
        name="combine_final" if final else "combine",
    )(*args)


def _dft_mats(n, scale):
    k = np.arange(n)
    ang = 2.0 * np.pi * ((k[:, None] * k[None, :]) % n) / n
    return (jnp.asarray(np.cos(ang) * scale, F32).astype(BF16),
            jnp.asarray(np.sin(ang) * scale, F32).astype(BF16))


def _fft1_kernel(x_ref, g_ref, c1_ref, s1_ref, twc_ref, tws_ref, br_ref, bi_ref):
    for j in range(FFT_J):
        xj = _rms(x_ref[:, j, :], g_ref[...]).astype(BF16)
        ar = _dot(c1_ref[...], xj)
        ai = -_dot(s1_ref[...], xj)
        c = twc_ref[0, :, j:j + 1]
        s = tws_ref[0, :, j:j + 1]
        br_ref[:, j, :] = ar * c + ai * s
        bi_ref[:, j, :] = ai * c - ar * s


def _fft1(x, g, nb, n1):
    t = x.shape[0]
    n = n1 * FFT_N2
    c1, s1 = _dft_mats(n1, 1.0 / math.sqrt(n1))
    k1 = np.arange(n1)[:, None]
    n2 = np.arange(FFT_N2)[None, :]
    ang = 2.0 * np.pi * ((k1 * n2) % n) / n
    nj = FFT_N2 // FFT_J
    to_tab = lambda a: jnp.asarray(a.reshape(n1, nj, FFT_J).transpose(1, 0, 2), F32)
    twc, tws = to_tab(np.cos(ang)), to_tab(np.sin(ang))
    xv = x.reshape(nb * n1, FFT_N2, D_MODEL)
    blk = pl.BlockSpec((n1, FFT_J, D_MODEL), lambda b, j: (b, j, 0))
    tws_spec = pl.BlockSpec((1, n1, FFT_J), lambda b, j: (j, 0, 0))
    mat = pl.BlockSpec((n1, n1), lambda b, j: (0, 0))
    br, bi = pl.pallas_call(
        _fft1_kernel,
        grid=(nb, nj),
        in_specs=[blk, pl.BlockSpec((1, D_MODEL), lambda b, j: (0, 0)), mat, mat,
                  tws_spec, tws_spec],
        out_specs=[blk, blk],
        out_shape=[jax.ShapeDtypeStruct(xv.shape, F32)] * 2,
        compiler_params=_cparams(("arbitrary", "arbitrary")),
        name="fft1",
    )(xv, g, c1, s1, twc, tws)
    return br.reshape(t, D_MODEL), bi.reshape(t, D_MODEL)


FFT_PAIR = 2


def _fft2_kernel(br_ref, bi_ref, x_ref, c2_ref, s2_ref, cc_ref, sc_ref, wout_ref, mg_ref, wr_ref,
                 x1_ref, lg_ref, f_scr):
    n2 = FFT_N2
    for jp in range(FFT_K1C // FFT_PAIR):
        for q in range(FFT_PAIR):
            j = jp * FFT_PAIR + q
            rows = slice(j * n2, (j + 1) * n2)
            bjr = br_ref[rows, :].astype(BF16)
            bji = bi_ref[rows, :].astype(BF16)
            yr = (_dot(c2_ref[...], bjr) + _dot(s2_ref[...], bji)).astype(BF16)
            yi = (_dot(c2_ref[...], bji) - _dot(s2_ref[...], bjr)).astype(BF16)
            for g in range(C_GROUPS):
                cols = slice(g * C_GROUP_DIM, (g + 1) * C_GROUP_DIM)
                f = _dot(yr[:, cols], cc_ref[...]) + _dot(yi[:, cols], sc_ref[...])
                f_scr[q * n2:(q + 1) * n2, cols] = f.astype(BF16)
        y = _dot(f_scr[...], wout_ref[...])
        for q in range(FFT_PAIR):
            j = jp * FFT_PAIR + q
            x1 = x_ref[:, j, :] + y[q * n2:(q + 1) * n2, :]
            x1_ref[:, j, :] = x1
            lg_ref[:, j * n2:(j + 1) * n2] = _router_logits(x1, mg_ref, wr_ref)


def _fft2(br, bi, x, nb, n1, wout, mg, wr_t):
    t = x.shape[0]
    n2 = FFT_N2
    c2, s2 = _dft_mats(n2, 1.0 / math.sqrt(n2))
    cc, sc = _dft_mats(C_GROUP_DIM, 1.0 / math.sqrt(C_GROUP_DIM))
    nkc = n1 // FFT_K1C
    rows = FFT_K1C * n2
    xv = x.reshape(nb * n2, n1, D_MODEL)
    full = lambda shape: pl.BlockSpec(shape, lambda b, k: (0,) * len(shape))
    bspec = pl.BlockSpec((rows, D_MODEL), lambda b, k: (b * nkc + k, 0))
    xspec = pl.BlockSpec((n2, FFT_K1C, D_MODEL), lambda b, k: (b, k, 0))
    x1, lg = pl.pallas_call(
        _fft2_kernel,
        grid=(nb, nkc),
        in_specs=[bspec, bspec, xspec, full((n2, n2)), full((n2, n2)),
                  full((C_GROUP_DIM, C_GROUP_DIM)), full((C_GROUP_DIM, C_GROUP_DIM)),
                  full((D_MODEL, D_MODEL)), full((1, D_MODEL)), full((N_EXPERTS, D_MODEL))],
        out_specs=[xspec, pl.BlockSpec((N_EXPERTS, rows), lambda b, k: (0, b * nkc + k))],
        out_shape=[jax.ShapeDtypeStruct(xv.shape, F32), jax.ShapeDtypeStruct((N_EXPERTS, t), F32)],
        scratch_shapes=[pltpu.VMEM((FFT_PAIR * n2, D_MODEL), BF16)],
        compiler_params=_cparams(("arbitrary", "arbitrary")),
        name="fft2",
    )(br, bi, xv, c2, s2, cc, sc, wout, mg, wr_t)
    lg = lg.reshape(N_EXPERTS, nb, n1, n2).transpose(0, 1, 3, 2).reshape(N_EXPERTS, t)
    return x1.reshape(t, D_MODEL), lg


def _moe(x1, logits_t, mg, wg, wu, wd, g_final=None):
    t = x1.shape[0]
    cap = CAPACITY_FACTOR * t // N_EXPERTS
    idx, gate, cb, pos = _route(logits_t, cap)
    ye = _ffn(idx, gate, x1, mg, wg, wu, wd)
    return _combine(cb, pos, x1, ye, g_final)


def _trunk(x3, p):
    nb, s, _ = x3.shape
    x = x3.reshape(nb * s, D_MODEL)
    row = lambda v: v.reshape(1, -1)
    outa, glu = _front(x, row(p["mix_norm"][0]), p["w_in"], row(p["ab_v_norm_g"][0]),
                       row(p["ab_v_norm_b"][0]), p["w_spatial"], p["b_spatial"])
    mg0, mg1 = row(p["moe_norm"][0]), row(p["moe_norm"][1])
    x1, lg = _back(x, outa, glu, p["ab_conv_w"][0], row(p["ab_conv_b"][0]),
                   row(p["ab_conv_norm_g"][0]), row(p["ab_conv_norm_b"][0]), p["ab_w_out"],
                   mg0, p["router_t"][0], s)
    x2 = _moe(x1, lg, mg0, p["wg"][0], p["wu"][0], p["wd"][0])
    n1 = s // FFT_N2
    br, bi = _fft1(x2, row(p["mix_norm"][1]), nb, n1)
    x1, lg = _fft2(br, bi, x2, nb, n1, p["c_w_out"], mg1, p["router_t"][1])
    y = _moe(x1, lg, mg1, p["wg"][1], p["wu"][1], p["wd"][1], row(p["final_norm"]))
    return y.reshape(nb, s, D_MODEL)


def kernel(x_prompt, x_sample, mix_norm, ab_w_in, ab_v_norm_g, ab_v_norm_b, ab_w_spatial,
           ab_b_spatial, ab_conv_w, ab_conv_b, ab_conv_norm_g, ab_conv_norm_b, ab_w_out,
           c_w_out, moe_norm, moe_router, moe_w_gate, moe_w_up, moe_w_down, final_norm):
    p = dict(
        mix_norm=mix_norm, moe_norm=moe_norm, final_norm=final_norm,
        w_in=ab_w_in[0].astype(BF16),
        ab_v_norm_g=ab_v_norm_g, ab_v_norm_b=ab_v_norm_b,
        w_spatial=ab_w_spatial[0].astype(BF16),
        b_spatial=jnp.repeat(jnp.transpose(ab_b_spatial[0]), A_HEAD_DIM, axis=1),
        ab_conv_w=ab_conv_w, ab_conv_b=ab_conv_b,
        ab_conv_norm_g=ab_conv_norm_g, ab_conv_norm_b=ab_conv_norm_b,
        ab_w_out=ab_w_out[0].astype(BF16),
        c_w_out=c_w_out[0].astype(BF16),
        router_t=jnp.transpose(moe_router, (0, 2, 1)),
        wg=moe_w_gate.astype(BF16), wu=moe_w_up.astype(BF16), wd=moe_w_down.astype(BF16),
    )
    return (_trunk(x_prompt, p), _trunk(x_sample, p))
```

```python
import functools
import math

import numpy as np
import jax
import jax.numpy as jnp
from jax import lax
from jax.experimental import pallas as pl
from jax.experimental.pallas import tpu as pltpu

F32, BF16, I32 = jnp.float32, jnp.bfloat16, jnp.int32

D_MODEL = 1024
D_A = 512
D_B = 512
A_GROUPS = 4
A_HEAD_DIM = 128
CHUNK = 128
CONV_WIDTH = 31
CONV_PAD = 15
C_GROUPS = 4
C_GROUP_DIM = 256
N_EXPERTS = 16
CAPACITY_FACTOR = 2
RMS_EPS = 1e-6
LN_EPS = 1e-5

LANES = 128
HALO = 16
TM = 512
RB = 512
TB = 512
FFT_N2 = 128
FFT_J = 8
FFT_K1C = 8
VMEM_LIMIT = 48 * 1024 * 1024
BISECT_BITS_STEPS = 31
BISECT_VALUE_STEPS = 30


def _cparams(sem):
    return pltpu.CompilerParams(dimension_semantics=sem, vmem_limit_bytes=VMEM_LIMIT)


def _dot(a, b):
    return jnp.dot(a, b, preferred_element_type=F32)


def _dot_nt(a, b, precision=None):
    return lax.dot_general(a, b, (((1,), (1,)), ((), ())), precision=precision,
                           preferred_element_type=F32)


def _rms(x, g):
    return x * lax.rsqrt(jnp.mean(x * x, axis=-1, keepdims=True) + RMS_EPS) * g


def _ln(x, g, b):
    mu = jnp.mean(x, axis=-1, keepdims=True)
    xc = x - mu
    var = jnp.mean(xc * xc, axis=-1, keepdims=True)
    return xc * lax.rsqrt(var + LN_EPS) * g + b


def _gelu(x):
    return 0.5 * x * (1.0 + lax.erf(x * np.float32(math.sqrt(0.5))))


def _silu(x):
    return x * jax.nn.sigmoid(x)


def _front_kernel(x_ref, g_ref, win_ref, vg_ref, vb_ref, wsp_ref, bsp_ref,
                  outa_ref, glu_ref, zu_scr, vn_scr):
    xn = _rms(x_ref[...], g_ref[...]).astype(BF16)
    zu_scr[...] = _gelu(_dot(xn, win_ref[:, 0:D_A]))
    zv = _gelu(_dot(xn, win_ref[:, D_A:2 * D_A]))
    vn_scr[...] = _ln(zv, vg_ref[...], vb_ref[...]).astype(BF16)
    a_in = _dot(xn, win_ref[:, 2 * D_A:2 * D_A + D_B])
    gate = _dot(xn, win_ref[:, 2 * D_A + D_B:])
    glu_ref[...] = a_in * jax.nn.sigmoid(gate)
    for c in range(TM // CHUNK):
        rows = slice(c * CHUNK, (c + 1) * CHUNK)
        for h in range(A_GROUPS):
            cols = slice(h * A_HEAD_DIM, (h + 1) * A_HEAD_DIM)
            mixed = _dot(wsp_ref[h], vn_scr[rows, cols]) + bsp_ref[:, cols]
            outa_ref[rows, cols] = (zu_scr[rows, cols] * mixed).astype(BF16)


def _front(x, g, w_in, vg, vb, wsp, bsp):
    t = x.shape[0]
    full = lambda shape: pl.BlockSpec(shape, lambda i: (0,) * len(shape))
    return pl.pallas_call(
        _front_kernel,
        grid=(t // TM,),
        in_specs=[
            pl.BlockSpec((TM, D_MODEL), lambda i: (i, 0)),
            full((1, D_MODEL)),
            full((D_MODEL, 2 * D_A + 2 * D_B)),
            full((1, D_A)), full((1, D_A)),
            full((A_GROUPS, CHUNK, CHUNK)),
            full((CHUNK, D_A)),
        ],
        out_specs=[pl.BlockSpec((TM, D_A), lambda i: (i, 0)),
                   pl.BlockSpec((TM, D_B), lambda i: (i, 0))],
        out_shape=[jax.ShapeDtypeStruct((t, D_A), BF16),
                   jax.ShapeDtypeStruct((t, D_B), F32)],
        scratch_shapes=[pltpu.VMEM((TM, D_A), F32), pltpu.VMEM((TM, D_A), BF16)],
        compiler_params=_cparams(("arbitrary",)),
        name="front",
    )(x, g, w_in, vg, vb, wsp, bsp)


CONV_ROWS = 64
SUBLANES = 8
SHIFT_ROWS = TM + (HALO + CONV_PAD) // SUBLANES * SUBLANES


def _router_logits(x1, mg_ref, wr_ref):
    return _dot_nt(wr_ref[...], _rms(x1, mg_ref[...]), precision=lax.Precision.HIGHEST)


def _back_kernel(x_ref, outa_ref, glu_ref, prev_ref, next_ref, cw_ref, cb_ref, ng_ref, nb_ref,
                 wout_ref, mg_ref, wr_ref, x1_ref, lg_ref, ext_scr, sh_scr, ob_scr,
                 *, blocks_per_seq):
    i = pl.program_id(0)
    pos = lax.rem(i, blocks_per_seq)
    ext_scr[0:HALO, :] = jnp.where(pos == 0, 0.0, prev_ref[...])
    ext_scr[HALO:HALO + TM, :] = glu_ref[...]
    ext_scr[HALO + TM:, :] = jnp.where(pos == blocks_per_seq - 1, 0.0, next_ref[...])
    for b in range(SUBLANES):
        sh_scr[b] = ext_scr[b:b + SHIFT_ROWS, :]
    for r in range(TM // CONV_ROWS):
        acc = jnp.zeros((CONV_ROWS, D_B), F32) + cb_ref[...]
        for k in range(CONV_WIDTH):
            a, b = divmod(HALO - CONV_PAD + k, SUBLANES)
            row0 = r * CONV_ROWS + a * SUBLANES
            acc = acc + cw_ref[k:k + 1, :] * sh_scr[b, row0:row0 + CONV_ROWS, :]
        ob = _silu(_ln(acc, ng_ref[...], nb_ref[...]))
        ob_scr[r * CONV_ROWS:(r + 1) * CONV_ROWS, :] = ob.astype(BF16)
    y = _dot(outa_ref[...], wout_ref[0:D_A, :]) + _dot(ob_scr[...], wout_ref[D_A:, :])
    x1 = x_ref[...] + y
    x1_ref[...] = x1
    lg_ref[...] = _router_logits(x1, mg_ref, wr_ref)


def _back(x, outa, glu, cw, cb, ng, nb_, wout, mg, wr_t, seq_len):
    t = x.shape[0]
    full = lambda shape: pl.BlockSpec(shape, lambda i: (0,) * len(shape))
    hb = TM // HALO
    return pl.pallas_call(
        functools.partial(_back_kernel, blocks_per_seq=seq_len // TM),
        grid=(t // TM,),
        in_specs=[
            pl.BlockSpec((TM, D_MODEL), lambda i: (i, 0)),
            pl.BlockSpec((TM, D_A), lambda i: (i, 0)),
            pl.BlockSpec((TM, D_B), lambda i: (i, 0)),
            pl.BlockSpec((HALO, D_B), lambda i: (jnp.maximum(i * hb - 1, 0), 0)),
            pl.BlockSpec((HALO, D_B), lambda i: (jnp.minimum((i + 1) * hb, t // HALO - 1), 0)),
            full((CONV_WIDTH, D_B)), full((1, D_B)), full((1, D_B)), full((1, D_B)),
            full((D_A + D_B, D_MODEL)),
            full((1, D_MODEL)),
            full((N_EXPERTS, D_MODEL)),
        ],
        out_specs=[pl.BlockSpec((TM, D_MODEL), lambda i: (i, 0)),
                   pl.BlockSpec((N_EXPERTS, TM), lambda i: (0, i))],
        out_shape=[jax.ShapeDtypeStruct((t, D_MODEL), F32),
                   jax.ShapeDtypeStruct((N_EXPERTS, t), F32)],
        scratch_shapes=[pltpu.VMEM((TM + 2 * HALO, D_B), F32),
                        pltpu.VMEM((SUBLANES, SHIFT_ROWS, D_B), F32),
                        pltpu.VMEM((TM, D_B), BF16)],
        compiler_params=_cparams(("arbitrary",)),
        name="back",
    )(x, outa, glu, glu, glu, cw, cb, ng, nb_, wout, mg, wr_t)


def _split3(a):
    a1 = a.astype(BF16)
    r1 = a - a1.astype(F32)
    a2 = r1.astype(BF16)
    a3 = (r1 - a2.astype(F32)).astype(BF16)
    return a1, a2, a3


def _route_kernel(lg_ref, idx_ref, gate_ref, cb_ref, pos_ref, lm_scr, w_scr, a1_scr, a2_scr, a3_scr,
                  cnt_scr, cbr_scr, *, nbk, cap):
    ne = N_EXPERTS
    lg = lg_ref[...]
    ex = jnp.exp(lg - jnp.max(lg, axis=0, keepdims=True))
    aff = ex / jnp.sum(ex, axis=0, keepdims=True)

    def count(m):
        s = jnp.sum(m.astype(F32), axis=1, keepdims=True)
        return jnp.sum(s, axis=2, keepdims=True)

    def update(mid, lo, hi):
        ge = count(aff >= mid) >= cap
        return jnp.where(ge, mid, lo), jnp.where(ge, hi, mid)

    def bits_body(_, c):
        lo, hi = c
        lo_i = lax.bitcast_convert_type(lo, I32)
        hi_i = lax.bitcast_convert_type(hi, I32)
        mid = lax.bitcast_convert_type(lo_i + lax.shift_right_logical(hi_i - lo_i, 1), F32)
        return update(mid, lo, hi)

    def value_body(_, c):
        lo, hi = c
        return update(lo + (hi - lo) * 0.5, lo, hi)

    lo0 = jnp.zeros((ne, 1, 1), F32)
    hi0 = jnp.full((ne, 1, 1), jnp.inf, F32)
    c = lax.fori_loop(0, BISECT_BITS_STEPS, bits_body, (lo0, hi0))
    lo, hi = lax.fori_loop(0, BISECT_VALUE_STEPS, value_body, c)

    li = lax.broadcasted_iota(I32, (LANES, LANES), 0)
    lj = lax.broadcasted_iota(I32, (LANES, LANES), 1)
    ut = (li <= lj).astype(BF16)
    ones = jnp.ones((LANES, LANES), BF16)
    bi = lax.broadcasted_iota(I32, (nbk, nbk), 0)
    bj = lax.broadcasted_iota(I32, (nbk, nbk), 1)
    slt = (bj < bi).astype(BF16)
    utb = (bi <= bj).astype(BF16)
    ones8 = jnp.ones((8, LANES), BF16)

    def block_cum(m):
        m2 = m.astype(BF16).reshape(ne * nbk, LANES)
        loc = _dot(m2, ut).reshape(ne, nbk, LANES)
        tot = _dot(m2, ones).astype(BF16).reshape(ne, nbk, LANES)
        off = jnp.stack([_dot(slt, tot[e]) for e in range(ne)])
        return loc, tot, off

    gt = aff >= hi
    eq = (aff >= lo) & (aff < hi)
    need = np.float32(cap) - count(gt)
    eloc, _, eoff = block_cum(eq)
    mask = gt | (eq & (eloc + eoff <= need))

    mloc, mtot, moff = block_cum(mask)
    pos_ref[...] = jnp.where(mask, mloc + moff - 1.0, -1.0).astype(I32)
    lm_scr[...] = mloc.astype(BF16)
    w_scr[...] = mtot
    a1, a2, a3 = _split3(aff)
    a1_scr[...] = a1
    a2_scr[...] = a2
    a3_scr[...] = a3
    maskb = mask.astype(BF16)
    for e in range(ne):
        cnt = _dot_nt(ones8, maskb[e])
        cnt_scr[e] = cnt
        cbr_scr[e] = _dot(cnt.astype(BF16), utb)
    cb_ref[...] = cbr_scr[:, 0, :].astype(I32)

    s_col = lax.broadcasted_iota(I32, (cap, 1), 0).astype(F32)
    lane_b = lax.broadcasted_iota(I32, (cap, nbk), 1).astype(F32)
    lane_l = lax.broadcasted_iota(I32, (cap, LANES), 1).astype(F32)
    ones_b = jnp.ones((nbk, LANES), BF16)
    ones8b = jnp.ones((8, nbk), BF16)

    def per_expert(e, _):
        cbrow = cbr_scr[e][0:1, :]
        le = (cbrow <= s_col).astype(BF16)
        blk_c = _dot(le, ones_b)
        start_c = _dot(le, w_scr[e])
        oh = (lane_b == blk_c[:, 0:nbk]).astype(BF16)
        g_loc = _dot(oh, lm_scr[e])
        le2 = (g_loc <= s_col - start_c).astype(BF16)
        within_c = _dot(le2, ones)
        blk_r = _dot_nt(ones8b, le)
        within_r = _dot_nt(ones8, le2)
        idx_ref[pl.ds(e, 1), :] = (blk_r * LANES + within_r)[0:1, :].astype(I32)
        pick = lane_l == within_c
        gsum = None
        for a_scr in (a1_scr, a2_scr, a3_scr):
            sel = jnp.where(pick, _dot(oh, a_scr[e]), 0.0).astype(BF16)
            part = _dot_nt(ones8, sel)
            gsum = part if gsum is None else gsum + part
        gate_ref[pl.ds(e, 1), :] = gsum[0:1, :]
        return 0

    lax.fori_loop(0, ne, per_expert, 0)


def _route(logits_t, cap):
    ne, t = logits_t.shape
    nbk = t // LANES
    lg3 = logits_t.reshape(ne, nbk, LANES)
    full = lambda shape: pl.BlockSpec(shape, lambda i: (0,) * len(shape))
    tab = lambda dt: pltpu.VMEM((ne, nbk, LANES), dt)
    return pl.pallas_call(
        functools.partial(_route_kernel, nbk=nbk, cap=cap),
        grid=(1,),
        in_specs=[full((ne, nbk, LANES))],
        out_specs=[full((ne, cap)), full((ne, cap)), full((ne, nbk)), full((ne, nbk, LANES))],
        out_shape=[jax.ShapeDtypeStruct((ne, cap), I32),
                   jax.ShapeDtypeStruct((ne, cap), F32),
                   jax.ShapeDtypeStruct((ne, nbk), I32),
                   jax.ShapeDtypeStruct((ne, nbk, LANES), I32)],
        scratch_shapes=[tab(BF16), tab(BF16), tab(BF16), tab(BF16), tab(BF16),
                        pltpu.VMEM((ne, 8, nbk), F32), pltpu.VMEM((ne, 8, nbk), F32)],
        compiler_params=_cparams(("arbitrary",)),
        name="route",
    )(lg3)


def _ffn_kernel(idx_ref, idxn_ref, gate_ref, x_hbm, mg_ref, wg_ref, wu_ref, wd_ref, ye_ref,
                xbuf, wbf, sem, *, nsteps, rb):
    step = pl.program_id(0) * pl.num_programs(1) + pl.program_id(1)
    slot = lax.rem(step, 2)

    @pl.when(pl.program_id(1) == 0)
    def _():
        for m, w_ref in enumerate((wg_ref, wu_ref, wd_ref)):
            wbf[m] = w_ref[0, 0].astype(BF16)

    def issue(ids_ref, s):
        for r in range(rb):
            pltpu.make_async_copy(x_hbm.at[pl.ds(ids_ref[0, 0, r], 1)], xbuf.at[s, pl.ds(r, 1)],
                                  sem.at[s]).start()

    def wait(s):
        pltpu.make_async_copy(x_hbm.at[pl.ds(0, rb)], xbuf.at[s], sem.at[s]).wait()

    @pl.when(step == 0)
    def _():
        issue(idx_ref, 0)

    issue(idxn_ref, 1 - slot)
    wait(slot)

    xe = _rms(xbuf[slot], mg_ref[...]).astype(BF16)
    hid = (_silu(_dot(xe, wbf[0])) * _dot(xe, wbf[1])).astype(BF16)
    ye = _dot(hid, wbf[2])
    ri = lax.broadcasted_iota(I32, (rb, rb), 0)
    ci = lax.broadcasted_iota(I32, (rb, rb), 1)
    gcol = jnp.sum(jnp.where(ri == ci, gate_ref[0], 0.0), axis=1, keepdims=True)
    ye_ref[...] = (ye * gcol).astype(BF16)

    @pl.when(step == nsteps - 1)
    def _():
        wait(1 - slot)


def _ffn(idx, gate, x1, mg, wg, wu, wd, layer):
    ne, cap = idx.shape
    rb = min(RB, cap)
    nblk = cap // rb
    nsteps = ne * nblk
    idx3 = idx.reshape(nsteps, 1, rb)
    gate3 = gate.reshape(nsteps, 1, rb)
    wspec = pl.BlockSpec((1, 1, D_MODEL, D_MODEL), lambda e, j: (layer, e, 0, 0))
    return pl.pallas_call(
        functools.partial(_ffn_kernel, nsteps=nsteps, rb=rb),
        grid=(ne, nblk),
        in_specs=[
            pl.BlockSpec((1, 1, rb), lambda e, j: (e * nblk + j, 0, 0), memory_space=pltpu.SMEM),
            pl.BlockSpec((1, 1, rb), lambda e, j: (jnp.minimum(e * nblk + j + 1, nsteps - 1), 0, 0),
                         memory_space=pltpu.SMEM),
            pl.BlockSpec((1, 1, rb), lambda e, j: (e * nblk + j, 0, 0)),
            pl.BlockSpec(memory_space=pl.ANY),
            pl.BlockSpec((1, D_MODEL), lambda e, j: (0, 0)),
            wspec, wspec, wspec,
        ],
        out_specs=pl.BlockSpec((rb, D_MODEL), lambda e, j: (e * nblk + j, 0)),
        out_shape=jax.ShapeDtypeStruct((ne * cap, D_MODEL), BF16),
        scratch_shapes=[pltpu.VMEM((2, rb, D_MODEL), F32), pltpu.VMEM((3, D_MODEL, D_MODEL), BF16),
                        pltpu.SemaphoreType.DMA((2,))],
        compiler_params=_cparams(("arbitrary", "arbitrary")),
        name="ffn",
    )(idx3, idx3, gate3, x1, mg, wg, wu, wd)


CHUNK_ROWS = 16
MAIN_CHUNKS = 96
MAIN_ROWS = CHUNK_ROWS * MAIN_CHUNKS
MAX_CHUNKS = -(-(N_EXPERTS * TB // CHUNK_ROWS + 2 * N_EXPERTS) // MAIN_CHUNKS) * MAIN_CHUNKS
NO_SLOT = 1 << 30


def _combine_kernel(cb_ref, x1_ref, pos_ref, ye_hbm, *rest, nbk, cap, ntb, final):
    if final:
        g_ref, out_ref, ybuf, acc_scr, sem, ie_scr, ir_scr, n_scr = rest
    else:
        out_ref, ybuf, acc_scr, sem, ie_scr, ir_scr, n_scr = rest
    tb = pl.program_id(0)
    s = lax.rem(tb, 2)
    bpt = TB // LANES
    shift = CHUNK_ROWS.bit_length() - 1

    def build(tbx, par):
        base = par * MAX_CHUNKS
        n = jnp.int32(0)
        for e in range(N_EXPERTS):
            b0 = e * nbk + tbx * bpt
            lo = jnp.where(tbx == 0, 0, cb_ref[jnp.maximum(b0 - 1, 0)])
            hi = cb_ref[b0 + bpt - 1]
            start = lax.shift_left(lax.shift_right_logical(lo, shift), shift)
            nch = jnp.where(hi > lo,
                            lax.shift_right_logical(hi - start + (CHUNK_ROWS - 1), shift), 0)

            def push(c, k, e=e, start=start):
                ie_scr[base + k] = e
                ir_scr[base + k] = start + c * CHUNK_ROWS
                return k + 1

            n = lax.fori_loop(0, nch, push, n)
        nsg = jnp.maximum(lax.div(n + (MAIN_CHUNKS - 1), MAIN_CHUNKS), 1)

        def pad(k, _):
            ie_scr[base + k] = 0
            ir_scr[base + k] = NO_SLOT
            return 0

        lax.fori_loop(n, nsg * MAIN_CHUNKS, pad, 0)
        n_scr[par] = n
        return n

    def chunk_copy(par, k, k0):
        base = par * MAX_CHUNKS
        row0 = pl.multiple_of(ie_scr[base + k] * cap + ir_scr[base + k], CHUNK_ROWS)
        dst0 = pl.multiple_of((k - k0) * CHUNK_ROWS, CHUNK_ROWS)
        return pltpu.make_async_copy(ye_hbm.at[pl.ds(row0, CHUNK_ROWS)],
                                     ybuf.at[par, pl.ds(dst0, CHUNK_ROWS)], sem.at[par])

    def issue(par, k0, k1):
        def body(k, _):
            chunk_copy(par, k, k0).start()
            return 0
        lax.fori_loop(k0, k1, body, 0)

    def drain(par, k0, k1):
        def body(k, _):
            chunk_copy(par, k, k0).wait()
            return 0
        lax.fori_loop(k0, k1, body, 0)

    @pl.when(tb == 0)
    def _():
        ybuf[...] = jnp.zeros_like(ybuf)
        n0 = build(0, 0)
        issue(0, 0, jnp.minimum(n0, MAIN_CHUNKS))

    @pl.when(tb + 1 < ntb)
    def _():
        n1 = build(tb + 1, 1 - s)
        issue(1 - s, 0, jnp.minimum(n1, MAIN_CHUNKS))

    jrow = lax.broadcasted_iota(I32, (CHUNK_ROWS, TB), 0)

    def scatter(sg):
        base = s * MAX_CHUNKS + sg * MAIN_CHUNKS
        pieces = []
        for c in range(MAIN_CHUNKS):
            prow = pos_ref[pl.ds(ie_scr[base + c], 1), :]
            pieces.append((prow == jrow + ir_scr[base + c]).astype(BF16))
        onehot_t = jnp.concatenate(pieces, axis=0)
        return lax.dot_general(onehot_t, ybuf[s], (((0,), (0,)), ((), ())),
                               preferred_element_type=F32)

    def finish(x2):
        return _rms(x2, g_ref[...]) if final else x2

    n = n_scr[s]
    drain(s, 0, jnp.minimum(n, MAIN_CHUNKS))
    nsg = lax.div(n + (MAIN_CHUNKS - 1), MAIN_CHUNKS)

    @pl.when(nsg <= 1)
    def _():
        out_ref[...] = finish(x1_ref[...] + scatter(0))

    @pl.when(nsg > 1)
    def _():
        acc_scr[...] = x1_ref[...] + scatter(0)

        def extra(sg, _):
            k0 = sg * MAIN_CHUNKS
            k1 = jnp.minimum(n, k0 + MAIN_CHUNKS)
            issue(s, k0, k1)
            drain(s, k0, k1)
            acc_scr[...] += scatter(sg)
            return 0

        lax.fori_loop(1, nsg, extra, 0)
        out_ref[...] = finish(acc_scr[...])


def _combine(cb, pos, x1, ye, g_final=None):
    t = x1.shape[0]
    ne, nbk = cb.shape
    cap = ye.shape[0] // ne
    final = g_final is not None
    blk = pl.BlockSpec((TB, D_MODEL), lambda i, cb: (i, 0))
    in_specs = [blk, pl.BlockSpec((ne, TB), lambda i, cb: (0, i)), pl.BlockSpec(memory_space=pl.ANY)]
    args = [cb.reshape(-1), x1, pos.reshape(ne, t), ye]
    if final:
        in_specs.append(pl.BlockSpec((1, D_MODEL), lambda i, cb: (0, 0)))
        args.append(g_final)
    grid_spec = pltpu.PrefetchScalarGridSpec(
        num_scalar_prefetch=1,
        grid=(t // TB,),
        in_specs=in_specs,
        out_specs=blk,
        scratch_shapes=[pltpu.VMEM((2, MAIN_ROWS, D_MODEL), BF16),
                        pltpu.VMEM((TB, D_MODEL), F32),
                        pltpu.SemaphoreType.DMA((2,)),
                        pltpu.SMEM((2 * MAX_CHUNKS,), I32), pltpu.SMEM((2 * MAX_CHUNKS,), I32),
                        pltpu.SMEM((2,), I32)],
    )
    return pl.pallas_call(
        functools.partial(_combine_kernel, nbk=nbk, cap=cap, ntb=t // TB, final=final),
        grid_spec=grid_spec,
        out_shape=jax.ShapeDtypeStruct((t, D_MODEL), F32),
        compiler_params=_cparams(("arbitrary",)),
        name="combine_final" if final else "combine",
    )(*args)


def _dft_mats(n, scale):
    k = np.arange(n)
    ang = 2.0 * np.pi * ((k[:, None] * k[None, :]) % n) / n
    return (jnp.asarray(np.cos(ang) * scale, F32).astype(BF16),
            jnp.asarray(np.sin(ang) * scale, F32).astype(BF16))


def _fft1_kernel(x_ref, g_ref, c1_ref, s1_ref, twc_ref, tws_ref, br_ref, bi_ref):
    for j in range(FFT_J):
        xj = _rms(x_ref[:, j, :], g_ref[...]).astype(BF16)
        ar = _dot(c1_ref[...], xj)
        ai = -_dot(s1_ref[...], xj)
        c = twc_ref[0, :, j:j + 1]
        s = tws_ref[0, :, j:j + 1]
        br_ref[:, j, :] = ar * c + ai * s
        bi_ref[:, j, :] = ai * c - ar * s


def _fft1(x, g, nb, n1):
    t = x.shape[0]
    n = n1 * FFT_N2
    c1, s1 = _dft_mats(n1, 1.0 / math.sqrt(n1))
    k1 = np.arange(n1)[:, None]
    n2 = np.arange(FFT_N2)[None, :]
    ang = 2.0 * np.pi * ((k1 * n2) % n) / n
    nj = FFT_N2 // FFT_J
    to_tab = lambda a: jnp.asarray(a.reshape(n1, nj, FFT_J).transpose(1, 0, 2), F32)
    twc, tws = to_tab(np.cos(ang)), to_tab(np.sin(ang))
    xv = x.reshape(nb * n1, FFT_N2, D_MODEL)
    blk = pl.BlockSpec((n1, FFT_J, D_MODEL), lambda b, j: (b, j, 0))
    tws_spec = pl.BlockSpec((1, n1, FFT_J), lambda b, j: (j, 0, 0))
    mat = pl.BlockSpec((n1, n1), lambda b, j: (0, 0))
    br, bi = pl.pallas_call(
        _fft1_kernel,
        grid=(nb, nj),
        in_specs=[blk, pl.BlockSpec((1, D_MODEL), lambda b, j: (0, 0)), mat, mat,
                  tws_spec, tws_spec],
        out_specs=[blk, blk],
        out_shape=[jax.ShapeDtypeStruct(xv.shape, F32)] * 2,
        compiler_params=_cparams(("arbitrary", "arbitrary")),
        name="fft1",
    )(xv, g, c1, s1, twc, tws)
    return br.reshape(t, D_MODEL), bi.reshape(t, D_MODEL)


FFT_PAIR = 2


def _fft2_kernel(br_ref, bi_ref, x_ref, c2_ref, s2_ref, cc_ref, sc_ref, wout_ref, mg_ref, wr_ref,
                 x1_ref, lg_ref, f_scr):
    n2 = FFT_N2
    for jp in range(FFT_K1C // FFT_PAIR):
        for q in range(FFT_PAIR):
            j = jp * FFT_PAIR + q
            rows = slice(j * n2, (j + 1) * n2)
            bjr = br_ref[rows, :].astype(BF16)
            bji = bi_ref[rows, :].astype(BF16)
            yr = (_dot(c2_ref[...], bjr) + _dot(s2_ref[...], bji)).astype(BF16)
            yi = (_dot(c2_ref[...], bji) - _dot(s2_ref[...], bjr)).astype(BF16)
            for g in range(C_GROUPS):
                cols = slice(g * C_GROUP_DIM, (g + 1) * C_GROUP_DIM)
                f = _dot(yr[:, cols], cc_ref[...]) + _dot(yi[:, cols], sc_ref[...])
                f_scr[q * n2:(q + 1) * n2, cols] = f.astype(BF16)
        y = _dot(f_scr[...], wout_ref[...])
        for q in range(FFT_PAIR):
            j = jp * FFT_PAIR + q
            x1 = x_ref[:, j, :] + y[q * n2:(q + 1) * n2, :]
            x1_ref[:, j, :] = x1
            lg_ref[:, j * n2:(j + 1) * n2] = _router_logits(x1, mg_ref, wr_ref)


def _fft2(br, bi, x, nb, n1, wout, mg, wr_t):
    t = x.shape[0]
    n2 = FFT_N2
    c2, s2 = _dft_mats(n2, 1.0 / math.sqrt(n2))
    cc, sc = _dft_mats(C_GROUP_DIM, 1.0 / math.sqrt(C_GROUP_DIM))
    nkc = n1 // FFT_K1C
    rows = FFT_K1C * n2
    xv = x.reshape(nb * n2, n1, D_MODEL)
    full = lambda shape: pl.BlockSpec(shape, lambda b, k: (0,) * len(shape))
    bspec = pl.BlockSpec((rows, D_MODEL), lambda b, k: (b * nkc + k, 0))
    xspec = pl.BlockSpec((n2, FFT_K1C, D_MODEL), lambda b, k: (b, k, 0))
    x1, lg = pl.pallas_call(
        _fft2_kernel,
        grid=(nb, nkc),
        in_specs=[bspec, bspec, xspec, full((n2, n2)), full((n2, n2)),
                  full((C_GROUP_DIM, C_GROUP_DIM)), full((C_GROUP_DIM, C_GROUP_DIM)),
                  full((D_MODEL, D_MODEL)), full((1, D_MODEL)), full((N_EXPERTS, D_MODEL))],
        out_specs=[xspec, pl.BlockSpec((N_EXPERTS, rows), lambda b, k: (0, b * nkc + k))],
        out_shape=[jax.ShapeDtypeStruct(xv.shape, F32), jax.ShapeDtypeStruct((N_EXPERTS, t), F32)],
        scratch_shapes=[pltpu.VMEM((FFT_PAIR * n2, D_MODEL), BF16)],
        compiler_params=_cparams(("arbitrary", "arbitrary")),
        name="fft2",
    )(br, bi, xv, c2, s2, cc, sc, wout, mg, wr_t)
    lg = lg.reshape(N_EXPERTS, nb, n1, n2).transpose(0, 1, 3, 2).reshape(N_EXPERTS, t)
    return x1.reshape(t, D_MODEL), lg


def _moe(x1, logits_t, mg, p, layer, g_final=None):
    t = x1.shape[0]
    cap = CAPACITY_FACTOR * t // N_EXPERTS
    idx, gate, cb, pos = _route(logits_t, cap)
    ye = _ffn(idx, gate, x1, mg, p["wg"], p["wu"], p["wd"], layer)
    return _combine(cb, pos, x1, ye, g_final)


def _trunk(x3, p):
    nb, s, _ = x3.shape
    x = x3.reshape(nb * s, D_MODEL)
    row = lambda v: v.reshape(1, -1)
    outa, glu = _front(x, row(p["mix_norm"][0]), p["w_in"], row(p["ab_v_norm_g"][0]),
                       row(p["ab_v_norm_b"][0]), p["w_spatial"], p["b_spatial"])
    mg0, mg1 = row(p["moe_norm"][0]), row(p["moe_norm"][1])
    x1, lg = _back(x, outa, glu, p["ab_conv_w"][0], row(p["ab_conv_b"][0]),
                   row(p["ab_conv_norm_g"][0]), row(p["ab_conv_norm_b"][0]), p["ab_w_out"],
                   mg0, p["router_t"][0], s)
    x2 = _moe(x1, lg, mg0, p, 0)
    n1 = s // FFT_N2
    br, bi = _fft1(x2, row(p["mix_norm"][1]), nb, n1)
    x1, lg = _fft2(br, bi, x2, nb, n1, p["c_w_out"], mg1, p["router_t"][1])
    y = _moe(x1, lg, mg1, p, 1, row(p["final_norm"]))
    return y.reshape(nb, s, D_MODEL)


def kernel(x_prompt, x_sample, mix_norm, ab_w_in, ab_v_norm_g, ab_v_norm_b, ab_w_spatial,
           ab_b_spatial, ab_conv_w, ab_conv_b, ab_conv_norm_g, ab_conv_norm_b, ab_w_out,
           c_w_out, moe_norm, moe_router, moe_w_gate, moe_w_up, moe_w_down, final_norm):
    p = dict(
        mix_norm=mix_norm, moe_norm=moe_norm, final_norm=final_norm,
        w_in=ab_w_in[0].astype(BF16),
        ab_v_norm_g=ab_v_norm_g, ab_v_norm_b=ab_v_norm_b,
        w_spatial=ab_w_spatial[0].astype(BF16),
        b_spatial=jnp.repeat(jnp.transpose(ab_b_spatial[0]), A_HEAD_DIM, axis=1),
        ab_conv_w=ab_conv_w, ab_conv_b=ab_conv_b,
        ab_conv_norm_g=ab_conv_norm_g, ab_conv_norm_b=ab_conv_norm_b,
        ab_w_out=ab_w_out[0].astype(BF16),
        c_w_out=c_w_out[0].astype(BF16),
        router_t=jnp.transpose(moe_router, (0, 2, 1)),
        wg=moe_w_gate, wu=moe_w_up, wd=moe_w_down,
    )
    return (_trunk(x_prompt, p), _trunk(x_sample, p))
```

```python
import functools
import math

import numpy as np
import jax
import jax.numpy as jnp
from jax import lax
from jax.experimental import pallas as pl
from jax.experimental.pallas import tpu as pltpu

F32, BF16, I32 = jnp.float32, jnp.bfloat16, jnp.int32

D_MODEL = 1024
D_A = 512
D_B = 512
A_GROUPS = 4
A_HEAD_DIM = 128
CHUNK = 128
CONV_WIDTH = 31
CONV_PAD = 15
C_GROUPS = 4
C_GROUP_DIM = 256
N_EXPERTS = 16
CAPACITY_FACTOR = 2
RMS_EPS = 1e-6
LN_EPS = 1e-5

LANES = 128
HALO = 16
TM = 512
RB = 512
TB = 512
FFT_N2 = 128
VMEM_LIMIT = 48 * 1024 * 1024
BISECT_BITS_STEPS = 31
BISECT_VALUE_STEPS = 30


def _cparams(sem):
    return pltpu.CompilerParams(dimension_semantics=sem, vmem_limit_bytes=VMEM_LIMIT)


def _dot(a, b):
    return jnp.dot(a, b, preferred_element_type=F32)


def _dot_nt(a, b, precision=None):
    return lax.dot_general(a, b, (((1,), (1,)), ((), ())), precision=precision,
                           preferred_element_type=F32)


def _rms(x, g):
    return x * lax.rsqrt(jnp.mean(x * x, axis=-1, keepdims=True) + RMS_EPS) * g


def _ln(x, g, b):
    mu = jnp.mean(x, axis=-1, keepdims=True)
    xc = x - mu
    var = jnp.mean(xc * xc, axis=-1, keepdims=True)
    return xc * lax.rsqrt(var + LN_EPS) * g + b


def _gelu(x):
    return 0.5 * x * (1.0 + lax.erf(x * np.float32(math.sqrt(0.5))))


def _silu(x):
    return x * jax.nn.sigmoid(x)


def _front_kernel(x_ref, g_ref, win_ref, vg_ref, vb_ref, wsp_ref, bsp_ref,
                  outa_ref, glu_ref, zu_scr, vn_scr):
    xn = _rms(x_ref[...], g_ref[...]).astype(BF16)
    zu_scr[...] = _gelu(_dot(xn, win_ref[:, 0:D_A]))
    zv = _gelu(_dot(xn, win_ref[:, D_A:2 * D_A]))
    vn_scr[...] = _ln(zv, vg_ref[...], vb_ref[...]).astype(BF16)
    a_in = _dot(xn, win_ref[:, 2 * D_A:2 * D_A + D_B])
    gate = _dot(xn, win_ref[:, 2 * D_A + D_B:])
    glu_ref[...] = a_in * jax.nn.sigmoid(gate)
    for c in range(TM // CHUNK):
        rows = slice(c * CHUNK, (c + 1) * CHUNK)
        for h in range(A_GROUPS):
            cols = slice(h * A_HEAD_DIM, (h + 1) * A_HEAD_DIM)
            mixed = _dot(wsp_ref[h], vn_scr[rows, cols]) + bsp_ref[:, cols]
            outa_ref[rows, cols] = (zu_scr[rows, cols] * mixed).astype(BF16)


def _front(x, g, w_in, vg, vb, wsp, bsp):
    t = x.shape[0]
    full = lambda shape: pl.BlockSpec(shape, lambda i: (0,) * len(shape))
    return pl.pallas_call(
        _front_kernel,
        grid=(t // TM,),
        in_specs=[
            pl.BlockSpec((TM, D_MODEL), lambda i: (i, 0)),
            full((1, D_MODEL)),
            full((D_MODEL, 2 * D_A + 2 * D_B)),
            full((1, D_A)), full((1, D_A)),
            full((A_GROUPS, CHUNK, CHUNK)),
            full((CHUNK, D_A)),
        ],
        out_specs=[pl.BlockSpec((TM, D_A), lambda i: (i, 0)),
                   pl.BlockSpec((TM, D_B), lambda i: (i, 0))],
        out_shape=[jax.ShapeDtypeStruct((t, D_A), BF16),
                   jax.ShapeDtypeStruct((t, D_B), F32)],
        scratch_shapes=[pltpu.VMEM((TM, D_A), F32), pltpu.VMEM((TM, D_A), BF16)],
        compiler_params=_cparams(("arbitrary",)),
        name="front",
    )(x, g, w_in, vg, vb, wsp, bsp)


CONV_ROWS = 64
SUBLANES = 8
SHIFT_ROWS = TM + (HALO + CONV_PAD) // SUBLANES * SUBLANES


def _split2(a):
    hi = a.astype(BF16)
    return hi, (a - hi.astype(F32)).astype(BF16)


def _router_logits(x1, mg_ref, wrh_ref, wrl_ref):
    xh, xl = _split2(_rms(x1, mg_ref[...]))
    return _dot_nt(wrh_ref[...], xh) + (_dot_nt(wrh_ref[...], xl) + _dot_nt(wrl_ref[...], xh))


def _back_kernel(x_ref, outa_ref, glu_ref, prev_ref, next_ref, cw_ref, cb_ref, ng_ref, nb_ref,
                 wout_ref, mg_ref, wrh_ref, wrl_ref, x1_ref, lg_ref, ext_scr, sh_scr, ob_scr,
                 *, blocks_per_seq):
    i = pl.program_id(0)
    pos = lax.rem(i, blocks_per_seq)
    ext_scr[0:HALO, :] = jnp.where(pos == 0, 0.0, prev_ref[...])
    ext_scr[HALO:HALO + TM, :] = glu_ref[...]
    ext_scr[HALO + TM:, :] = jnp.where(pos == blocks_per_seq - 1, 0.0, next_ref[...])
    for b in range(SUBLANES):
        sh_scr[b] = ext_scr[b:b + SHIFT_ROWS, :]
    for r in range(TM // CONV_ROWS):
        acc = jnp.zeros((CONV_ROWS, D_B), F32) + cb_ref[...]
        for k in range(CONV_WIDTH):
            a, b = divmod(HALO - CONV_PAD + k, SUBLANES)
            row0 = r * CONV_ROWS + a * SUBLANES
            acc = acc + cw_ref[k:k + 1, :] * sh_scr[b, row0:row0 + CONV_ROWS, :]
        ob = _silu(_ln(acc, ng_ref[...], nb_ref[...]))
        ob_scr[r * CONV_ROWS:(r + 1) * CONV_ROWS, :] = ob.astype(BF16)
    y = _dot(outa_ref[...], wout_ref[0:D_A, :]) + _dot(ob_scr[...], wout_ref[D_A:, :])
    x1 = x_ref[...] + y
    x1_ref[...] = x1
    lg_ref[...] = _router_logits(x1, mg_ref, wrh_ref, wrl_ref)


def _back(x, outa, glu, cw, cb, ng, nb_, wout, mg, wr, seq_len):
    t = x.shape[0]
    full = lambda shape: pl.BlockSpec(shape, lambda i: (0,) * len(shape))
    hb = TM // HALO
    return pl.pallas_call(
        functools.partial(_back_kernel, blocks_per_seq=seq_len // TM),
        grid=(t // TM,),
        in_specs=[
            pl.BlockSpec((TM, D_MODEL), lambda i: (i, 0)),
            pl.BlockSpec((TM, D_A), lambda i: (i, 0)),
            pl.BlockSpec((TM, D_B), lambda i: (i, 0)),
            pl.BlockSpec((HALO, D_B), lambda i: (jnp.maximum(i * hb - 1, 0), 0)),
            pl.BlockSpec((HALO, D_B), lambda i: (jnp.minimum((i + 1) * hb, t // HALO - 1), 0)),
            full((CONV_WIDTH, D_B)), full((1, D_B)), full((1, D_B)), full((1, D_B)),
            full((D_A + D_B, D_MODEL)),
            full((1, D_MODEL)),
            full((N_EXPERTS, D_MODEL)), full((N_EXPERTS, D_MODEL)),
        ],
        out_specs=[pl.BlockSpec((TM, D_MODEL), lambda i: (i, 0)),
                   pl.BlockSpec((N_EXPERTS, TM), lambda i: (0, i))],
        out_shape=[jax.ShapeDtypeStruct((t, D_MODEL), F32),
                   jax.ShapeDtypeStruct((N_EXPERTS, t), F32)],
        scratch_shapes=[pltpu.VMEM((TM + 2 * HALO, D_B), F32),
                        pltpu.VMEM((SUBLANES, SHIFT_ROWS, D_B), F32),
                        pltpu.VMEM((TM, D_B), BF16)],
        compiler_params=_cparams(("arbitrary",)),
        name="back",
    )(x, outa, glu, glu, glu, cw, cb, ng, nb_, wout, mg, *wr)


def _split3(a):
    a1 = a.astype(BF16)
    r1 = a - a1.astype(F32)
    a2 = r1.astype(BF16)
    a3 = (r1 - a2.astype(F32)).astype(BF16)
    return a1, a2, a3


def _route_kernel(lg_ref, idx_ref, gate_ref, cb_ref, pos_ref, lm_scr, w_scr, a1_scr, a2_scr, a3_scr,
                  cnt_scr, cbr_scr, *, nbk, cap):
    ne = N_EXPERTS
    lg = lg_ref[...]
    ex = jnp.exp(lg - jnp.max(lg, axis=0, keepdims=True))
    aff = ex / jnp.sum(ex, axis=0, keepdims=True)

    def count(m):
        s = jnp.sum(m.astype(F32), axis=1, keepdims=True)
        return jnp.sum(s, axis=2, keepdims=True)

    def update(mid, lo, hi):
        ge = count(aff >= mid) >= cap
        return jnp.where(ge, mid, lo), jnp.where(ge, hi, mid)

    def bits_body(_, c):
        lo, hi = c
        lo_i = lax.bitcast_convert_type(lo, I32)
        hi_i = lax.bitcast_convert_type(hi, I32)
        mid = lax.bitcast_convert_type(lo_i + lax.shift_right_logical(hi_i - lo_i, 1), F32)
        return update(mid, lo, hi)

    def value_body(_, c):
        lo, hi = c
        return update(lo + (hi - lo) * 0.5, lo, hi)

    lo0 = jnp.zeros((ne, 1, 1), F32)
    hi0 = jnp.full((ne, 1, 1), jnp.inf, F32)
    c = lax.fori_loop(0, BISECT_BITS_STEPS, bits_body, (lo0, hi0))
    lo, hi = lax.fori_loop(0, BISECT_VALUE_STEPS, value_body, c)

    li = lax.broadcasted_iota(I32, (LANES, LANES), 0)
    lj = lax.broadcasted_iota(I32, (LANES, LANES), 1)
    ut = (li <= lj).astype(BF16)
    ones = jnp.ones((LANES, LANES), BF16)
    bi = lax.broadcasted_iota(I32, (nbk, nbk), 0)
    bj = lax.broadcasted_iota(I32, (nbk, nbk), 1)
    slt = (bj < bi).astype(BF16)
    utb = (bi <= bj).astype(BF16)
    ones8 = jnp.ones((8, LANES), BF16)

    def block_cum(m):
        m2 = m.astype(BF16).reshape(ne * nbk, LANES)
        loc = _dot(m2, ut).reshape(ne, nbk, LANES)
        tot = _dot(m2, ones).astype(BF16).reshape(ne, nbk, LANES)
        off = jnp.stack([_dot(slt, tot[e]) for e in range(ne)])
        return loc, tot, off

    gt = aff >= hi
    eq = (aff >= lo) & (aff < hi)
    need = np.float32(cap) - count(gt)
    eloc, _, eoff = block_cum(eq)
    mask = gt | (eq & (eloc + eoff <= need))

    mloc, mtot, moff = block_cum(mask)
    pos_ref[...] = jnp.where(mask, mloc + moff - 1.0, -1.0).astype(I32)
    lm_scr[...] = mloc.astype(BF16)
    w_scr[...] = mtot
    a1, a2, a3 = _split3(aff)
    a1_scr[...] = a1
    a2_scr[...] = a2
    a3_scr[...] = a3
    maskb = mask.astype(BF16)
    for e in range(ne):
        cnt = _dot_nt(ones8, maskb[e])
        cnt_scr[e] = cnt
        cbr_scr[e] = _dot(cnt.astype(BF16), utb)
    cb_ref[...] = cbr_scr[:, 0, :].astype(I32)

    s_col = lax.broadcasted_iota(I32, (cap, 1), 0).astype(F32)
    lane_b = lax.broadcasted_iota(I32, (cap, nbk), 1).astype(F32)
    lane_l = lax.broadcasted_iota(I32, (cap, LANES), 1).astype(F32)
    ones_b = jnp.ones((nbk, LANES), BF16)
    ones8b = jnp.ones((8, nbk), BF16)

    def per_expert(e, _):
        cbrow = cbr_scr[e][0:1, :]
        le = (cbrow <= s_col).astype(BF16)
        blk_c = _dot(le, ones_b)
        start_c = _dot(le, w_scr[e])
        oh = (lane_b == blk_c[:, 0:nbk]).astype(BF16)
        g_loc = _dot(oh, lm_scr[e])
        le2 = (g_loc <= s_col - start_c).astype(BF16)
        within_c = _dot(le2, ones)
        blk_r = _dot_nt(ones8b, le)
        within_r = _dot_nt(ones8, le2)
        idx_ref[pl.ds(e, 1), :] = (blk_r * LANES + within_r)[0:1, :].astype(I32)
        pick = lane_l == within_c
        gsum = None
        for a_scr in (a1_scr, a2_scr, a3_scr):
            sel = jnp.where(pick, _dot(oh, a_scr[e]), 0.0).astype(BF16)
            part = _dot_nt(ones8, sel)
            gsum = part if gsum is None else gsum + part
        gate_ref[pl.ds(e, 1), :] = gsum[0:1, :]
        return 0

    lax.fori_loop(0, ne, per_expert, 0)


def _route(logits_t, cap):
    ne, t = logits_t.shape
    nbk = t // LANES
    lg3 = logits_t.reshape(ne, nbk, LANES)
    full = lambda shape: pl.BlockSpec(shape, lambda i: (0,) * len(shape))
    tab = lambda dt: pltpu.VMEM((ne, nbk, LANES), dt)
    return pl.pallas_call(
        functools.partial(_route_kernel, nbk=nbk, cap=cap),
        grid=(1,),
        in_specs=[full((ne, nbk, LANES))],
        out_specs=[full((ne, cap)), full((ne, cap)), full((ne, nbk)), full((ne, nbk, LANES))],
        out_shape=[jax.ShapeDtypeStruct((ne, cap), I32),
                   jax.ShapeDtypeStruct((ne, cap), F32),
                   jax.ShapeDtypeStruct((ne, nbk), I32),
                   jax.ShapeDtypeStruct((ne, nbk, LANES), I32)],
        scratch_shapes=[tab(BF16), tab(BF16), tab(BF16), tab(BF16), tab(BF16),
                        pltpu.VMEM((ne, 8, nbk), F32), pltpu.VMEM((ne, 8, nbk), F32)],
        compiler_params=_cparams(("arbitrary",)),
        name="route",
    )(lg3)


def _ffn_kernel(idx_ref, idxn_ref, gate_ref, x_hbm, mg_ref, wg_hbm, wu_hbm, wd_hbm, ye_ref,
                xbuf, wst, wbf, sem, wsem, *, layer, ne, nsteps, rb):
    e = pl.program_id(0)
    step = e * pl.num_programs(1) + pl.program_id(1)
    slot = lax.rem(step, 2)

    def weight_copies(ex):
        return [pltpu.make_async_copy(w_hbm.at[layer, ex], wst.at[m], wsem.at[m])
                for m, w_hbm in enumerate((wg_hbm, wu_hbm, wd_hbm))]

    def issue(ids_ref, s):
        for r in range(rb):
            pltpu.make_async_copy(x_hbm.at[pl.ds(ids_ref[0, 0, r], 1)], xbuf.at[s, pl.ds(r, 1)],
                                  sem.at[s]).start()

    def wait(s):
        pltpu.make_async_copy(x_hbm.at[pl.ds(0, rb)], xbuf.at[s], sem.at[s]).wait()

    @pl.when(step == 0)
    def _():
        for cp in weight_copies(0):
            cp.start()
        issue(idx_ref, 0)

    @pl.when(pl.program_id(1) == 0)
    def _():
        for m, cp in enumerate(weight_copies(e)):
            cp.wait()
            wbf[m] = wst[m].astype(BF16)

        @pl.when(e + 1 < ne)
        def _():
            for cp in weight_copies(e + 1):
                cp.start()

    issue(idxn_ref, 1 - slot)
    wait(slot)

    xe = _rms(xbuf[slot], mg_ref[...]).astype(BF16)
    hid = (_silu(_dot(xe, wbf[0])) * _dot(xe, wbf[1])).astype(BF16)
    ye = _dot(hid, wbf[2])
    ri = lax.broadcasted_iota(I32, (rb, rb), 0)
    ci = lax.broadcasted_iota(I32, (rb, rb), 1)
    gcol = jnp.sum(jnp.where(ri == ci, gate_ref[0], 0.0), axis=1, keepdims=True)
    ye_ref[...] = (ye * gcol).astype(BF16)

    @pl.when(step == nsteps - 1)
    def _():
        wait(1 - slot)


def _ffn(idx, gate, x1, mg, wg, wu, wd, layer):
    ne, cap = idx.shape
    rb = min(RB, cap)
    nblk = cap // rb
    nsteps = ne * nblk
    idx3 = idx.reshape(nsteps, 1, rb)
    gate3 = gate.reshape(nsteps, 1, rb)
    hbm = pl.BlockSpec(memory_space=pl.ANY)
    return pl.pallas_call(
        functools.partial(_ffn_kernel, layer=layer, ne=ne, nsteps=nsteps, rb=rb),
        grid=(ne, nblk),
        in_specs=[
            pl.BlockSpec((1, 1, rb), lambda e, j: (e * nblk + j, 0, 0), memory_space=pltpu.SMEM),
            pl.BlockSpec((1, 1, rb), lambda e, j: (jnp.minimum(e * nblk + j + 1, nsteps - 1), 0, 0),
                         memory_space=pltpu.SMEM),
            pl.BlockSpec((1, 1, rb), lambda e, j: (e * nblk + j, 0, 0)),
            hbm,
            pl.BlockSpec((1, D_MODEL), lambda e, j: (0, 0)),
            hbm, hbm, hbm,
        ],
        out_specs=pl.BlockSpec((rb, D_MODEL), lambda e, j: (e * nblk + j, 0)),
        out_shape=jax.ShapeDtypeStruct((ne * cap, D_MODEL), BF16),
        scratch_shapes=[pltpu.VMEM((2, rb, D_MODEL), F32),
                        pltpu.VMEM((3, D_MODEL, D_MODEL), F32),
                        pltpu.VMEM((3, D_MODEL, D_MODEL), BF16),
                        pltpu.SemaphoreType.DMA((2,)), pltpu.SemaphoreType.DMA((3,))],
        compiler_params=_cparams(("arbitrary", "arbitrary")),
        name="ffn",
    )(idx3, idx3, gate3, x1, mg, wg, wu, wd)


CHUNK_ROWS = 16
MAIN_CHUNKS = 96
MAIN_ROWS = CHUNK_ROWS * MAIN_CHUNKS
MAX_CHUNKS = -(-(N_EXPERTS * TB // CHUNK_ROWS + 2 * N_EXPERTS) // MAIN_CHUNKS) * MAIN_CHUNKS
NO_SLOT = 1 << 30


def _combine_kernel(cb_ref, x1_ref, pos_ref, ye_hbm, *rest, nbk, cap, ntb, final):
    if final:
        g_ref, out_ref, ybuf, acc_scr, sem, ie_scr, ir_scr, n_scr = rest
    else:
        out_ref, ybuf, acc_scr, sem, ie_scr, ir_scr, n_scr = rest
    tb = pl.program_id(0)
    s = lax.rem(tb, 2)
    bpt = TB // LANES
    shift = CHUNK_ROWS.bit_length() - 1

    def build(tbx, par):
        base = par * MAX_CHUNKS
        n = jnp.int32(0)
        for e in range(N_EXPERTS):
            b0 = e * nbk + tbx * bpt
            lo = jnp.where(tbx == 0, 0, cb_ref[jnp.maximum(b0 - 1, 0)])
            hi = cb_ref[b0 + bpt - 1]
            start = lax.shift_left(lax.shift_right_logical(lo, shift), shift)
            nch = jnp.where(hi > lo,
                            lax.shift_right_logical(hi - start + (CHUNK_ROWS - 1), shift), 0)

            def push(c, k, e=e, start=start):
                ie_scr[base + k] = e
                ir_scr[base + k] = start + c * CHUNK_ROWS
                return k + 1

            n = lax.fori_loop(0, nch, push, n)
        nsg = jnp.maximum(lax.div(n + (MAIN_CHUNKS - 1), MAIN_CHUNKS), 1)

        def pad(k, _):
            ie_scr[base + k] = 0
            ir_scr[base + k] = NO_SLOT
            return 0

        lax.fori_loop(n, nsg * MAIN_CHUNKS, pad, 0)
        n_scr[par] = n
        return n

    def chunk_copy(par, k, k0):
        base = par * MAX_CHUNKS
        row0 = pl.multiple_of(ie_scr[base + k] * cap + ir_scr[base + k], CHUNK_ROWS)
        dst0 = pl.multiple_of((k - k0) * CHUNK_ROWS, CHUNK_ROWS)
        return pltpu.make_async_copy(ye_hbm.at[pl.ds(row0, CHUNK_ROWS)],
                                     ybuf.at[par, pl.ds(dst0, CHUNK_ROWS)], sem.at[par])

    def issue(par, k0, k1):
        def body(k, _):
            chunk_copy(par, k, k0).start()
            return 0
        lax.fori_loop(k0, k1, body, 0)

    def drain(par, k0, k1):
        def body(k, _):
            chunk_copy(par, k, k0).wait()
            return 0
        lax.fori_loop(k0, k1, body, 0)

    @pl.when(tb == 0)
    def _():
        ybuf[...] = jnp.zeros_like(ybuf)
        n0 = build(0, 0)
        issue(0, 0, jnp.minimum(n0, MAIN_CHUNKS))

    @pl.when(tb + 1 < ntb)
    def _():
        n1 = build(tb + 1, 1 - s)
        issue(1 - s, 0, jnp.minimum(n1, MAIN_CHUNKS))

    jrow = lax.broadcasted_iota(I32, (CHUNK_ROWS, TB), 0)

    def scatter(sg):
        base = s * MAX_CHUNKS + sg * MAIN_CHUNKS
        pieces = []
        for c in range(MAIN_CHUNKS):
            prow = pos_ref[pl.ds(ie_scr[base + c], 1), :]
            pieces.append((prow == jrow + ir_scr[base + c]).astype(BF16))
        onehot_t = jnp.concatenate(pieces, axis=0)
        return lax.dot_general(onehot_t, ybuf[s], (((0,), (0,)), ((), ())),
                               preferred_element_type=F32)

    def finish(x2):
        return _rms(x2, g_ref[...]) if final else x2

    n = n_scr[s]
    drain(s, 0, jnp.minimum(n, MAIN_CHUNKS))
    nsg = lax.div(n + (MAIN_CHUNKS - 1), MAIN_CHUNKS)

    @pl.when(nsg <= 1)
    def _():
        out_ref[...] = finish(x1_ref[...] + scatter(0))

    @pl.when(nsg > 1)
    def _():
        acc_scr[...] = x1_ref[...] + scatter(0)

        def extra(sg, _):
            k0 = sg * MAIN_CHUNKS
            k1 = jnp.minimum(n, k0 + MAIN_CHUNKS)
            issue(s, k0, k1)
            drain(s, k0, k1)
            acc_scr[...] += scatter(sg)
            return 0

        lax.fori_loop(1, nsg, extra, 0)
        out_ref[...] = finish(acc_scr[...])


def _combine(cb, pos, x1, ye, g_final=None):
    t = x1.shape[0]
    ne, nbk = cb.shape
    cap = ye.shape[0] // ne
    final = g_final is not None
    blk = pl.BlockSpec((TB, D_MODEL), lambda i, cb: (i, 0))
    in_specs = [blk, pl.BlockSpec((ne, TB), lambda i, cb: (0, i)), pl.BlockSpec(memory_space=pl.ANY)]
    args = [cb.reshape(-1), x1, pos.reshape(ne, t), ye]
    if final:
        in_specs.append(pl.BlockSpec((1, D_MODEL), lambda i, cb: (0, 0)))
        args.append(g_final)
    grid_spec = pltpu.PrefetchScalarGridSpec(
        num_scalar_prefetch=1,
        grid=(t // TB,),
        in_specs=in_specs,
        out_specs=blk,
        scratch_shapes=[pltpu.VMEM((2, MAIN_ROWS, D_MODEL), BF16),
                        pltpu.VMEM((TB, D_MODEL), F32),
                        pltpu.SemaphoreType.DMA((2,)),
                        pltpu.SMEM((2 * MAX_CHUNKS,), I32), pltpu.SMEM((2 * MAX_CHUNKS,), I32),
                        pltpu.SMEM((2,), I32)],
    )
    return pl.pallas_call(
        functools.partial(_combine_kernel, nbk=nbk, cap=cap, ntb=t // TB, final=final),
        grid_spec=grid_spec,
        out_shape=jax.ShapeDtypeStruct((t, D_MODEL), F32),
        compiler_params=_cparams(("arbitrary",)),
        name="combine_final" if final else "combine",
    )(*args)


def _dft_mats(n, scale):
    k = np.arange(n)
    ang = 2.0 * np.pi * ((k[:, None] * k[None, :]) % n) / n
    return (jnp.asarray(np.cos(ang) * scale, F32).astype(BF16),
            jnp.asarray(np.sin(ang) * scale, F32).astype(BF16))


LANE_SLABS = D_MODEL // LANES
SUB = 8


def _lane_slab_specs(block, index_map):
    return [pl.BlockSpec(block + (LANES,), functools.partial(lambda *i, c: index_map(*i) + (c,), c=c))
            for c in range(LANE_SLABS)]


def _every_8th_row(slabs, j, rows):
    parts = [slabs[c, pl.ds(j, rows, stride=SUB), :] if not isinstance(slabs, (list, tuple))
             else slabs[c][pl.ds(j, rows, stride=SUB), :] for c in range(LANE_SLABS)]
    return jnp.concatenate(parts, axis=1)


def _fft1_kernel(*refs):
    xs = refs[:LANE_SLABS]
    g_ref, c1_ref, s1_ref, twc_ref, tws_ref, br_ref, bi_ref, x_scr = refs[LANE_SLABS:]
    n1 = c1_ref.shape[0]
    for c in range(LANE_SLABS):
        x_scr[c] = xs[c][...].reshape(n1 * SUB, LANES)
    for j in range(SUB):
        xj = _rms(_every_8th_row(x_scr, j, n1), g_ref[...]).astype(BF16)
        ar = _dot(c1_ref[...], xj)
        ai = -_dot(s1_ref[...], xj)
        c = twc_ref[0, :, j:j + 1]
        s = tws_ref[0, :, j:j + 1]
        br_ref[:, j] = (ar * c + ai * s).reshape(n1 // SUB, SUB, D_MODEL)
        bi_ref[:, j] = (ai * c - ar * s).reshape(n1 // SUB, SUB, D_MODEL)


def _fft1(x, g, nb, n1):
    t = x.shape[0]
    n = n1 * FFT_N2
    c1, s1 = _dft_mats(n1, 1.0 / math.sqrt(n1))
    k1 = np.arange(n1)[:, None]
    n2 = np.arange(FFT_N2)[None, :]
    ang = 2.0 * np.pi * ((k1 * n2) % n) / n
    nj = FFT_N2 // SUB
    nkc = n1 // SUB
    to_tab = lambda a: jnp.asarray(a.reshape(n1, nj, SUB).transpose(1, 0, 2), F32)
    twc, tws = to_tab(np.cos(ang)), to_tab(np.sin(ang))
    xv = x.reshape(nb * n1, FFT_N2, D_MODEL)
    out_blk = pl.BlockSpec((nkc, SUB, SUB, D_MODEL), lambda b, j: (b, j, 0, 0))
    out_shape = jax.ShapeDtypeStruct((nb * nkc, FFT_N2, SUB, D_MODEL), F32)
    tws_spec = pl.BlockSpec((1, n1, SUB), lambda b, j: (j, 0, 0))
    mat = pl.BlockSpec((n1, n1), lambda b, j: (0, 0))
    br, bi = pl.pallas_call(
        _fft1_kernel,
        grid=(nb, nj),
        in_specs=_lane_slab_specs((n1, SUB), lambda b, j: (b, j))
        + [pl.BlockSpec((1, D_MODEL), lambda b, j: (0, 0)), mat, mat, tws_spec, tws_spec],
        out_specs=[out_blk, out_blk],
        out_shape=[out_shape, out_shape],
        scratch_shapes=[pltpu.VMEM((LANE_SLABS, n1 * SUB, LANES), F32)],
        compiler_params=_cparams(("arbitrary", "arbitrary")),
        name="fft1",
    )(*([xv] * LANE_SLABS), g, c1, s1, twc, tws)
    return br.reshape(t, D_MODEL), bi.reshape(t, D_MODEL)


FFT_PAIR = 2


def _fft2_kernel(*refs):
    brs, bis, xs = (refs[i * LANE_SLABS:(i + 1) * LANE_SLABS] for i in range(3))
    (c2_ref, s2_ref, cc_ref, sc_ref, wout_ref, mg_ref, wrh_ref, wrl_ref,
     x1_ref, lg_ref, f_scr, x_scr) = refs[3 * LANE_SLABS:]
    n2 = FFT_N2
    for c in range(LANE_SLABS):
        x_scr[c] = xs[c][...].reshape(n2 * SUB, LANES)
    for jp in range(SUB // FFT_PAIR):
        for q in range(FFT_PAIR):
            j = jp * FFT_PAIR + q
            bjr = _every_8th_row(brs, j, n2).astype(BF16)
            bji = _every_8th_row(bis, j, n2).astype(BF16)
            yr = (_dot(c2_ref[...], bjr) + _dot(s2_ref[...], bji)).astype(BF16)
            yi = (_dot(c2_ref[...], bji) - _dot(s2_ref[...], bjr)).astype(BF16)
            for g in range(C_GROUPS):
                cols = slice(g * C_GROUP_DIM, (g + 1) * C_GROUP_DIM)
                f = _dot(yr[:, cols], cc_ref[...]) + _dot(yi[:, cols], sc_ref[...])
                f_scr[q * n2:(q + 1) * n2, cols] = f.astype(BF16)
        y = _dot(f_scr[...], wout_ref[...])
        for q in range(FFT_PAIR):
            j = jp * FFT_PAIR + q
            x1 = _every_8th_row(x_scr, j, n2) + y[q * n2:(q + 1) * n2, :]
            x1_ref[:, j, :] = x1
            lg_ref[:, j * n2:(j + 1) * n2] = _router_logits(x1, mg_ref, wrh_ref, wrl_ref)


def _fft2(br, bi, x, nb, n1, wout, mg, wr):
    t = x.shape[0]
    n2 = FFT_N2
    c2, s2 = _dft_mats(n2, 1.0 / math.sqrt(n2))
    cc, sc = _dft_mats(C_GROUP_DIM, 1.0 / math.sqrt(C_GROUP_DIM))
    nkc = n1 // SUB
    rows = SUB * n2
    xv = x.reshape(nb * n2, n1, D_MODEL)
    full = lambda shape: pl.BlockSpec(shape, lambda b, k: (0,) * len(shape))
    b_slabs = _lane_slab_specs((rows,), lambda b, k: (b * nkc + k,))
    x_slabs = _lane_slab_specs((n2, SUB), lambda b, k: (b, k))
    x1, lg = pl.pallas_call(
        _fft2_kernel,
        grid=(nb, nkc),
        in_specs=b_slabs * 2 + x_slabs
        + [full((n2, n2)), full((n2, n2)),
           full((C_GROUP_DIM, C_GROUP_DIM)), full((C_GROUP_DIM, C_GROUP_DIM)),
           full((D_MODEL, D_MODEL)), full((1, D_MODEL)),
           full((N_EXPERTS, D_MODEL)), full((N_EXPERTS, D_MODEL))],
        out_specs=[pl.BlockSpec((n2, SUB, D_MODEL), lambda b, k: (b, k, 0)),
                   pl.BlockSpec((N_EXPERTS, rows), lambda b, k: (0, b * nkc + k))],
        out_shape=[jax.ShapeDtypeStruct(xv.shape, F32),
                   jax.ShapeDtypeStruct((N_EXPERTS, t), F32)],
        scratch_shapes=[pltpu.VMEM((FFT_PAIR * n2, D_MODEL), BF16),
                        pltpu.VMEM((LANE_SLABS, rows, LANES), F32)],
        compiler_params=_cparams(("arbitrary", "arbitrary")),
        name="fft2",
    )(*([br] * LANE_SLABS), *([bi] * LANE_SLABS), *([xv] * LANE_SLABS),
      c2, s2, cc, sc, wout, mg, *wr)
    lg = lg.reshape(N_EXPERTS, nb, n1, n2).transpose(0, 1, 3, 2).reshape(N_EXPERTS, t)
    return x1.reshape(t, D_MODEL), lg


def _moe(x1, logits_t, mg, p, layer, g_final=None):
    t = x1.shape[0]
    cap = CAPACITY_FACTOR * t // N_EXPERTS
    idx, gate, cb, pos = _route(logits_t, cap)
    ye = _ffn(idx, gate, x1, mg, p["wg"], p["wu"], p["wd"], layer)
    return _combine(cb, pos, x1, ye, g_final)


def _trunk(x3, p):
    nb, s, _ = x3.shape
    x = x3.reshape(nb * s, D_MODEL)
    row = lambda v: v.reshape(1, -1)
    outa, glu = _front(x, row(p["mix_norm"][0]), p["w_in"], row(p["ab_v_norm_g"][0]),
                       row(p["ab_v_norm_b"][0]), p["w_spatial"], p["b_spatial"])
    mg0, mg1 = row(p["moe_norm"][0]), row(p["moe_norm"][1])
    x1, lg = _back(x, outa, glu, p["ab_conv_w"][0], row(p["ab_conv_b"][0]),
                   row(p["ab_conv_norm_g"][0]), row(p["ab_conv_norm_b"][0]), p["ab_w_out"],
                   mg0, p["router"][0], s)
    x2 = _moe(x1, lg, mg0, p, 0)
    n1 = s // FFT_N2
    br, bi = _fft1(x2, row(p["mix_norm"][1]), nb, n1)
    x1, lg = _fft2(br, bi, x2, nb, n1, p["c_w_out"], mg1, p["router"][1])
    y = _moe(x1, lg, mg1, p, 1, row(p["final_norm"]))
    return y.reshape(nb, s, D_MODEL)


def kernel(x_prompt, x_sample, mix_norm, ab_w_in, ab_v_norm_g, ab_v_norm_b, ab_w_spatial,
           ab_b_spatial, ab_conv_w, ab_conv_b, ab_conv_norm_g, ab_conv_norm_b, ab_w_out,
           c_w_out, moe_norm, moe_router, moe_w_gate, moe_w_up, moe_w_down, final_norm):
    p = dict(
        mix_norm=mix_norm, moe_norm=moe_norm, final_norm=final_norm,
        w_in=ab_w_in[0].astype(BF16),
        ab_v_norm_g=ab_v_norm_g, ab_v_norm_b=ab_v_norm_b,
        w_spatial=ab_w_spatial[0].astype(BF16),
        b_spatial=jnp.repeat(jnp.transpose(ab_b_spatial[0]), A_HEAD_DIM, axis=1),
        ab_conv_w=ab_conv_w, ab_conv_b=ab_conv_b,
        ab_conv_norm_g=ab_conv_norm_g, ab_conv_norm_b=ab_conv_norm_b,
        ab_w_out=ab_w_out[0].astype(BF16),
        c_w_out=c_w_out[0].astype(BF16),
        router=[_split2(jnp.transpose(moe_router[l])) for l in range(moe_router.shape[0])],
        wg=moe_w_gate, wu=moe_w_up, wd=moe_w_down,
    )
    return (_trunk(x_prompt, p), _trunk(x_sample, p))
```

```python
import functools
import math

import numpy as np
import jax
import jax.numpy as jnp
from jax import lax
from jax.experimental import pallas as pl
from jax.experimental.pallas import tpu as pltpu

F32, BF16, I32 = jnp.float32, jnp.bfloat16, jnp.int32

D_MODEL = 1024
D_A = 512
D_B = 512
A_GROUPS = 4
A_HEAD_DIM = 128
CHUNK = 128
CONV_WIDTH = 31
CONV_PAD = 15
C_GROUPS = 4
C_GROUP_DIM = 256
N_EXPERTS = 16
CAPACITY_FACTOR = 2
RMS_EPS = 1e-6
LN_EPS = 1e-5

LANES = 128
SUB = 8
LANE_SLABS = D_MODEL // LANES
HALO = 16
TM = 512
RB = 512
TB = 512
FFT_N2 = 128
VMEM_LIMIT = 48 * 1024 * 1024
BISECT_BITS_STEPS = 31
BISECT_VALUE_STEPS = 30


def _cparams(sem):
    return pltpu.CompilerParams(dimension_semantics=sem, vmem_limit_bytes=VMEM_LIMIT)


def _dot(a, b):
    return jnp.dot(a, b, preferred_element_type=F32)


def _dot_nt(a, b, precision=None):
    return lax.dot_general(a, b, (((1,), (1,)), ((), ())), precision=precision,
                           preferred_element_type=F32)


def _rms(x, g):
    return x * lax.rsqrt(jnp.mean(x * x, axis=-1, keepdims=True) + RMS_EPS) * g


def _ln(x, g, b):
    mu = jnp.mean(x, axis=-1, keepdims=True)
    xc = x - mu
    var = jnp.mean(xc * xc, axis=-1, keepdims=True)
    return xc * lax.rsqrt(var + LN_EPS) * g + b


def _gelu(x):
    return 0.5 * x * (1.0 + lax.erf(x * np.float32(math.sqrt(0.5))))


def _silu(x):
    return x * jax.nn.sigmoid(x)


def _front_kernel(x_ref, g_ref, win_ref, vg_ref, vb_ref, wsp_ref, bsp_ref,
                  outa_ref, glu_ref, zu_scr, vn_scr):
    xn = _rms(x_ref[...], g_ref[...]).astype(BF16)
    zu_scr[...] = _gelu(_dot(xn, win_ref[:, 0:D_A]))
    zv = _gelu(_dot(xn, win_ref[:, D_A:2 * D_A]))
    vn_scr[...] = _ln(zv, vg_ref[...], vb_ref[...]).astype(BF16)
    a_in = _dot(xn, win_ref[:, 2 * D_A:2 * D_A + D_B])
    gate = _dot(xn, win_ref[:, 2 * D_A + D_B:])
    glu_ref[...] = a_in * jax.nn.sigmoid(gate)
    for c in range(TM // CHUNK):
        rows = slice(c * CHUNK, (c + 1) * CHUNK)
        for h in range(A_GROUPS):
            cols = slice(h * A_HEAD_DIM, (h + 1) * A_HEAD_DIM)
            mixed = _dot(wsp_ref[h], vn_scr[rows, cols]) + bsp_ref[:, cols]
            outa_ref[rows, cols] = (zu_scr[rows, cols] * mixed).astype(BF16)


def _front(x, g, w_in, vg, vb, wsp, bsp):
    t = x.shape[0]
    full = lambda shape: pl.BlockSpec(shape, lambda i: (0,) * len(shape))
    return pl.pallas_call(
        _front_kernel,
        grid=(t // TM,),
        in_specs=[
            pl.BlockSpec((TM, D_MODEL), lambda i: (i, 0)),
            full((1, D_MODEL)),
            full((D_MODEL, 2 * D_A + 2 * D_B)),
            full((1, D_A)), full((1, D_A)),
            full((A_GROUPS, CHUNK, CHUNK)),
            full((CHUNK, D_A)),
        ],
        out_specs=[pl.BlockSpec((TM, D_A), lambda i: (i, 0)),
                   pl.BlockSpec((TM, D_B), lambda i: (i, 0))],
        out_shape=[jax.ShapeDtypeStruct((t, D_A), BF16),
                   jax.ShapeDtypeStruct((t, D_B), F32)],
        scratch_shapes=[pltpu.VMEM((TM, D_A), F32), pltpu.VMEM((TM, D_A), BF16)],
        compiler_params=_cparams(("arbitrary",)),
        name="front",
    )(x, g, w_in, vg, vb, wsp, bsp)


CONV_ROWS = 64
SUBLANES = 8
SHIFT_ROWS = TM + (HALO + CONV_PAD) // SUBLANES * SUBLANES


def _split2(a):
    hi = a.astype(BF16)
    return hi, (a - hi.astype(F32)).astype(BF16)


def _router_logits(x1, mg_ref, wrh_ref, wrl_ref):
    xh, xl = _split2(_rms(x1, mg_ref[...]))
    return _dot_nt(wrh_ref[...], xh) + (_dot_nt(wrh_ref[...], xl) + _dot_nt(wrl_ref[...], xh))


def _back_kernel(x_ref, outa_ref, glu_ref, prev_ref, next_ref, cw_ref, cb_ref, ng_ref, nb_ref,
                 wout_ref, mg_ref, wrh_ref, wrl_ref, x1_ref, lg_ref, ext_scr, sh_scr, ob_scr,
                 *, blocks_per_seq):
    i = pl.program_id(0)
    pos = lax.rem(i, blocks_per_seq)
    ext_scr[0:HALO, :] = jnp.where(pos == 0, 0.0, prev_ref[...])
    ext_scr[HALO:HALO + TM, :] = glu_ref[...]
    ext_scr[HALO + TM:, :] = jnp.where(pos == blocks_per_seq - 1, 0.0, next_ref[...])
    for b in range(SUBLANES):
        sh_scr[b] = ext_scr[b:b + SHIFT_ROWS, :]
    for r in range(TM // CONV_ROWS):
        acc = jnp.zeros((CONV_ROWS, D_B), F32) + cb_ref[...]
        for k in range(CONV_WIDTH):
            a, b = divmod(HALO - CONV_PAD + k, SUBLANES)
            row0 = r * CONV_ROWS + a * SUBLANES
            acc = acc + cw_ref[k:k + 1, :] * sh_scr[b, row0:row0 + CONV_ROWS, :]
        ob = _silu(_ln(acc, ng_ref[...], nb_ref[...]))
        ob_scr[r * CONV_ROWS:(r + 1) * CONV_ROWS, :] = ob.astype(BF16)
    y = _dot(outa_ref[...], wout_ref[0:D_A, :]) + _dot(ob_scr[...], wout_ref[D_A:, :])
    x1 = x_ref[...] + y
    x1_ref[...] = x1
    lg_ref[...] = _router_logits(x1, mg_ref, wrh_ref, wrl_ref)


def _back(x, outa, glu, cw, cb, ng, nb_, wout, mg, wr, seq_len):
    t = x.shape[0]
    full = lambda shape: pl.BlockSpec(shape, lambda i: (0,) * len(shape))
    hb = TM // HALO
    return pl.pallas_call(
        functools.partial(_back_kernel, blocks_per_seq=seq_len // TM),
        grid=(t // TM,),
        in_specs=[
            pl.BlockSpec((TM, D_MODEL), lambda i: (i, 0)),
            pl.BlockSpec((TM, D_A), lambda i: (i, 0)),
            pl.BlockSpec((TM, D_B), lambda i: (i, 0)),
            pl.BlockSpec((HALO, D_B), lambda i: (jnp.maximum(i * hb - 1, 0), 0)),
            pl.BlockSpec((HALO, D_B), lambda i: (jnp.minimum((i + 1) * hb, t // HALO - 1), 0)),
            full((CONV_WIDTH, D_B)), full((1, D_B)), full((1, D_B)), full((1, D_B)),
            full((D_A + D_B, D_MODEL)),
            full((1, D_MODEL)),
            full((N_EXPERTS, D_MODEL)), full((N_EXPERTS, D_MODEL)),
        ],
        out_specs=[pl.BlockSpec((TM, D_MODEL), lambda i: (i, 0)),
                   pl.BlockSpec((N_EXPERTS, TM), lambda i: (0, i))],
        out_shape=[jax.ShapeDtypeStruct((t, D_MODEL), F32),
                   jax.ShapeDtypeStruct((N_EXPERTS, t), F32)],
        scratch_shapes=[pltpu.VMEM((TM + 2 * HALO, D_B), F32),
                        pltpu.VMEM((SUBLANES, SHIFT_ROWS, D_B), F32),
                        pltpu.VMEM((TM, D_B), BF16)],
        compiler_params=_cparams(("arbitrary",)),
        name="back",
    )(x, outa, glu, glu, glu, cw, cb, ng, nb_, wout, mg, *wr)


def _split3(a):
    a1 = a.astype(BF16)
    r1 = a - a1.astype(F32)
    a2 = r1.astype(BF16)
    a3 = (r1 - a2.astype(F32)).astype(BF16)
    return a1, a2, a3


def _route_kernel(lg_ref, idx_ref, gate_ref, cb_ref, pos_ref, lm_scr, w_scr, a1_scr, a2_scr, a3_scr,
                  cnt_scr, cbr_scr, *, nbk, cap):
    ne = N_EXPERTS
    lg = lg_ref[...]
    ex = jnp.exp(lg - jnp.max(lg, axis=0, keepdims=True))
    aff = ex / jnp.sum(ex, axis=0, keepdims=True)

    def count(m):
        s = jnp.sum(m.astype(F32), axis=1, keepdims=True)
        return jnp.sum(s, axis=2, keepdims=True)

    def update(mid, lo, hi):
        ge = count(aff >= mid) >= cap
        return jnp.where(ge, mid, lo), jnp.where(ge, hi, mid)

    def bits_body(_, c):
        lo, hi = c
        lo_i = lax.bitcast_convert_type(lo, I32)
        hi_i = lax.bitcast_convert_type(hi, I32)
        mid = lax.bitcast_convert_type(lo_i + lax.shift_right_logical(hi_i - lo_i, 1), F32)
        return update(mid, lo, hi)

    def value_body(_, c):
        lo, hi = c
        return update(lo + (hi - lo) * 0.5, lo, hi)

    lo0 = jnp.zeros((ne, 1, 1), F32)
    hi0 = jnp.full((ne, 1, 1), jnp.inf, F32)
    c = lax.fori_loop(0, BISECT_BITS_STEPS, bits_body, (lo0, hi0))
    lo, hi = lax.fori_loop(0, BISECT_VALUE_STEPS, value_body, c)

    li = lax.broadcasted_iota(I32, (LANES, LANES), 0)
    lj = lax.broadcasted_iota(I32, (LANES, LANES), 1)
    ut = (li <= lj).astype(BF16)
    ones = jnp.ones((LANES, LANES), BF16)
    bi = lax.broadcasted_iota(I32, (nbk, nbk), 0)
    bj = lax.broadcasted_iota(I32, (nbk, nbk), 1)
    slt = (bj < bi).astype(BF16)
    utb = (bi <= bj).astype(BF16)
    ones8 = jnp.ones((8, LANES), BF16)

    def block_cum(m):
        m2 = m.astype(BF16).reshape(ne * nbk, LANES)
        loc = _dot(m2, ut).reshape(ne, nbk, LANES)
        tot = _dot(m2, ones).astype(BF16).reshape(ne, nbk, LANES)
        off = jnp.stack([_dot(slt, tot[e]) for e in range(ne)])
        return loc, tot, off

    gt = aff >= hi
    eq = (aff >= lo) & (aff < hi)
    need = np.float32(cap) - count(gt)
    eloc, _, eoff = block_cum(eq)
    mask = gt | (eq & (eloc + eoff <= need))

    mloc, mtot, moff = block_cum(mask)
    pos_ref[...] = jnp.where(mask, mloc + moff - 1.0, -1.0).astype(I32)
    lm_scr[...] = mloc.astype(BF16)
    w_scr[...] = mtot
    a1, a2, a3 = _split3(aff)
    a1_scr[...] = a1
    a2_scr[...] = a2
    a3_scr[...] = a3
    maskb = mask.astype(BF16)
    for e in range(ne):
        cnt = _dot_nt(ones8, maskb[e])
        cnt_scr[e] = cnt
        cbr_scr[e] = _dot(cnt.astype(BF16), utb)
    cb_ref[...] = cbr_scr[:, 0, :].astype(I32)

    s_col = lax.broadcasted_iota(I32, (cap, 1), 0).astype(F32)
    lane_b = lax.broadcasted_iota(I32, (cap, nbk), 1).astype(F32)
    lane_l = lax.broadcasted_iota(I32, (cap, LANES), 1).astype(F32)
    ones_b = jnp.ones((nbk, LANES), BF16)
    ones8b = jnp.ones((8, nbk), BF16)

    def per_expert(e, _):
        cbrow = cbr_scr[e][0:1, :]
        le = (cbrow <= s_col).astype(BF16)
        blk_c = _dot(le, ones_b)
        start_c = _dot(le, w_scr[e])
        oh = (lane_b == blk_c[:, 0:nbk]).astype(BF16)
        g_loc = _dot(oh, lm_scr[e])
        le2 = (g_loc <= s_col - start_c).astype(BF16)
        within_c = _dot(le2, ones)
        blk_r = _dot_nt(ones8b, le)
        within_r = _dot_nt(ones8, le2)
        idx_ref[pl.ds(e, 1), :] = (blk_r * LANES + within_r)[0:1, :].astype(I32)
        pick = lane_l == within_c
        gsum = None
        for a_scr in (a1_scr, a2_scr, a3_scr):
            sel = jnp.where(pick, _dot(oh, a_scr[e]), 0.0).astype(BF16)
            part = _dot_nt(ones8, sel)
            gsum = part if gsum is None else gsum + part
        gate_ref[pl.ds(e, 1), :] = gsum[0:1, :]
        return 0

    lax.fori_loop(0, ne, per_expert, 0)


def _route(logits_t, cap):
    ne, t = logits_t.shape
    nbk = t // LANES
    lg3 = logits_t.reshape(ne, nbk, LANES)
    full = lambda shape: pl.BlockSpec(shape, lambda i: (0,) * len(shape))
    tab = lambda dt: pltpu.VMEM((ne, nbk, LANES), dt)
    return pl.pallas_call(
        functools.partial(_route_kernel, nbk=nbk, cap=cap),
        grid=(1,),
        in_specs=[full((ne, nbk, LANES))],
        out_specs=[full((ne, cap)), full((ne, cap)), full((ne, nbk)), full((ne, nbk, LANES))],
        out_shape=[jax.ShapeDtypeStruct((ne, cap), I32),
                   jax.ShapeDtypeStruct((ne, cap), F32),
                   jax.ShapeDtypeStruct((ne, nbk), I32),
                   jax.ShapeDtypeStruct((ne, nbk, LANES), I32)],
        scratch_shapes=[tab(BF16), tab(BF16), tab(BF16), tab(BF16), tab(BF16),
                        pltpu.VMEM((ne, 8, nbk), F32), pltpu.VMEM((ne, 8, nbk), F32)],
        compiler_params=_cparams(("arbitrary",)),
        name="route",
    )(lg3)


def _ffn_kernel(idx_ref, idxn_ref, gate_ref, x_hbm, x2d_hbm, mg_ref, wg_hbm, wu_hbm, wd_hbm, ye_ref,
                xbuf, wst, wbf, sem, wsem, *, layer, ne, nsteps, rb):
    e = pl.program_id(0)
    step = e * pl.num_programs(1) + pl.program_id(1)
    slot = lax.rem(step, 2)

    def weight_copies(ex):
        return [pltpu.make_async_copy(w_hbm.at[layer, ex], wst.at[m], wsem.at[m])
                for m, w_hbm in enumerate((wg_hbm, wu_hbm, wd_hbm))]

    def issue(ids_ref, s):
        for r in range(rb):
            pltpu.make_async_copy(x_hbm.at[ids_ref[0, 0, r]], xbuf.at[s, pl.ds(r * SUB, SUB)],
                                  sem.at[s]).start()

    def wait(s):
        pltpu.make_async_copy(x2d_hbm.at[pl.ds(0, rb * SUB)], xbuf.at[s], sem.at[s]).wait()

    @pl.when(step == 0)
    def _():
        for cp in weight_copies(0):
            cp.start()
        issue(idx_ref, 0)

    @pl.when(pl.program_id(1) == 0)
    def _():
        for m, cp in enumerate(weight_copies(e)):
            cp.wait()
            wbf[m] = wst[m].astype(BF16)

        @pl.when(e + 1 < ne)
        def _():
            for cp in weight_copies(e + 1):
                cp.start()

    issue(idxn_ref, 1 - slot)
    wait(slot)

    xrows = jnp.concatenate([xbuf[slot, pl.ds(c, rb, stride=SUB), :] for c in range(LANE_SLABS)],
                            axis=1)
    xe = _rms(xrows, mg_ref[...]).astype(BF16)
    hid = (_silu(_dot(xe, wbf[0])) * _dot(xe, wbf[1])).astype(BF16)
    ye = _dot(hid, wbf[2])
    ri = lax.broadcasted_iota(I32, (rb, rb), 0)
    ci = lax.broadcasted_iota(I32, (rb, rb), 1)
    gcol = jnp.sum(jnp.where(ri == ci, gate_ref[0], 0.0), axis=1, keepdims=True)
    ye_ref[...] = (ye * gcol).astype(BF16)

    @pl.when(step == nsteps - 1)
    def _():
        wait(1 - slot)


def _ffn(idx, gate, x1, mg, wg, wu, wd, layer):
    ne, cap = idx.shape
    t = x1.shape[0]
    x1t = x1.reshape(t, SUB, LANES)
    rb = min(RB, cap)
    nblk = cap // rb
    nsteps = ne * nblk
    idx3 = idx.reshape(nsteps, 1, rb)
    gate3 = gate.reshape(nsteps, 1, rb)
    hbm = pl.BlockSpec(memory_space=pl.ANY)
    return pl.pallas_call(
        functools.partial(_ffn_kernel, layer=layer, ne=ne, nsteps=nsteps, rb=rb),
        grid=(ne, nblk),
        in_specs=[
            pl.BlockSpec((1, 1, rb), lambda e, j: (e * nblk + j, 0, 0), memory_space=pltpu.SMEM),
            pl.BlockSpec((1, 1, rb), lambda e, j: (jnp.minimum(e * nblk + j + 1, nsteps - 1), 0, 0),
                         memory_space=pltpu.SMEM),
            pl.BlockSpec((1, 1, rb), lambda e, j: (e * nblk + j, 0, 0)),
            hbm, hbm,
            pl.BlockSpec((1, D_MODEL), lambda e, j: (0, 0)),
            hbm, hbm, hbm,
        ],
        out_specs=pl.BlockSpec((rb, D_MODEL), lambda e, j: (e * nblk + j, 0)),
        out_shape=jax.ShapeDtypeStruct((ne * cap, D_MODEL), BF16),
        scratch_shapes=[pltpu.VMEM((2, rb * SUB, LANES), F32),
                        pltpu.VMEM((3, D_MODEL, D_MODEL), F32),
                        pltpu.VMEM((3, D_MODEL, D_MODEL), BF16),
                        pltpu.SemaphoreType.DMA((2,)), pltpu.SemaphoreType.DMA((3,))],
        compiler_params=_cparams(("arbitrary", "arbitrary")),
        name="ffn",
    )(idx3, idx3, gate3, x1t, x1t.reshape(t * SUB, LANES), mg, wg, wu, wd)


CHUNK_ROWS = 16
MAIN_CHUNKS = 96
MAIN_ROWS = CHUNK_ROWS * MAIN_CHUNKS
MAX_CHUNKS = -(-(N_EXPERTS * TB // CHUNK_ROWS + 2 * N_EXPERTS) // MAIN_CHUNKS) * MAIN_CHUNKS
NO_SLOT = 1 << 30


def _combine_kernel(cb_ref, x1_ref, pos_ref, ye_hbm, *rest, nbk, cap, ntb, final):
    if final:
        g_ref, out_ref, ybuf, acc_scr, sem, ie_scr, ir_scr, n_scr = rest
    else:
        out_ref, ybuf, acc_scr, sem, ie_scr, ir_scr, n_scr = rest
    tb = pl.program_id(0)
    s = lax.rem(tb, 2)
    bpt = TB // LANES
    shift = CHUNK_ROWS.bit_length() - 1

    def build(tbx, par):
        base = par * MAX_CHUNKS
        n = jnp.int32(0)
        for e in range(N_EXPERTS):
            b0 = e * nbk + tbx * bpt
            lo = jnp.where(tbx == 0, 0, cb_ref[jnp.maximum(b0 - 1, 0)])
            hi = cb_ref[b0 + bpt - 1]
            start = lax.shift_left(lax.shift_right_logical(lo, shift), shift)
            nch = jnp.where(hi > lo,
                            lax.shift_right_logical(hi - start + (CHUNK_ROWS - 1), shift), 0)

            def push(c, k, e=e, start=start):
                ie_scr[base + k] = e
                ir_scr[base + k] = start + c * CHUNK_ROWS
                return k + 1

            n = lax.fori_loop(0, nch, push, n)
        nsg = jnp.maximum(lax.div(n + (MAIN_CHUNKS - 1), MAIN_CHUNKS), 1)

        def pad(k, _):
            ie_scr[base + k] = 0
            ir_scr[base + k] = NO_SLOT
            return 0

        lax.fori_loop(n, nsg * MAIN_CHUNKS, pad, 0)
        n_scr[par] = n
        return n

    def chunk_copy(par, k, k0):
        base = par * MAX_CHUNKS
        row0 = pl.multiple_of(ie_scr[base + k] * cap + ir_scr[base + k], CHUNK_ROWS)
        dst0 = pl.multiple_of((k - k0) * CHUNK_ROWS, CHUNK_ROWS)
        return pltpu.make_async_copy(ye_hbm.at[pl.ds(row0, CHUNK_ROWS)],
                                     ybuf.at[par, pl.ds(dst0, CHUNK_ROWS)], sem.at[par])

    def issue(par, k0, k1):
        def body(k, _):
            chunk_copy(par, k, k0).start()
            return 0
        lax.fori_loop(k0, k1, body, 0)

    def drain(par, k0, k1):
        def body(k, _):
            chunk_copy(par, k, k0).wait()
            return 0
        lax.fori_loop(k0, k1, body, 0)

    @pl.when(tb == 0)
    def _():
        ybuf[...] = jnp.zeros_like(ybuf)
        n0 = build(0, 0)
        issue(0, 0, jnp.minimum(n0, MAIN_CHUNKS))

    @pl.when(tb + 1 < ntb)
    def _():
        n1 = build(tb + 1, 1 - s)
        issue(1 - s, 0, jnp.minimum(n1, MAIN_CHUNKS))

    jrow = lax.broadcasted_iota(I32, (CHUNK_ROWS, TB), 0)

    def scatter(sg):
        base = s * MAX_CHUNKS + sg * MAIN_CHUNKS
        pieces = []
        for c in range(MAIN_CHUNKS):
            prow = pos_ref[pl.ds(ie_scr[base + c], 1), :]
            pieces.append((prow == jrow + ir_scr[base + c]).astype(BF16))
        onehot_t = jnp.concatenate(pieces, axis=0)
        return lax.dot_general(onehot_t, ybuf[s], (((0,), (0,)), ((), ())),
                               preferred_element_type=F32)

    def finish(x2):
        return _rms(x2, g_ref[...]) if final else x2

    n = n_scr[s]
    drain(s, 0, jnp.minimum(n, MAIN_CHUNKS))
    nsg = lax.div(n + (MAIN_CHUNKS - 1), MAIN_CHUNKS)

    @pl.when(nsg <= 1)
    def _():
        out_ref[...] = finish(x1_ref[...] + scatter(0))

    @pl.when(nsg > 1)
    def _():
        acc_scr[...] = x1_ref[...] + scatter(0)

        def extra(sg, _):
            k0 = sg * MAIN_CHUNKS
            k1 = jnp.minimum(n, k0 + MAIN_CHUNKS)
            issue(s, k0, k1)
            drain(s, k0, k1)
            acc_scr[...] += scatter(sg)
            return 0

        lax.fori_loop(1, nsg, extra, 0)
        out_ref[...] = finish(acc_scr[...])


def _combine(cb, pos, x1, ye, g_final=None):
    t = x1.shape[0]
    ne, nbk = cb.shape
    cap = ye.shape[0] // ne
    final = g_final is not None
    blk = pl.BlockSpec((TB, D_MODEL), lambda i, cb: (i, 0))
    in_specs = [blk, pl.BlockSpec((ne, TB), lambda i, cb: (0, i)), pl.BlockSpec(memory_space=pl.ANY)]
    args = [cb.reshape(-1), x1, pos.reshape(ne, t), ye]
    if final:
        in_specs.append(pl.BlockSpec((1, D_MODEL), lambda i, cb: (0, 0)))
        args.append(g_final)
    grid_spec = pltpu.PrefetchScalarGridSpec(
        num_scalar_prefetch=1,
        grid=(t // TB,),
        in_specs=in_specs,
        out_specs=blk,
        scratch_shapes=[pltpu.VMEM((2, MAIN_ROWS, D_MODEL), BF16),
                        pltpu.VMEM((TB, D_MODEL), F32),
                        pltpu.SemaphoreType.DMA((2,)),
                        pltpu.SMEM((2 * MAX_CHUNKS,), I32), pltpu.SMEM((2 * MAX_CHUNKS,), I32),
                        pltpu.SMEM((2,), I32)],
    )
    return pl.pallas_call(
        functools.partial(_combine_kernel, nbk=nbk, cap=cap, ntb=t // TB, final=final),
        grid_spec=grid_spec,
        out_shape=jax.ShapeDtypeStruct((t, D_MODEL), F32),
        compiler_params=_cparams(("arbitrary",)),
        name="combine_final" if final else "combine",
    )(*args)


def _dft_mats(n, scale):
    k = np.arange(n)
    ang = 2.0 * np.pi * ((k[:, None] * k[None, :]) % n) / n
    return (jnp.asarray(np.cos(ang) * scale, F32).astype(BF16),
            jnp.asarray(np.sin(ang) * scale, F32).astype(BF16))


def _lane_slab_specs(block, index_map):
    return [pl.BlockSpec(block + (LANES,), functools.partial(lambda *i, c: index_map(*i) + (c,), c=c))
            for c in range(LANE_SLABS)]


def _every_8th_row(slabs, j, rows):
    parts = [slabs[c, pl.ds(j, rows, stride=SUB), :] if not isinstance(slabs, (list, tuple))
             else slabs[c][pl.ds(j, rows, stride=SUB), :] for c in range(LANE_SLABS)]
    return jnp.concatenate(parts, axis=1)


def _fft1_kernel(*refs):
    xs = refs[:LANE_SLABS]
    g_ref, c1_ref, s1_ref, twc_ref, tws_ref, br_ref, bi_ref, x_scr = refs[LANE_SLABS:]
    n1 = c1_ref.shape[0]
    for c in range(LANE_SLABS):
        x_scr[c] = xs[c][...].reshape(n1 * SUB, LANES)
    for j in range(SUB):
        xj = _rms(_every_8th_row(x_scr, j, n1), g_ref[...]).astype(BF16)
        ar = _dot(c1_ref[...], xj)
        ai = -_dot(s1_ref[...], xj)
        c = twc_ref[0, :, j:j + 1]
        s = tws_ref[0, :, j:j + 1]
        br_ref[:, j] = (ar * c + ai * s).reshape(n1 // SUB, SUB, D_MODEL)
        bi_ref[:, j] = (ai * c - ar * s).reshape(n1 // SUB, SUB, D_MODEL)


def _fft1(x, g, nb, n1):
    t = x.shape[0]
    n = n1 * FFT_N2
    c1, s1 = _dft_mats(n1, 1.0 / math.sqrt(n1))
    k1 = np.arange(n1)[:, None]
    n2 = np.arange(FFT_N2)[None, :]
    ang = 2.0 * np.pi * ((k1 * n2) % n) / n
    nj = FFT_N2 // SUB
    nkc = n1 // SUB
    to_tab = lambda a: jnp.asarray(a.reshape(n1, nj, SUB).transpose(1, 0, 2), F32)
    twc, tws = to_tab(np.cos(ang)), to_tab(np.sin(ang))
    xv = x.reshape(nb * n1, FFT_N2, D_MODEL)
    out_blk = pl.BlockSpec((nkc, SUB, SUB, D_MODEL), lambda b, j: (b, j, 0, 0))
    out_shape = jax.ShapeDtypeStruct((nb * nkc, FFT_N2, SUB, D_MODEL), F32)
    tws_spec = pl.BlockSpec((1, n1, SUB), lambda b, j: (j, 0, 0))
    mat = pl.BlockSpec((n1, n1), lambda b, j: (0, 0))
    br, bi = pl.pallas_call(
        _fft1_kernel,
        grid=(nb, nj),
        in_specs=_lane_slab_specs((n1, SUB), lambda b, j: (b, j))
        + [pl.BlockSpec((1, D_MODEL), lambda b, j: (0, 0)), mat, mat, tws_spec, tws_spec],
        out_specs=[out_blk, out_blk],
        out_shape=[out_shape, out_shape],
        scratch_shapes=[pltpu.VMEM((LANE_SLABS, n1 * SUB, LANES), F32)],
        compiler_params=_cparams(("arbitrary", "arbitrary")),
        name="fft1",
    )(*([xv] * LANE_SLABS), g, c1, s1, twc, tws)
    return br.reshape(t, D_MODEL), bi.reshape(t, D_MODEL)


FFT_PAIR = 2


def _fft2_kernel(*refs):
    brs, bis, xs = (refs[i * LANE_SLABS:(i + 1) * LANE_SLABS] for i in range(3))
    (c2_ref, s2_ref, cc_ref, sc_ref, wout_ref, mg_ref, wrh_ref, wrl_ref,
     x1_ref, lg_ref, f_scr, x_scr) = refs[3 * LANE_SLABS:]
    n2 = FFT_N2
    for c in range(LANE_SLABS):
        x_scr[c] = xs[c][...].reshape(n2 * SUB, LANES)
    for jp in range(SUB // FFT_PAIR):
        for q in range(FFT_PAIR):
            j = jp * FFT_PAIR + q
            bjr = _every_8th_row(brs, j, n2).astype(BF16)
            bji = _every_8th_row(bis, j, n2).astype(BF16)
            yr = (_dot(c2_ref[...], bjr) + _dot(s2_ref[...], bji)).astype(BF16)
            yi = (_dot(c2_ref[...], bji) - _dot(s2_ref[...], bjr)).astype(BF16)
            for g in range(C_GROUPS):
                cols = slice(g * C_GROUP_DIM, (g + 1) * C_GROUP_DIM)
                f = _dot(yr[:, cols], cc_ref[...]) + _dot(yi[:, cols], sc_ref[...])
                f_scr[q * n2:(q + 1) * n2, cols] = f.astype(BF16)
        y = _dot(f_scr[...], wout_ref[...])
        for q in range(FFT_PAIR):
            j = jp * FFT_PAIR + q
            x1 = _every_8th_row(x_scr, j, n2) + y[q * n2:(q + 1) * n2, :]
            x1_ref[:, j, :] = x1
            lg_ref[:, j * n2:(j + 1) * n2] = _router_logits(x1, mg_ref, wrh_ref, wrl_ref)


def _fft2(br, bi, x, nb, n1, wout, mg, wr):
    t = x.shape[0]
    n2 = FFT_N2
    c2, s2 = _dft_mats(n2, 1.0 / math.sqrt(n2))
    cc, sc = _dft_mats(C_GROUP_DIM, 1.0 / math.sqrt(C_GROUP_DIM))
    nkc = n1 // SUB
    rows = SUB * n2
    xv = x.reshape(nb * n2, n1, D_MODEL)
    full = lambda shape: pl.BlockSpec(shape, lambda b, k: (0,) * len(shape))
    b_slabs = _lane_slab_specs((rows,), lambda b, k: (b * nkc + k,))
    x_slabs = _lane_slab_specs((n2, SUB), lambda b, k: (b, k))
    x1, lg = pl.pallas_call(
        _fft2_kernel,
        grid=(nb, nkc),
        in_specs=b_slabs * 2 + x_slabs
        + [full((n2, n2)), full((n2, n2)),
           full((C_GROUP_DIM, C_GROUP_DIM)), full((C_GROUP_DIM, C_GROUP_DIM)),
           full((D_MODEL, D_MODEL)), full((1, D_MODEL)),
           full((N_EXPERTS, D_MODEL)), full((N_EXPERTS, D_MODEL))],
        out_specs=[pl.BlockSpec((n2, SUB, D_MODEL), lambda b, k: (b, k, 0)),
                   pl.BlockSpec((N_EXPERTS, rows), lambda b, k: (0, b * nkc + k))],
        out_shape=[jax.ShapeDtypeStruct(xv.shape, F32),
                   jax.ShapeDtypeStruct((N_EXPERTS, t), F32)],
        scratch_shapes=[pltpu.VMEM((FFT_PAIR * n2, D_MODEL), BF16),
                        pltpu.VMEM((LANE_SLABS, rows, LANES), F32)],
        compiler_params=_cparams(("arbitrary", "arbitrary")),
        name="fft2",
    )(*([br] * LANE_SLABS), *([bi] * LANE_SLABS), *([xv] * LANE_SLABS),
      c2, s2, cc, sc, wout, mg, *wr)
    lg = lg.reshape(N_EXPERTS, nb, n1, n2).transpose(0, 1, 3, 2).reshape(N_EXPERTS, t)
    return x1.reshape(t, D_MODEL), lg


def _moe(x1, logits_t, mg, p, layer, g_final=None):
    t = x1.shape[0]
    cap = CAPACITY_FACTOR * t // N_EXPERTS
    idx, gate, cb, pos = _route(logits_t, cap)
    ye = _ffn(idx, gate, x1, mg, p["wg"], p["wu"], p["wd"], layer)
    return _combine(cb, pos, x1, ye, g_final)


def _trunk(x3, p):
    nb, s, _ = x3.shape
    x = x3.reshape(nb * s, D_MODEL)
    row = lambda v: v.reshape(1, -1)
    outa, glu = _front(x, row(p["mix_norm"][0]), p["w_in"], row(p["ab_v_norm_g"][0]),
                       row(p["ab_v_norm_b"][0]), p["w_spatial"], p["b_spatial"])
    mg0, mg1 = row(p["moe_norm"][0]), row(p["moe_norm"][1])
    x1, lg = _back(x, outa, glu, p["ab_conv_w"][0], row(p["ab_conv_b"][0]),
                   row(p["ab_conv_norm_g"][0]), row(p["ab_conv_norm_b"][0]), p["ab_w_out"],
                   mg0, p["router"][0], s)
    x2 = _moe(x1, lg, mg0, p, 0)
    n1 = s // FFT_N2
    br, bi = _fft1(x2, row(p["mix_norm"][1]), nb, n1)
    x1, lg = _fft2(br, bi, x2, nb, n1, p["c_w_out"], mg1, p["router"][1])
    y = _moe(x1, lg, mg1, p, 1, row(p["final_norm"]))
    return y.reshape(nb, s, D_MODEL)


def kernel(x_prompt, x_sample, mix_norm, ab_w_in, ab_v_norm_g, ab_v_norm_b, ab_w_spatial,
           ab_b_spatial, ab_conv_w, ab_conv_b, ab_conv_norm_g, ab_conv_norm_b, ab_w_out,
           c_w_out, moe_norm, moe_router, moe_w_gate, moe_w_up, moe_w_down, final_norm):
    p = dict(
        mix_norm=mix_norm, moe_norm=moe_norm, final_norm=final_norm,
        w_in=ab_w_in[0].astype(BF16),
        ab_v_norm_g=ab_v_norm_g, ab_v_norm_b=ab_v_norm_b,
        w_spatial=ab_w_spatial[0].astype(BF16),
        b_spatial=jnp.repeat(jnp.transpose(ab_b_spatial[0]), A_HEAD_DIM, axis=1),
        ab_conv_w=ab_conv_w, ab_conv_b=ab_conv_b,
        ab_conv_norm_g=ab_conv_norm_g, ab_conv_norm_b=ab_conv_norm_b,
        ab_w_out=ab_w_out[0].astype(BF16),
        c_w_out=c_w_out[0].astype(BF16),
        router=[_split2(jnp.transpose(moe_router[l])) for l in range(moe_router.shape[0])],
        wg=moe_w_gate, wu=moe_w_up, wd=moe_w_down,
    )
    return (_trunk(x_prompt, p), _trunk(x_sample, p))
```

```python
import functools
import math

import numpy as np
import jax
import jax.numpy as jnp
from jax import lax
from jax.experimental import pallas as pl
from jax.experimental.pallas import tpu as pltpu

F32, BF16, I32 = jnp.float32, jnp.bfloat16, jnp.int32

D_MODEL = 1024
D_A = 512
D_B = 512
A_GROUPS = 4
A_HEAD_DIM = 128
CHUNK = 128
CONV_WIDTH = 31
CONV_PAD = 15
C_GROUPS = 4
C_GROUP_DIM = 256
N_EXPERTS = 16
CAPACITY_FACTOR = 2
RMS_EPS = 1e-6
LN_EPS = 1e-5

LANES = 128
SUB = 8
LANE_SLABS = D_MODEL // LANES
HALO = 16
TM = 512
RB = 512
TB = 512
FFT_N2 = 128
VMEM_LIMIT = 48 * 1024 * 1024
BISECT_BITS_STEPS = 31
BISECT_VALUE_STEPS = 30


def _cparams(sem):
    return pltpu.CompilerParams(dimension_semantics=sem, vmem_limit_bytes=VMEM_LIMIT)


def _dot(a, b):
    return jnp.dot(a, b, preferred_element_type=F32)


def _dot_nt(a, b, precision=None):
    return lax.dot_general(a, b, (((1,), (1,)), ((), ())), precision=precision,
                           preferred_element_type=F32)


def _rms(x, g):
    return x * lax.rsqrt(jnp.mean(x * x, axis=-1, keepdims=True) + RMS_EPS) * g


def _ln(x, g, b):
    mu = jnp.mean(x, axis=-1, keepdims=True)
    xc = x - mu
    var = jnp.mean(xc * xc, axis=-1, keepdims=True)
    return xc * lax.rsqrt(var + LN_EPS) * g + b


def _gelu(x):
    return 0.5 * x * (1.0 + lax.erf(x * np.float32(math.sqrt(0.5))))


def _silu(x):
    return x * jax.nn.sigmoid(x)


def _front_kernel(x_ref, g_ref, win_ref, vg_ref, vb_ref, wsp_ref, bsp_ref,
                  outa_ref, glu_ref, zu_scr, vn_scr):
    xn = _rms(x_ref[...], g_ref[...]).astype(BF16)
    zu_scr[...] = _gelu(_dot(xn, win_ref[:, 0:D_A]))
    zv = _gelu(_dot(xn, win_ref[:, D_A:2 * D_A]))
    vn_scr[...] = _ln(zv, vg_ref[...], vb_ref[...]).astype(BF16)
    a_in = _dot(xn, win_ref[:, 2 * D_A:2 * D_A + D_B])
    gate = _dot(xn, win_ref[:, 2 * D_A + D_B:])
    glu_ref[...] = a_in * jax.nn.sigmoid(gate)
    for c in range(TM // CHUNK):
        rows = slice(c * CHUNK, (c + 1) * CHUNK)
        for h in range(A_GROUPS):
            cols = slice(h * A_HEAD_DIM, (h + 1) * A_HEAD_DIM)
            mixed = _dot(wsp_ref[h], vn_scr[rows, cols]) + bsp_ref[:, cols]
            outa_ref[rows, cols] = (zu_scr[rows, cols] * mixed).astype(BF16)


def _front(x, g, w_in, vg, vb, wsp, bsp):
    t = x.shape[0]
    full = lambda shape: pl.BlockSpec(shape, lambda i: (0,) * len(shape))
    return pl.pallas_call(
        _front_kernel,
        grid=(t // TM,),
        in_specs=[
            pl.BlockSpec((TM, D_MODEL), lambda i: (i, 0)),
            full((1, D_MODEL)),
            full((D_MODEL, 2 * D_A + 2 * D_B)),
            full((1, D_A)), full((1, D_A)),
            full((A_GROUPS, CHUNK, CHUNK)),
            full((CHUNK, D_A)),
        ],
        out_specs=[pl.BlockSpec((TM, D_A), lambda i: (i, 0)),
                   pl.BlockSpec((TM, D_B), lambda i: (i, 0))],
        out_shape=[jax.ShapeDtypeStruct((t, D_A), BF16),
                   jax.ShapeDtypeStruct((t, D_B), F32)],
        scratch_shapes=[pltpu.VMEM((TM, D_A), F32), pltpu.VMEM((TM, D_A), BF16)],
        compiler_params=_cparams(("arbitrary",)),
        name="front",
    )(x, g, w_in, vg, vb, wsp, bsp)


CONV_ROWS = 64
SUBLANES = 8
SHIFT_ROWS = TM + (HALO + CONV_PAD) // SUBLANES * SUBLANES


def _split2(a):
    hi = a.astype(BF16)
    return hi, (a - hi.astype(F32)).astype(BF16)


def _store_token_tiles(tiles_ref, x):
    rows = x.shape[0]
    for c in range(LANE_SLABS):
        tiles_ref[pl.ds(c, rows, stride=SUB), :] = x[:, c * LANES:(c + 1) * LANES]


def _load_token_tiles(tiles_ref, rows, lead=()):
    parts = [tiles_ref[lead + (pl.ds(c, rows, stride=SUB), slice(None))] for c in range(LANE_SLABS)]
    return jnp.concatenate(parts, axis=1)


def _router_logits(x1, mg_ref, wrh_ref, wrl_ref):
    xh, xl = _split2(_rms(x1, mg_ref[...]))
    return _dot_nt(wrh_ref[...], xh) + (_dot_nt(wrh_ref[...], xl) + _dot_nt(wrl_ref[...], xh))


def _back_kernel(x_ref, outa_ref, glu_ref, prev_ref, next_ref, cw_ref, cb_ref, ng_ref, nb_ref,
                 wout_ref, mg_ref, wrh_ref, wrl_ref, x1t_ref, lg_ref, ext_scr, sh_scr, ob_scr,
                 *, blocks_per_seq):
    i = pl.program_id(0)
    pos = lax.rem(i, blocks_per_seq)
    ext_scr[0:HALO, :] = jnp.where(pos == 0, 0.0, prev_ref[...])
    ext_scr[HALO:HALO + TM, :] = glu_ref[...]
    ext_scr[HALO + TM:, :] = jnp.where(pos == blocks_per_seq - 1, 0.0, next_ref[...])
    for b in range(SUBLANES):
        sh_scr[b] = ext_scr[b:b + SHIFT_ROWS, :]
    for r in range(TM // CONV_ROWS):
        acc = jnp.zeros((CONV_ROWS, D_B), F32) + cb_ref[...]
        for k in range(CONV_WIDTH):
            a, b = divmod(HALO - CONV_PAD + k, SUBLANES)
            row0 = r * CONV_ROWS + a * SUBLANES
            acc = acc + cw_ref[k:k + 1, :] * sh_scr[b, row0:row0 + CONV_ROWS, :]
        ob = _silu(_ln(acc, ng_ref[...], nb_ref[...]))
        ob_scr[r * CONV_ROWS:(r + 1) * CONV_ROWS, :] = ob.astype(BF16)
    y = _dot(outa_ref[...], wout_ref[0:D_A, :]) + _dot(ob_scr[...], wout_ref[D_A:, :])
    x1 = x_ref[...] + y
    _store_token_tiles(x1t_ref, x1)
    lg_ref[...] = _router_logits(x1, mg_ref, wrh_ref, wrl_ref)


def _back(x, outa, glu, cw, cb, ng, nb_, wout, mg, wr, seq_len):
    t = x.shape[0]
    full = lambda shape: pl.BlockSpec(shape, lambda i: (0,) * len(shape))
    hb = TM // HALO
    return pl.pallas_call(
        functools.partial(_back_kernel, blocks_per_seq=seq_len // TM),
        grid=(t // TM,),
        in_specs=[
            pl.BlockSpec((TM, D_MODEL), lambda i: (i, 0)),
            pl.BlockSpec((TM, D_A), lambda i: (i, 0)),
            pl.BlockSpec((TM, D_B), lambda i: (i, 0)),
            pl.BlockSpec((HALO, D_B), lambda i: (jnp.maximum(i * hb - 1, 0), 0)),
            pl.BlockSpec((HALO, D_B), lambda i: (jnp.minimum((i + 1) * hb, t // HALO - 1), 0)),
            full((CONV_WIDTH, D_B)), full((1, D_B)), full((1, D_B)), full((1, D_B)),
            full((D_A + D_B, D_MODEL)),
            full((1, D_MODEL)),
            full((N_EXPERTS, D_MODEL)), full((N_EXPERTS, D_MODEL)),
        ],
        out_specs=[pl.BlockSpec((TM * SUB, LANES), lambda i: (i, 0)),
                   pl.BlockSpec((N_EXPERTS, TM), lambda i: (0, i))],
        out_shape=[jax.ShapeDtypeStruct((t * SUB, LANES), F32),
                   jax.ShapeDtypeStruct((N_EXPERTS, t), F32)],
        scratch_shapes=[pltpu.VMEM((TM + 2 * HALO, D_B), F32),
                        pltpu.VMEM((SUBLANES, SHIFT_ROWS, D_B), F32),
                        pltpu.VMEM((TM, D_B), BF16)],
        compiler_params=_cparams(("arbitrary",)),
        name="back",
    )(x, outa, glu, glu, glu, cw, cb, ng, nb_, wout, mg, *wr)


def _split3(a):
    a1 = a.astype(BF16)
    r1 = a - a1.astype(F32)
    a2 = r1.astype(BF16)
    a3 = (r1 - a2.astype(F32)).astype(BF16)
    return a1, a2, a3


def _route_kernel(lg_ref, idx_ref, gate_ref, cb_ref, pos_ref, lm_scr, w_scr, a1_scr, a2_scr, a3_scr,
                  cnt_scr, cbr_scr, *, nbk, cap):
    ne = N_EXPERTS
    lg = lg_ref[...]
    ex = jnp.exp(lg - jnp.max(lg, axis=0, keepdims=True))
    aff = ex / jnp.sum(ex, axis=0, keepdims=True)

    def count(m):
        s = jnp.sum(m.astype(F32), axis=1, keepdims=True)
        return jnp.sum(s, axis=2, keepdims=True)

    def update(mid, lo, hi):
        ge = count(aff >= mid) >= cap
        return jnp.where(ge, mid, lo), jnp.where(ge, hi, mid)

    def bits_body(_, c):
        lo, hi = c
        lo_i = lax.bitcast_convert_type(lo, I32)
        hi_i = lax.bitcast_convert_type(hi, I32)
        mid = lax.bitcast_convert_type(lo_i + lax.shift_right_logical(hi_i - lo_i, 1), F32)
        return update(mid, lo, hi)

    def value_body(_, c):
        lo, hi = c
        return update(lo + (hi - lo) * 0.5, lo, hi)

    lo0 = jnp.zeros((ne, 1, 1), F32)
    hi0 = jnp.full((ne, 1, 1), jnp.inf, F32)
    c = lax.fori_loop(0, BISECT_BITS_STEPS, bits_body, (lo0, hi0))
    lo, hi = lax.fori_loop(0, BISECT_VALUE_STEPS, value_body, c)

    li = lax.broadcasted_iota(I32, (LANES, LANES), 0)
    lj = lax.broadcasted_iota(I32, (LANES, LANES), 1)
    ut = (li <= lj).astype(BF16)
    ones = jnp.ones((LANES, LANES), BF16)
    bi = lax.broadcasted_iota(I32, (nbk, nbk), 0)
    bj = lax.broadcasted_iota(I32, (nbk, nbk), 1)
    slt = (bj < bi).astype(BF16)
    utb = (bi <= bj).astype(BF16)
    ones8 = jnp.ones((8, LANES), BF16)

    def block_cum(m):
        m2 = m.astype(BF16).reshape(ne * nbk, LANES)
        loc = _dot(m2, ut).reshape(ne, nbk, LANES)
        tot = _dot(m2, ones).astype(BF16).reshape(ne, nbk, LANES)
        off = jnp.stack([_dot(slt, tot[e]) for e in range(ne)])
        return loc, tot, off

    gt = aff >= hi
    eq = (aff >= lo) & (aff < hi)
    need = np.float32(cap) - count(gt)
    eloc, _, eoff = block_cum(eq)
    mask = gt | (eq & (eloc + eoff <= need))

    mloc, mtot, moff = block_cum(mask)
    pos_ref[...] = jnp.where(mask, mloc + moff - 1.0, -1.0).astype(I32)
    lm_scr[...] = mloc.astype(BF16)
    w_scr[...] = mtot
    a1, a2, a3 = _split3(aff)
    a1_scr[...] = a1
    a2_scr[...] = a2
    a3_scr[...] = a3
    maskb = mask.astype(BF16)
    for e in range(ne):
        cnt = _dot_nt(ones8, maskb[e])
        cnt_scr[e] = cnt
        cbr_scr[e] = _dot(cnt.astype(BF16), utb)
    cb_ref[...] = cbr_scr[:, 0, :].astype(I32)

    s_col = lax.broadcasted_iota(I32, (cap, 1), 0).astype(F32)
    lane_b = lax.broadcasted_iota(I32, (cap, nbk), 1).astype(F32)
    lane_l = lax.broadcasted_iota(I32, (cap, LANES), 1).astype(F32)
    ones_b = jnp.ones((nbk, LANES), BF16)
    ones8b = jnp.ones((8, nbk), BF16)

    def per_expert(e, _):
        cbrow = cbr_scr[e][0:1, :]
        le = (cbrow <= s_col).astype(BF16)
        blk_c = _dot(le, ones_b)
        start_c = _dot(le, w_scr[e])
        oh = (lane_b == blk_c[:, 0:nbk]).astype(BF16)
        g_loc = _dot(oh, lm_scr[e])
        le2 = (g_loc <= s_col - start_c).astype(BF16)
        within_c = _dot(le2, ones)
        blk_r = _dot_nt(ones8b, le)
        within_r = _dot_nt(ones8, le2)
        idx_ref[pl.ds(e, 1), :] = (blk_r * LANES + within_r)[0:1, :].astype(I32)
        pick = lane_l == within_c
        gsum = None
        for a_scr in (a1_scr, a2_scr, a3_scr):
            sel = jnp.where(pick, _dot(oh, a_scr[e]), 0.0).astype(BF16)
            part = _dot_nt(ones8, sel)
            gsum = part if gsum is None else gsum + part
        gate_ref[pl.ds(e, 1), :] = gsum[0:1, :]
        return 0

    lax.fori_loop(0, ne, per_expert, 0)


def _route(logits_t, cap):
    ne, t = logits_t.shape
    nbk = t // LANES
    lg3 = logits_t.reshape(ne, nbk, LANES)
    full = lambda shape: pl.BlockSpec(shape, lambda i: (0,) * len(shape))
    tab = lambda dt: pltpu.VMEM((ne, nbk, LANES), dt)
    return pl.pallas_call(
        functools.partial(_route_kernel, nbk=nbk, cap=cap),
        grid=(1,),
        in_specs=[full((ne, nbk, LANES))],
        out_specs=[full((ne, cap)), full((ne, cap)), full((ne, nbk)), full((ne, nbk, LANES))],
        out_shape=[jax.ShapeDtypeStruct((ne, cap), I32),
                   jax.ShapeDtypeStruct((ne, cap), F32),
                   jax.ShapeDtypeStruct((ne, nbk), I32),
                   jax.ShapeDtypeStruct((ne, nbk, LANES), I32)],
        scratch_shapes=[tab(BF16), tab(BF16), tab(BF16), tab(BF16), tab(BF16),
                        pltpu.VMEM((ne, 8, nbk), F32), pltpu.VMEM((ne, 8, nbk), F32)],
        compiler_params=_cparams(("arbitrary",)),
        name="route",
    )(lg3)


def _ffn_kernel(idx_ref, idxn_ref, gate_ref, x_hbm, x2d_hbm, mg_ref, wg_hbm, wu_hbm, wd_hbm, ye_ref,
                xbuf, wst, wbf, sem, wsem, *, layer, ne, nsteps, rb):
    e = pl.program_id(0)
    step = e * pl.num_programs(1) + pl.program_id(1)
    slot = lax.rem(step, 2)

    def weight_copies(ex):
        return [pltpu.make_async_copy(w_hbm.at[layer, ex], wst.at[m], wsem.at[m])
                for m, w_hbm in enumerate((wg_hbm, wu_hbm, wd_hbm))]

    def issue(ids_ref, s):
        for r in range(rb):
            pltpu.make_async_copy(x_hbm.at[ids_ref[0, 0, r]], xbuf.at[s, pl.ds(r * SUB, SUB)],
                                  sem.at[s]).start()

    def wait(s):
        pltpu.make_async_copy(x2d_hbm.at[pl.ds(0, rb * SUB)], xbuf.at[s], sem.at[s]).wait()

    @pl.when(step == 0)
    def _():
        for cp in weight_copies(0):
            cp.start()
        issue(idx_ref, 0)

    @pl.when(pl.program_id(1) == 0)
    def _():
        for m, cp in enumerate(weight_copies(e)):
            cp.wait()
            wbf[m] = wst[m].astype(BF16)

        @pl.when(e + 1 < ne)
        def _():
            for cp in weight_copies(e + 1):
                cp.start()

    issue(idxn_ref, 1 - slot)
    wait(slot)

    xe = _rms(_load_token_tiles(xbuf, rb, (slot,)), mg_ref[...]).astype(BF16)
    hid = (_silu(_dot(xe, wbf[0])) * _dot(xe, wbf[1])).astype(BF16)
    ye = _dot(hid, wbf[2])
    ri = lax.broadcasted_iota(I32, (rb, rb), 0)
    ci = lax.broadcasted_iota(I32, (rb, rb), 1)
    gcol = jnp.sum(jnp.where(ri == ci, gate_ref[0], 0.0), axis=1, keepdims=True)
    ye_ref[...] = (ye * gcol).astype(BF16)

    @pl.when(step == nsteps - 1)
    def _():
        wait(1 - slot)


def _ffn(idx, gate, x1t, mg, wg, wu, wd, layer):
    ne, cap = idx.shape
    t = x1t.shape[0] // SUB
    rb = min(RB, cap)
    nblk = cap // rb
    nsteps = ne * nblk
    idx3 = idx.reshape(nsteps, 1, rb)
    gate3 = gate.reshape(nsteps, 1, rb)
    hbm = pl.BlockSpec(memory_space=pl.ANY)
    return pl.pallas_call(
        functools.partial(_ffn_kernel, layer=layer, ne=ne, nsteps=nsteps, rb=rb),
        grid=(ne, nblk),
        in_specs=[
            pl.BlockSpec((1, 1, rb), lambda e, j: (e * nblk + j, 0, 0), memory_space=pltpu.SMEM),
            pl.BlockSpec((1, 1, rb), lambda e, j: (jnp.minimum(e * nblk + j + 1, nsteps - 1), 0, 0),
                         memory_space=pltpu.SMEM),
            pl.BlockSpec((1, 1, rb), lambda e, j: (e * nblk + j, 0, 0)),
            hbm, hbm,
            pl.BlockSpec((1, D_MODEL), lambda e, j: (0, 0)),
            hbm, hbm, hbm,
        ],
        out_specs=pl.BlockSpec((rb, D_MODEL), lambda e, j: (e * nblk + j, 0)),
        out_shape=jax.ShapeDtypeStruct((ne * cap, D_MODEL), BF16),
        scratch_shapes=[pltpu.VMEM((2, rb * SUB, LANES), F32),
                        pltpu.VMEM((3, D_MODEL, D_MODEL), F32),
                        pltpu.VMEM((3, D_MODEL, D_MODEL), BF16),
                        pltpu.SemaphoreType.DMA((2,)), pltpu.SemaphoreType.DMA((3,))],
        compiler_params=_cparams(("arbitrary", "arbitrary")),
        name="ffn",
    )(idx3, idx3, gate3, x1t.reshape(t, SUB, LANES), x1t, mg, wg, wu, wd)


CHUNK_ROWS = 16
MAIN_CHUNKS = 96
MAIN_ROWS = CHUNK_ROWS * MAIN_CHUNKS
MAX_CHUNKS = -(-(N_EXPERTS * TB // CHUNK_ROWS + 2 * N_EXPERTS) // MAIN_CHUNKS) * MAIN_CHUNKS
NO_SLOT = 1 << 30


def _combine_kernel(cb_ref, x1t_ref, pos_ref, ye_hbm, *rest, nbk, cap, ntb, final):
    if final:
        g_ref, out_ref, ybuf, acc_scr, sem, ie_scr, ir_scr, n_scr = rest
    else:
        out_ref, ybuf, acc_scr, sem, ie_scr, ir_scr, n_scr = rest
    tb = pl.program_id(0)
    s = lax.rem(tb, 2)
    bpt = TB // LANES
    shift = CHUNK_ROWS.bit_length() - 1

    def build(tbx, par):
        base = par * MAX_CHUNKS
        n = jnp.int32(0)
        for e in range(N_EXPERTS):
            b0 = e * nbk + tbx * bpt
            lo = jnp.where(tbx == 0, 0, cb_ref[jnp.maximum(b0 - 1, 0)])
            hi = cb_ref[b0 + bpt - 1]
            start = lax.shift_left(lax.shift_right_logical(lo, shift), shift)
            nch = jnp.where(hi > lo,
                            lax.shift_right_logical(hi - start + (CHUNK_ROWS - 1), shift), 0)

            def push(c, k, e=e, start=start):
                ie_scr[base + k] = e
                ir_scr[base + k] = start + c * CHUNK_ROWS
                return k + 1

            n = lax.fori_loop(0, nch, push, n)
        nsg = jnp.maximum(lax.div(n + (MAIN_CHUNKS - 1), MAIN_CHUNKS), 1)

        def pad(k, _):
            ie_scr[base + k] = 0
            ir_scr[base + k] = NO_SLOT
            return 0

        lax.fori_loop(n, nsg * MAIN_CHUNKS, pad, 0)
        n_scr[par] = n
        return n

    def chunk_copy(par, k, k0):
        base = par * MAX_CHUNKS
        row0 = pl.multiple_of(ie_scr[base + k] * cap + ir_scr[base + k], CHUNK_ROWS)
        dst0 = pl.multiple_of((k - k0) * CHUNK_ROWS, CHUNK_ROWS)
        return pltpu.make_async_copy(ye_hbm.at[pl.ds(row0, CHUNK_ROWS)],
                                     ybuf.at[par, pl.ds(dst0, CHUNK_ROWS)], sem.at[par])

    def issue(par, k0, k1):
        def body(k, _):
            chunk_copy(par, k, k0).start()
            return 0
        lax.fori_loop(k0, k1, body, 0)

    def drain(par, k0, k1):
        def body(k, _):
            chunk_copy(par, k, k0).wait()
            return 0
        lax.fori_loop(k0, k1, body, 0)

    @pl.when(tb == 0)
    def _():
        ybuf[...] = jnp.zeros_like(ybuf)
        n0 = build(0, 0)
        issue(0, 0, jnp.minimum(n0, MAIN_CHUNKS))

    @pl.when(tb + 1 < ntb)
    def _():
        n1 = build(tb + 1, 1 - s)
        issue(1 - s, 0, jnp.minimum(n1, MAIN_CHUNKS))

    jrow = lax.broadcasted_iota(I32, (CHUNK_ROWS, TB), 0)

    def scatter(sg):
        base = s * MAX_CHUNKS + sg * MAIN_CHUNKS
        pieces = []
        for c in range(MAIN_CHUNKS):
            prow = pos_ref[pl.ds(ie_scr[base + c], 1), :]
            pieces.append((prow == jrow + ir_scr[base + c]).astype(BF16))
        onehot_t = jnp.concatenate(pieces, axis=0)
        return lax.dot_general(onehot_t, ybuf[s], (((0,), (0,)), ((), ())),
                               preferred_element_type=F32)

    def finish(x2):
        return _rms(x2, g_ref[...]) if final else x2

    n = n_scr[s]
    drain(s, 0, jnp.minimum(n, MAIN_CHUNKS))
    nsg = lax.div(n + (MAIN_CHUNKS - 1), MAIN_CHUNKS)

    @pl.when(nsg <= 1)
    def _():
        out_ref[...] = finish(_load_token_tiles(x1t_ref, TB) + scatter(0))

    @pl.when(nsg > 1)
    def _():
        acc_scr[...] = _load_token_tiles(x1t_ref, TB) + scatter(0)

        def extra(sg, _):
            k0 = sg * MAIN_CHUNKS
            k1 = jnp.minimum(n, k0 + MAIN_CHUNKS)
            issue(s, k0, k1)
            drain(s, k0, k1)
            acc_scr[...] += scatter(sg)
            return 0

        lax.fori_loop(1, nsg, extra, 0)
        out_ref[...] = finish(acc_scr[...])


def _combine(cb, pos, x1t, ye, g_final=None):
    t = x1t.shape[0] // SUB
    ne, nbk = cb.shape
    cap = ye.shape[0] // ne
    final = g_final is not None
    blk = pl.BlockSpec((TB, D_MODEL), lambda i, cb: (i, 0))
    in_specs = [pl.BlockSpec((TB * SUB, LANES), lambda i, cb: (i, 0)),
                pl.BlockSpec((ne, TB), lambda i, cb: (0, i)), pl.BlockSpec(memory_space=pl.ANY)]
    args = [cb.reshape(-1), x1t, pos.reshape(ne, t), ye]
    if final:
        in_specs.append(pl.BlockSpec((1, D_MODEL), lambda i, cb: (0, 0)))
        args.append(g_final)
    grid_spec = pltpu.PrefetchScalarGridSpec(
        num_scalar_prefetch=1,
        grid=(t // TB,),
        in_specs=in_specs,
        out_specs=blk,
        scratch_shapes=[pltpu.VMEM((2, MAIN_ROWS, D_MODEL), BF16),
                        pltpu.VMEM((TB, D_MODEL), F32),
                        pltpu.SemaphoreType.DMA((2,)),
                        pltpu.SMEM((2 * MAX_CHUNKS,), I32), pltpu.SMEM((2 * MAX_CHUNKS,), I32),
                        pltpu.SMEM((2,), I32)],
    )
    return pl.pallas_call(
        functools.partial(_combine_kernel, nbk=nbk, cap=cap, ntb=t // TB, final=final),
        grid_spec=grid_spec,
        out_shape=jax.ShapeDtypeStruct((t, D_MODEL), F32),
        compiler_params=_cparams(("arbitrary",)),
        name="combine_final" if final else "combine",
    )(*args)


def _dft_mats(n, scale):
    k = np.arange(n)
    ang = 2.0 * np.pi * ((k[:, None] * k[None, :]) % n) / n
    return (jnp.asarray(np.cos(ang) * scale, F32).astype(BF16),
            jnp.asarray(np.sin(ang) * scale, F32).astype(BF16))


def _lane_slab_specs(block, index_map):
    return [pl.BlockSpec(block + (LANES,), functools.partial(lambda *i, c: index_map(*i) + (c,), c=c))
            for c in range(LANE_SLABS)]


def _every_8th_row(slabs, j, rows):
    parts = [slabs[c, pl.ds(j, rows, stride=SUB), :] if not isinstance(slabs, (list, tuple))
             else slabs[c][pl.ds(j, rows, stride=SUB), :] for c in range(LANE_SLABS)]
    return jnp.concatenate(parts, axis=1)


def _fft1_kernel(*refs):
    xs = refs[:LANE_SLABS]
    g_ref, c1_ref, s1_ref, twc_ref, tws_ref, br_ref, bi_ref, x_scr = refs[LANE_SLABS:]
    n1 = c1_ref.shape[0]
    for c in range(LANE_SLABS):
        x_scr[c] = xs[c][...].reshape(n1 * SUB, LANES)
    for j in range(SUB):
        xj = _rms(_every_8th_row(x_scr, j, n1), g_ref[...]).astype(BF16)
        ar = _dot(c1_ref[...], xj)
        ai = -_dot(s1_ref[...], xj)
        c = twc_ref[0, :, j:j + 1]
        s = tws_ref[0, :, j:j + 1]
        br_ref[:, j] = (ar * c + ai * s).reshape(n1 // SUB, SUB, D_MODEL)
        bi_ref[:, j] = (ai * c - ar * s).reshape(n1 // SUB, SUB, D_MODEL)


def _fft1(x, g, nb, n1):
    t = x.shape[0]
    n = n1 * FFT_N2
    c1, s1 = _dft_mats(n1, 1.0 / math.sqrt(n1))
    k1 = np.arange(n1)[:, None]
    n2 = np.arange(FFT_N2)[None, :]
    ang = 2.0 * np.pi * ((k1 * n2) % n) / n
    nj = FFT_N2 // SUB
    nkc = n1 // SUB
    to_tab = lambda a: jnp.asarray(a.reshape(n1, nj, SUB).transpose(1, 0, 2), F32)
    twc, tws = to_tab(np.cos(ang)), to_tab(np.sin(ang))
    xv = x.reshape(nb * n1, FFT_N2, D_MODEL)
    out_blk = pl.BlockSpec((nkc, SUB, SUB, D_MODEL), lambda b, j: (b, j, 0, 0))
    out_shape = jax.ShapeDtypeStruct((nb * nkc, FFT_N2, SUB, D_MODEL), F32)
    tws_spec = pl.BlockSpec((1, n1, SUB), lambda b, j: (j, 0, 0))
    mat = pl.BlockSpec((n1, n1), lambda b, j: (0, 0))
    br, bi = pl.pallas_call(
        _fft1_kernel,
        grid=(nb, nj),
        in_specs=_lane_slab_specs((n1, SUB), lambda b, j: (b, j))
        + [pl.BlockSpec((1, D_MODEL), lambda b, j: (0, 0)), mat, mat, tws_spec, tws_spec],
        out_specs=[out_blk, out_blk],
        out_shape=[out_shape, out_shape],
        scratch_shapes=[pltpu.VMEM((LANE_SLABS, n1 * SUB, LANES), F32)],
        compiler_params=_cparams(("arbitrary", "arbitrary")),
        name="fft1",
    )(*([xv] * LANE_SLABS), g, c1, s1, twc, tws)
    return br.reshape(t, D_MODEL), bi.reshape(t, D_MODEL)


FFT_PAIR = 2


def _fft2_kernel(*refs):
    brs, bis, xs = (refs[i * LANE_SLABS:(i + 1) * LANE_SLABS] for i in range(3))
    (c2_ref, s2_ref, cc_ref, sc_ref, wout_ref, mg_ref, wrh_ref, wrl_ref,
     x1t_ref, lg_ref, f_scr, x_scr) = refs[3 * LANE_SLABS:]
    n2 = FFT_N2
    for c in range(LANE_SLABS):
        x_scr[c] = xs[c][...].reshape(n2 * SUB, LANES)
    for jp in range(SUB // FFT_PAIR):
        for q in range(FFT_PAIR):
            j = jp * FFT_PAIR + q
            bjr = _every_8th_row(brs, j, n2).astype(BF16)
            bji = _every_8th_row(bis, j, n2).astype(BF16)
            yr = (_dot(c2_ref[...], bjr) + _dot(s2_ref[...], bji)).astype(BF16)
            yi = (_dot(c2_ref[...], bji) - _dot(s2_ref[...], bjr)).astype(BF16)
            for g in range(C_GROUPS):
                cols = slice(g * C_GROUP_DIM, (g + 1) * C_GROUP_DIM)
                f = _dot(yr[:, cols], cc_ref[...]) + _dot(yi[:, cols], sc_ref[...])
                f_scr[q * n2:(q + 1) * n2, cols] = f.astype(BF16)
        y = _dot(f_scr[...], wout_ref[...])
        for q in range(FFT_PAIR):
            j = jp * FFT_PAIR + q
            x1 = _every_8th_row(x_scr, j, n2) + y[q * n2:(q + 1) * n2, :]
            x1t_ref[:, j * SUB:(j + 1) * SUB, :] = x1.reshape(n2, SUB, LANES)
            lg_ref[:, j * n2:(j + 1) * n2] = _router_logits(x1, mg_ref, wrh_ref, wrl_ref)


def _fft2(br, bi, x, nb, n1, wout, mg, wr):
    t = x.shape[0]
    n2 = FFT_N2
    c2, s2 = _dft_mats(n2, 1.0 / math.sqrt(n2))
    cc, sc = _dft_mats(C_GROUP_DIM, 1.0 / math.sqrt(C_GROUP_DIM))
    nkc = n1 // SUB
    rows = SUB * n2
    xv = x.reshape(nb * n2, n1, D_MODEL)
    full = lambda shape: pl.BlockSpec(shape, lambda b, k: (0,) * len(shape))
    b_slabs = _lane_slab_specs((rows,), lambda b, k: (b * nkc + k,))
    x_slabs = _lane_slab_specs((n2, SUB), lambda b, k: (b, k))
    x1, lg = pl.pallas_call(
        _fft2_kernel,
        grid=(nb, nkc),
        in_specs=b_slabs * 2 + x_slabs
        + [full((n2, n2)), full((n2, n2)),
           full((C_GROUP_DIM, C_GROUP_DIM)), full((C_GROUP_DIM, C_GROUP_DIM)),
           full((D_MODEL, D_MODEL)), full((1, D_MODEL)),
           full((N_EXPERTS, D_MODEL)), full((N_EXPERTS, D_MODEL))],
        out_specs=[pl.BlockSpec((n2, SUB * SUB, LANES), lambda b, k: (b, k, 0)),
                   pl.BlockSpec((N_EXPERTS, rows), lambda b, k: (0, b * nkc + k))],
        out_shape=[jax.ShapeDtypeStruct((nb * n2, n1 * SUB, LANES), F32),
                   jax.ShapeDtypeStruct((N_EXPERTS, t), F32)],
        scratch_shapes=[pltpu.VMEM((FFT_PAIR * n2, D_MODEL), BF16),
                        pltpu.VMEM((LANE_SLABS, rows, LANES), F32)],
        compiler_params=_cparams(("arbitrary", "arbitrary")),
        name="fft2",
    )(*([br] * LANE_SLABS), *([bi] * LANE_SLABS), *([xv] * LANE_SLABS),
      c2, s2, cc, sc, wout, mg, *wr)
    lg = lg.reshape(N_EXPERTS, nb, n1, n2).transpose(0, 1, 3, 2).reshape(N_EXPERTS, t)
    return x1.reshape(t * SUB, LANES), lg


def _moe(x1, logits_t, mg, p, layer, g_final=None):
    t = x1.shape[0] // SUB
    cap = CAPACITY_FACTOR * t // N_EXPERTS
    idx, gate, cb, pos = _route(logits_t, cap)
    ye = _ffn(idx, gate, x1, mg, p["wg"], p["wu"], p["wd"], layer)
    return _combine(cb, pos, x1, ye, g_final)


def _trunk(x3, p):
    nb, s, _ = x3.shape
    x = x3.reshape(nb * s, D_MODEL)
    row = lambda v: v.reshape(1, -1)
    outa, glu = _front(x, row(p["mix_norm"][0]), p["w_in"], row(p["ab_v_norm_g"][0]),
                       row(p["ab_v_norm_b"][0]), p["w_spatial"], p["b_spatial"])
    mg0, mg1 = row(p["moe_norm"][0]), row(p["moe_norm"][1])
    x1, lg = _back(x, outa, glu, p["ab_conv_w"][0], row(p["ab_conv_b"][0]),
                   row(p["ab_conv_norm_g"][0]), row(p["ab_conv_norm_b"][0]), p["ab_w_out"],
                   mg0, p["router"][0], s)
    x2 = _moe(x1, lg, mg0, p, 0)
    n1 = s // FFT_N2
    br, bi = _fft1(x2, row(p["mix_norm"][1]), nb, n1)
    x1, lg = _fft2(br, bi, x2, nb, n1, p["c_w_out"], mg1, p["router"][1])
    y = _moe(x1, lg, mg1, p, 1, row(p["final_norm"]))
    return y.reshape(nb, s, D_MODEL)


def kernel(x_prompt, x_sample, mix_norm, ab_w_in, ab_v_norm_g, ab_v_norm_b, ab_w_spatial,
           ab_b_spatial, ab_conv_w, ab_conv_b, ab_conv_norm_g, ab_conv_norm_b, ab_w_out,
           c_w_out, moe_norm, moe_router, moe_w_gate, moe_w_up, moe_w_down, final_norm):
    p = dict(
        mix_norm=mix_norm, moe_norm=moe_norm, final_norm=final_norm,
        w_in=ab_w_in[0].astype(BF16),
        ab_v_norm_g=ab_v_norm_g, ab_v_norm_b=ab_v_norm_b,
        w_spatial=ab_w_spatial[0].astype(BF16),
        b_spatial=jnp.repeat(jnp.transpose(ab_b_spatial[0]), A_HEAD_DIM, axis=1),
        ab_conv_w=ab_conv_w, ab_conv_b=ab_conv_b,
        ab_conv_norm_g=ab_conv_norm_g, ab_conv_norm_b=ab_conv_norm_b,
        ab_w_out=ab_w_out[0].astype(BF16),
        c_w_out=c_w_out[0].astype(BF16),
        router=[_split2(jnp.transpose(moe_router[l])) for l in range(moe_router.shape[0])],
        wg=moe_w_gate, wu=moe_w_up, wd=moe_w_down,
    )
    return (_trunk(x_prompt, p), _trunk(x_sample, p))
```

```python
import functools
import math

import numpy as np
import jax
import jax.numpy as jnp
from jax import lax
from jax.experimental import pallas as pl
from jax.experimental.pallas import tpu as pltpu

F32, BF16, I32 = jnp.float32, jnp.bfloat16, jnp.int32

D_MODEL = 1024
D_A = 512
D_B = 512
A_GROUPS = 4
A_HEAD_DIM = 128
CHUNK = 128
CONV_WIDTH = 31
CONV_PAD = 15
C_GROUPS = 4
C_GROUP_DIM = 256
N_EXPERTS = 16
CAPACITY_FACTOR = 2
RMS_EPS = 1e-6
LN_EPS = 1e-5

LANES = 128
SUB = 8
LANE_SLABS = D_MODEL // LANES
HALO = 16
TM = 512
RB = 1024
TB = 512
FFT_N2 = 128
VMEM_LIMIT = 48 * 1024 * 1024
BISECT_BITS_STEPS = 31
BISECT_VALUE_STEPS = 30


def _cparams(sem):
    return pltpu.CompilerParams(dimension_semantics=sem, vmem_limit_bytes=VMEM_LIMIT)


def _dot(a, b):
    return jnp.dot(a, b, preferred_element_type=F32)


def _dot_nt(a, b, precision=None):
    return lax.dot_general(a, b, (((1,), (1,)), ((), ())), precision=precision,
                           preferred_element_type=F32)


def _rms(x, g):
    return x * lax.rsqrt(jnp.mean(x * x, axis=-1, keepdims=True) + RMS_EPS) * g


def _ln(x, g, b):
    mu = jnp.mean(x, axis=-1, keepdims=True)
    xc = x - mu
    var = jnp.mean(xc * xc, axis=-1, keepdims=True)
    return xc * lax.rsqrt(var + LN_EPS) * g + b


def _gelu(x):
    return 0.5 * x * (1.0 + lax.erf(x * np.float32(math.sqrt(0.5))))


def _silu(x):
    return x * jax.nn.sigmoid(x)


def _front_kernel(x_ref, g_ref, win_ref, vg_ref, vb_ref, wsp_ref, bsp_ref,
                  outa_ref, glu_ref, zu_scr, vn_scr):
    xn = _rms(x_ref[...], g_ref[...]).astype(BF16)
    zu_scr[...] = _gelu(_dot(xn, win_ref[:, 0:D_A]))
    zv = _gelu(_dot(xn, win_ref[:, D_A:2 * D_A]))
    vn_scr[...] = _ln(zv, vg_ref[...], vb_ref[...]).astype(BF16)
    a_in = _dot(xn, win_ref[:, 2 * D_A:2 * D_A + D_B])
    gate = _dot(xn, win_ref[:, 2 * D_A + D_B:])
    glu_ref[...] = a_in * jax.nn.sigmoid(gate)
    for c in range(TM // CHUNK):
        rows = slice(c * CHUNK, (c + 1) * CHUNK)
        for h in range(A_GROUPS):
            cols = slice(h * A_HEAD_DIM, (h + 1) * A_HEAD_DIM)
            mixed = _dot(wsp_ref[h], vn_scr[rows, cols]) + bsp_ref[:, cols]
            outa_ref[rows, cols] = (zu_scr[rows, cols] * mixed).astype(BF16)


def _front(x, g, w_in, vg, vb, wsp, bsp):
    t = x.shape[0]
    full = lambda shape: pl.BlockSpec(shape, lambda i: (0,) * len(shape))
    return pl.pallas_call(
        _front_kernel,
        grid=(t // TM,),
        in_specs=[
            pl.BlockSpec((TM, D_MODEL), lambda i: (i, 0)),
            full((1, D_MODEL)),
            full((D_MODEL, 2 * D_A + 2 * D_B)),
            full((1, D_A)), full((1, D_A)),
            full((A_GROUPS, CHUNK, CHUNK)),
            full((CHUNK, D_A)),
        ],
        out_specs=[pl.BlockSpec((TM, D_A), lambda i: (i, 0)),
                   pl.BlockSpec((TM, D_B), lambda i: (i, 0))],
        out_shape=[jax.ShapeDtypeStruct((t, D_A), BF16),
                   jax.ShapeDtypeStruct((t, D_B), F32)],
        scratch_shapes=[pltpu.VMEM((TM, D_A), F32), pltpu.VMEM((TM, D_A), BF16)],
        compiler_params=_cparams(("arbitrary",)),
        name="front",
    )(x, g, w_in, vg, vb, wsp, bsp)


CONV_ROWS = 64
SUBLANES = 8
SHIFT_ROWS = TM + (HALO + CONV_PAD) // SUBLANES * SUBLANES


def _split2(a):
    hi = a.astype(BF16)
    return hi, (a - hi.astype(F32)).astype(BF16)


def _store_token_tiles(tiles_ref, x):
    rows = x.shape[0]
    for c in range(LANE_SLABS):
        tiles_ref[pl.ds(c, rows, stride=SUB), :] = x[:, c * LANES:(c + 1) * LANES]


def _load_token_tiles(tiles_ref, rows, lead=()):
    parts = [tiles_ref[lead + (pl.ds(c, rows, stride=SUB), slice(None))] for c in range(LANE_SLABS)]
    return jnp.concatenate(parts, axis=1)


def _router_logits(x1, mg_ref, wrh_ref, wrl_ref):
    xh, xl = _split2(_rms(x1, mg_ref[...]))
    return _dot_nt(wrh_ref[...], xh) + (_dot_nt(wrh_ref[...], xl) + _dot_nt(wrl_ref[...], xh))


def _back_kernel(x_ref, outa_ref, glu_ref, prev_ref, next_ref, cw_ref, cb_ref, ng_ref, nb_ref,
                 wout_ref, mg_ref, wrh_ref, wrl_ref, x1t_ref, lg_ref, ext_scr, sh_scr, ob_scr,
                 *, blocks_per_seq):
    i = pl.program_id(0)
    pos = lax.rem(i, blocks_per_seq)
    ext_scr[0:HALO, :] = jnp.where(pos == 0, 0.0, prev_ref[...])
    ext_scr[HALO:HALO + TM, :] = glu_ref[...]
    ext_scr[HALO + TM:, :] = jnp.where(pos == blocks_per_seq - 1, 0.0, next_ref[...])
    for b in range(SUBLANES):
        sh_scr[b] = ext_scr[b:b + SHIFT_ROWS, :]
    for r in range(TM // CONV_ROWS):
        acc = jnp.zeros((CONV_ROWS, D_B), F32) + cb_ref[...]
        for k in range(CONV_WIDTH):
            a, b = divmod(HALO - CONV_PAD + k, SUBLANES)
            row0 = r * CONV_ROWS + a * SUBLANES
            acc = acc + cw_ref[k:k + 1, :] * sh_scr[b, row0:row0 + CONV_ROWS, :]
        ob = _silu(_ln(acc, ng_ref[...], nb_ref[...]))
        ob_scr[r * CONV_ROWS:(r + 1) * CONV_ROWS, :] = ob.astype(BF16)
    y = _dot(outa_ref[...], wout_ref[0:D_A, :]) + _dot(ob_scr[...], wout_ref[D_A:, :])
    x1 = x_ref[...] + y
    _store_token_tiles(x1t_ref, x1)
    lg_ref[...] = _router_logits(x1, mg_ref, wrh_ref, wrl_ref)


def _back(x, outa, glu, cw, cb, ng, nb_, wout, mg, wr, seq_len):
    t = x.shape[0]
    full = lambda shape: pl.BlockSpec(shape, lambda i: (0,) * len(shape))
    hb = TM // HALO
    return pl.pallas_call(
        functools.partial(_back_kernel, blocks_per_seq=seq_len // TM),
        grid=(t // TM,),
        in_specs=[
            pl.BlockSpec((TM, D_MODEL), lambda i: (i, 0)),
            pl.BlockSpec((TM, D_A), lambda i: (i, 0)),
            pl.BlockSpec((TM, D_B), lambda i: (i, 0)),
            pl.BlockSpec((HALO, D_B), lambda i: (jnp.maximum(i * hb - 1, 0), 0)),
            pl.BlockSpec((HALO, D_B), lambda i: (jnp.minimum((i + 1) * hb, t // HALO - 1), 0)),
            full((CONV_WIDTH, D_B)), full((1, D_B)), full((1, D_B)), full((1, D_B)),
            full((D_A + D_B, D_MODEL)),
            full((1, D_MODEL)),
            full((N_EXPERTS, D_MODEL)), full((N_EXPERTS, D_MODEL)),
        ],
        out_specs=[pl.BlockSpec((TM * SUB, LANES), lambda i: (i, 0)),
                   pl.BlockSpec((N_EXPERTS, TM), lambda i: (0, i))],
        out_shape=[jax.ShapeDtypeStruct((t * SUB, LANES), F32),
                   jax.ShapeDtypeStruct((N_EXPERTS, t), F32)],
        scratch_shapes=[pltpu.VMEM((TM + 2 * HALO, D_B), F32),
                        pltpu.VMEM((SUBLANES, SHIFT_ROWS, D_B), F32),
                        pltpu.VMEM((TM, D_B), BF16)],
        compiler_params=_cparams(("arbitrary",)),
        name="back",
    )(x, outa, glu, glu, glu, cw, cb, ng, nb_, wout, mg, *wr)


def _split3(a):
    a1 = a.astype(BF16)
    r1 = a - a1.astype(F32)
    a2 = r1.astype(BF16)
    a3 = (r1 - a2.astype(F32)).astype(BF16)
    return a1, a2, a3


def _route_kernel(lg_ref, idx_ref, gate_ref, cb_ref, pos_ref, lm_scr, w_scr, a1_scr, a2_scr, a3_scr,
                  cnt_scr, cbr_scr, *, nbk, cap):
    ne = N_EXPERTS
    lg = lg_ref[...]
    ex = jnp.exp(lg - jnp.max(lg, axis=0, keepdims=True))
    aff = ex / jnp.sum(ex, axis=0, keepdims=True)

    def count(m):
        s = jnp.sum(m.astype(F32), axis=1, keepdims=True)
        return jnp.sum(s, axis=2, keepdims=True)

    def update(mid, lo, hi):
        ge = count(aff >= mid) >= cap
        return jnp.where(ge, mid, lo), jnp.where(ge, hi, mid)

    def bits_body(_, c):
        lo, hi = c
        lo_i = lax.bitcast_convert_type(lo, I32)
        hi_i = lax.bitcast_convert_type(hi, I32)
        mid = lax.bitcast_convert_type(lo_i + lax.shift_right_logical(hi_i - lo_i, 1), F32)
        return update(mid, lo, hi)

    def value_body(_, c):
        lo, hi = c
        return update(lo + (hi - lo) * 0.5, lo, hi)

    lo0 = jnp.zeros((ne, 1, 1), F32)
    hi0 = jnp.full((ne, 1, 1), jnp.inf, F32)
    c = lax.fori_loop(0, BISECT_BITS_STEPS, bits_body, (lo0, hi0))
    lo, hi = lax.fori_loop(0, BISECT_VALUE_STEPS, value_body, c)

    li = lax.broadcasted_iota(I32, (LANES, LANES), 0)
    lj = lax.broadcasted_iota(I32, (LANES, LANES), 1)
    ut = (li <= lj).astype(BF16)
    ones = jnp.ones((LANES, LANES), BF16)
    bi = lax.broadcasted_iota(I32, (nbk, nbk), 0)
    bj = lax.broadcasted_iota(I32, (nbk, nbk), 1)
    slt = (bj < bi).astype(BF16)
    utb = (bi <= bj).astype(BF16)
    ones8 = jnp.ones((8, LANES), BF16)

    def block_cum(m):
        m2 = m.astype(BF16).reshape(ne * nbk, LANES)
        loc = _dot(m2, ut).reshape(ne, nbk, LANES)
        tot = _dot(m2, ones).astype(BF16).reshape(ne, nbk, LANES)
        off = jnp.stack([_dot(slt, tot[e]) for e in range(ne)])
        return loc, tot, off

    gt = aff >= hi
    eq = (aff >= lo) & (aff < hi)
    need = np.float32(cap) - count(gt)
    eloc, _, eoff = block_cum(eq)
    mask = gt | (eq & (eloc + eoff <= need))

    mloc, mtot, moff = block_cum(mask)
    pos_ref[...] = jnp.where(mask, mloc + moff - 1.0, -1.0).astype(I32)
    lm_scr[...] = mloc.astype(BF16)
    w_scr[...] = mtot
    a1, a2, a3 = _split3(aff)
    a1_scr[...] = a1
    a2_scr[...] = a2
    a3_scr[...] = a3
    maskb = mask.astype(BF16)
    for e in range(ne):
        cnt = _dot_nt(ones8, maskb[e])
        cnt_scr[e] = cnt
        cbr_scr[e] = _dot(cnt.astype(BF16), utb)
    cb_ref[...] = cbr_scr[:, 0, :].astype(I32)

    s_col = lax.broadcasted_iota(I32, (cap, 1), 0).astype(F32)
    lane_b = lax.broadcasted_iota(I32, (cap, nbk), 1).astype(F32)
    lane_l = lax.broadcasted_iota(I32, (cap, LANES), 1).astype(F32)
    ones_b = jnp.ones((nbk, LANES), BF16)
    ones8b = jnp.ones((8, nbk), BF16)

    def per_expert(e, _):
        cbrow = cbr_scr[e][0:1, :]
        le = (cbrow <= s_col).astype(BF16)
        blk_c = _dot(le, ones_b)
        start_c = _dot(le, w_scr[e])
        oh = (lane_b == blk_c[:, 0:nbk]).astype(BF16)
        g_loc = _dot(oh, lm_scr[e])
        le2 = (g_loc <= s_col - start_c).astype(BF16)
        within_c = _dot(le2, ones)
        blk_r = _dot_nt(ones8b, le)
        within_r = _dot_nt(ones8, le2)
        idx_ref[pl.ds(e, 1), :] = (blk_r * LANES + within_r)[0:1, :].astype(I32)
        pick = lane_l == within_c
        gsum = None
        for a_scr in (a1_scr, a2_scr, a3_scr):
            sel = jnp.where(pick, _dot(oh, a_scr[e]), 0.0).astype(BF16)
            part = _dot_nt(ones8, sel)
            gsum = part if gsum is None else gsum + part
        gate_ref[pl.ds(e, 1), :] = gsum[0:1, :]
        return 0

    lax.fori_loop(0, ne, per_expert, 0)


def _route(logits_t, cap):
    ne, t = logits_t.shape
    nbk = t // LANES
    lg3 = logits_t.reshape(ne, nbk, LANES)
    full = lambda shape: pl.BlockSpec(shape, lambda i: (0,) * len(shape))
    tab = lambda dt: pltpu.VMEM((ne, nbk, LANES), dt)
    return pl.pallas_call(
        functools.partial(_route_kernel, nbk=nbk, cap=cap),
        grid=(1,),
        in_specs=[full((ne, nbk, LANES))],
        out_specs=[full((ne, cap)), full((ne, cap)), full((ne, nbk)), full((ne, nbk, LANES))],
        out_shape=[jax.ShapeDtypeStruct((ne, cap), I32),
                   jax.ShapeDtypeStruct((ne, cap), F32),
                   jax.ShapeDtypeStruct((ne, nbk), I32),
                   jax.ShapeDtypeStruct((ne, nbk, LANES), I32)],
        scratch_shapes=[tab(BF16), tab(BF16), tab(BF16), tab(BF16), tab(BF16),
                        pltpu.VMEM((ne, 8, nbk), F32), pltpu.VMEM((ne, 8, nbk), F32)],
        compiler_params=_cparams(("arbitrary",)),
        name="route",
    )(lg3)


def _ffn_kernel(idx_ref, idxn_ref, gate_ref, x_hbm, x2d_hbm, mg_ref, wg_hbm, wu_hbm, wd_hbm, ye_ref,
                xbuf, wst, wbf, sem, wsem, *, layer, ne, nsteps, rb):
    e = pl.program_id(0)
    step = e * pl.num_programs(1) + pl.program_id(1)
    slot = lax.rem(step, 2)

    def weight_copies(ex):
        return [pltpu.make_async_copy(w_hbm.at[layer, ex], wst.at[m], wsem.at[m])
                for m, w_hbm in enumerate((wg_hbm, wu_hbm, wd_hbm))]

    def issue(ids_ref, s):
        for r in range(rb):
            pltpu.make_async_copy(x_hbm.at[ids_ref[0, 0, r]], xbuf.at[s, pl.ds(r * SUB, SUB)],
                                  sem.at[s]).start()

    def wait(s):
        pltpu.make_async_copy(x2d_hbm.at[pl.ds(0, rb * SUB)], xbuf.at[s], sem.at[s]).wait()

    @pl.when(step == 0)
    def _():
        for cp in weight_copies(0):
            cp.start()
        issue(idx_ref, 0)

    @pl.when(pl.program_id(1) == 0)
    def _():
        for m, cp in enumerate(weight_copies(e)):
            cp.wait()
            wbf[m] = wst[m].astype(BF16)

        @pl.when(e + 1 < ne)
        def _():
            for cp in weight_copies(e + 1):
                cp.start()

    issue(idxn_ref, 1 - slot)
    wait(slot)

    xe = _rms(_load_token_tiles(xbuf, rb, (slot,)), mg_ref[...]).astype(BF16)
    hid = (_silu(_dot(xe, wbf[0])) * _dot(xe, wbf[1])).astype(BF16)
    ye = _dot(hid, wbf[2])
    ri = lax.broadcasted_iota(I32, (LANES, LANES), 0)
    ci = lax.broadcasted_iota(I32, (LANES, LANES), 1)
    gate = gate_ref[0]
    gcol = jnp.concatenate(
        [jnp.sum(jnp.where(ri == ci, gate[:, q:q + LANES], 0.0), axis=1, keepdims=True)
         for q in range(0, rb, LANES)], axis=0)
    ye_ref[...] = (ye * gcol).astype(BF16)

    @pl.when(step == nsteps - 1)
    def _():
        wait(1 - slot)


def _ffn(idx, gate, x1t, mg, wg, wu, wd, layer):
    ne, cap = idx.shape
    t = x1t.shape[0] // SUB
    rb = min(RB, cap)
    nblk = cap // rb
    nsteps = ne * nblk
    idx3 = idx.reshape(nsteps, 1, rb)
    gate3 = gate.reshape(nsteps, 1, rb)
    hbm = pl.BlockSpec(memory_space=pl.ANY)
    return pl.pallas_call(
        functools.partial(_ffn_kernel, layer=layer, ne=ne, nsteps=nsteps, rb=rb),
        grid=(ne, nblk),
        in_specs=[
            pl.BlockSpec((1, 1, rb), lambda e, j: (e * nblk + j, 0, 0), memory_space=pltpu.SMEM),
            pl.BlockSpec((1, 1, rb), lambda e, j: (jnp.minimum(e * nblk + j + 1, nsteps - 1), 0, 0),
                         memory_space=pltpu.SMEM),
            pl.BlockSpec((1, 1, rb), lambda e, j: (e * nblk + j, 0, 0)),
            hbm, hbm,
            pl.BlockSpec((1, D_MODEL), lambda e, j: (0, 0)),
            hbm, hbm, hbm,
        ],
        out_specs=pl.BlockSpec((rb, D_MODEL), lambda e, j: (e * nblk + j, 0)),
        out_shape=jax.ShapeDtypeStruct((ne * cap, D_MODEL), BF16),
        scratch_shapes=[pltpu.VMEM((2, rb * SUB, LANES), F32),
                        pltpu.VMEM((3, D_MODEL, D_MODEL), F32),
                        pltpu.VMEM((3, D_MODEL, D_MODEL), BF16),
                        pltpu.SemaphoreType.DMA((2,)), pltpu.SemaphoreType.DMA((3,))],
        compiler_params=_cparams(("arbitrary", "arbitrary")),
        name="ffn",
    )(idx3, idx3, gate3, x1t.reshape(t, SUB, LANES), x1t, mg, wg, wu, wd)


CHUNK_ROWS = 16
MAIN_CHUNKS = 96
MAIN_ROWS = CHUNK_ROWS * MAIN_CHUNKS
MAX_CHUNKS = -(-(N_EXPERTS * TB // CHUNK_ROWS + 2 * N_EXPERTS) // MAIN_CHUNKS) * MAIN_CHUNKS
NO_SLOT = 1 << 30


def _combine_kernel(cb_ref, x1t_ref, pos_ref, ye_hbm, *rest, nbk, cap, ntb, final):
    if final:
        g_ref, out_ref, ybuf, acc_scr, sem, ie_scr, ir_scr, n_scr = rest
    else:
        out_ref, ybuf, acc_scr, sem, ie_scr, ir_scr, n_scr = rest
    tb = pl.program_id(0)
    s = lax.rem(tb, 2)
    bpt = TB // LANES
    shift = CHUNK_ROWS.bit_length() - 1

    def build(tbx, par):
        base = par * MAX_CHUNKS
        n = jnp.int32(0)
        for e in range(N_EXPERTS):
            b0 = e * nbk + tbx * bpt
            lo = jnp.where(tbx == 0, 0, cb_ref[jnp.maximum(b0 - 1, 0)])
            hi = cb_ref[b0 + bpt - 1]
            start = lax.shift_left(lax.shift_right_logical(lo, shift), shift)
            nch = jnp.where(hi > lo,
                            lax.shift_right_logical(hi - start + (CHUNK_ROWS - 1), shift), 0)

            def push(c, k, e=e, start=start):
                ie_scr[base + k] = e
                ir_scr[base + k] = start + c * CHUNK_ROWS
                return k + 1

            n = lax.fori_loop(0, nch, push, n)
        nsg = jnp.maximum(lax.div(n + (MAIN_CHUNKS - 1), MAIN_CHUNKS), 1)

        def pad(k, _):
            ie_scr[base + k] = 0
            ir_scr[base + k] = NO_SLOT
            return 0

        lax.fori_loop(n, nsg * MAIN_CHUNKS, pad, 0)
        n_scr[par] = n
        return n

    def chunk_copy(par, k, k0):
        base = par * MAX_CHUNKS
        row0 = pl.multiple_of(ie_scr[base + k] * cap + ir_scr[base + k], CHUNK_ROWS)
        dst0 = pl.multiple_of((k - k0) * CHUNK_ROWS, CHUNK_ROWS)
        return pltpu.make_async_copy(ye_hbm.at[pl.ds(row0, CHUNK_ROWS)],
                                     ybuf.at[par, pl.ds(dst0, CHUNK_ROWS)], sem.at[par])

    def issue(par, k0, k1):
        def body(k, _):
            chunk_copy(par, k, k0).start()
            return 0
        lax.fori_loop(k0, k1, body, 0)

    def drain(par, k0, k1):
        def body(k, _):
            chunk_copy(par, k, k0).wait()
            return 0
        lax.fori_loop(k0, k1, body, 0)

    @pl.when(tb == 0)
    def _():
        ybuf[...] = jnp.zeros_like(ybuf)
        n0 = build(0, 0)
        issue(0, 0, jnp.minimum(n0, MAIN_CHUNKS))

    @pl.when(tb + 1 < ntb)
    def _():
        n1 = build(tb + 1, 1 - s)
        issue(1 - s, 0, jnp.minimum(n1, MAIN_CHUNKS))

    jrow = lax.broadcasted_iota(I32, (CHUNK_ROWS, TB), 0)

    def scatter(sg):
        base = s * MAX_CHUNKS + sg * MAIN_CHUNKS
        pieces = []
        for c in range(MAIN_CHUNKS):
            prow = pos_ref[pl.ds(ie_scr[base + c], 1), :]
            pieces.append((prow == jrow + ir_scr[base + c]).astype(BF16))
        onehot_t = jnp.concatenate(pieces, axis=0)
        return lax.dot_general(onehot_t, ybuf[s], (((0,), (0,)), ((), ())),
                               preferred_element_type=F32)

    def finish(x2):
        return _rms(x2, g_ref[...]) if final else x2

    n = n_scr[s]
    drain(s, 0, jnp.minimum(n, MAIN_CHUNKS))
    nsg = lax.div(n + (MAIN_CHUNKS - 1), MAIN_CHUNKS)

    @pl.when(nsg <= 1)
    def _():
        out_ref[...] = finish(_load_token_tiles(x1t_ref, TB) + scatter(0))

    @pl.when(nsg > 1)
    def _():
        acc_scr[...] = _load_token_tiles(x1t_ref, TB) + scatter(0)

        def extra(sg, _):
            k0 = sg * MAIN_CHUNKS
            k1 = jnp.minimum(n, k0 + MAIN_CHUNKS)
            issue(s, k0, k1)
            drain(s, k0, k1)
            acc_scr[...] += scatter(sg)
            return 0

        lax.fori_loop(1, nsg, extra, 0)
        out_ref[...] = finish(acc_scr[...])


def _combine(cb, pos, x1t, ye, g_final=None):
    t = x1t.shape[0] // SUB
    ne, nbk = cb.shape
    cap = ye.shape[0] // ne
    final = g_final is not None
    blk = pl.BlockSpec((TB, D_MODEL), lambda i, cb: (i, 0))
    in_specs = [pl.BlockSpec((TB * SUB, LANES), lambda i, cb: (i, 0)),
                pl.BlockSpec((ne, TB), lambda i, cb: (0, i)), pl.BlockSpec(memory_space=pl.ANY)]
    args = [cb.reshape(-1), x1t, pos.reshape(ne, t), ye]
    if final:
        in_specs.append(pl.BlockSpec((1, D_MODEL), lambda i, cb: (0, 0)))
        args.append(g_final)
    grid_spec = pltpu.PrefetchScalarGridSpec(
        num_scalar_prefetch=1,
        grid=(t // TB,),
        in_specs=in_specs,
        out_specs=blk,
        scratch_shapes=[pltpu.VMEM((2, MAIN_ROWS, D_MODEL), BF16),
                        pltpu.VMEM((TB, D_MODEL), F32),
                        pltpu.SemaphoreType.DMA((2,)),
                        pltpu.SMEM((2 * MAX_CHUNKS,), I32), pltpu.SMEM((2 * MAX_CHUNKS,), I32),
                        pltpu.SMEM((2,), I32)],
    )
    return pl.pallas_call(
        functools.partial(_combine_kernel, nbk=nbk, cap=cap, ntb=t // TB, final=final),
        grid_spec=grid_spec,
        out_shape=jax.ShapeDtypeStruct((t, D_MODEL), F32),
        compiler_params=_cparams(("arbitrary",)),
        name="combine_final" if final else "combine",
    )(*args)


def _dft_mats(n, scale):
    k = np.arange(n)
    ang = 2.0 * np.pi * ((k[:, None] * k[None, :]) % n) / n
    return (jnp.asarray(np.cos(ang) * scale, F32).astype(BF16),
            jnp.asarray(np.sin(ang) * scale, F32).astype(BF16))


def _lane_slab_specs(block, index_map):
    return [pl.BlockSpec(block + (LANES,), functools.partial(lambda *i, c: index_map(*i) + (c,), c=c))
            for c in range(LANE_SLABS)]


def _every_8th_row(slabs, j, rows):
    parts = [slabs[c, pl.ds(j, rows, stride=SUB), :] if not isinstance(slabs, (list, tuple))
             else slabs[c][pl.ds(j, rows, stride=SUB), :] for c in range(LANE_SLABS)]
    return jnp.concatenate(parts, axis=1)


def _fft1_kernel(*refs):
    xs = refs[:LANE_SLABS]
    g_ref, cs1_ref, twc_ref, tws_ref, br_ref, bi_ref, x_scr = refs[LANE_SLABS:]
    n1 = cs1_ref.shape[1]
    for c in range(LANE_SLABS):
        x_scr[c] = xs[c][...].reshape(n1 * SUB, LANES)
    for j in range(SUB):
        xj = _rms(_every_8th_row(x_scr, j, n1), g_ref[...]).astype(BF16)
        a = _dot(cs1_ref[...], xj)
        ar, ai = a[:n1], a[n1:]
        c = twc_ref[0, :, j:j + 1]
        s = tws_ref[0, :, j:j + 1]
        br_ref[:, j] = (ar * c + ai * s).reshape(n1 // SUB, SUB, D_MODEL)
        bi_ref[:, j] = (ai * c - ar * s).reshape(n1 // SUB, SUB, D_MODEL)


def _fft1(x, g, nb, n1):
    t = x.shape[0]
    n = n1 * FFT_N2
    c1, s1 = _dft_mats(n1, 1.0 / math.sqrt(n1))
    cs1 = jnp.concatenate([c1, -s1], axis=0)
    k1 = np.arange(n1)[:, None]
    n2 = np.arange(FFT_N2)[None, :]
    ang = 2.0 * np.pi * ((k1 * n2) % n) / n
    nj = FFT_N2 // SUB
    nkc = n1 // SUB
    to_tab = lambda a: jnp.asarray(a.reshape(n1, nj, SUB).transpose(1, 0, 2), F32)
    twc, tws = to_tab(np.cos(ang)), to_tab(np.sin(ang))
    xv = x.reshape(nb * n1, FFT_N2, D_MODEL)
    out_blk = pl.BlockSpec((nkc, SUB, SUB, D_MODEL), lambda b, j: (b, j, 0, 0))
    out_shape = jax.ShapeDtypeStruct((nb * nkc, FFT_N2, SUB, D_MODEL), F32)
    tws_spec = pl.BlockSpec((1, n1, SUB), lambda b, j: (j, 0, 0))
    mat = pl.BlockSpec((2 * n1, n1), lambda b, j: (0, 0))
    br, bi = pl.pallas_call(
        _fft1_kernel,
        grid=(nb, nj),
        in_specs=_lane_slab_specs((n1, SUB), lambda b, j: (b, j))
        + [pl.BlockSpec((1, D_MODEL), lambda b, j: (0, 0)), mat, tws_spec, tws_spec],
        out_specs=[out_blk, out_blk],
        out_shape=[out_shape, out_shape],
        scratch_shapes=[pltpu.VMEM((LANE_SLABS, n1 * SUB, LANES), F32)],
        compiler_params=_cparams(("arbitrary", "arbitrary")),
        name="fft1",
    )(*([xv] * LANE_SLABS), g, cs1, twc, tws)
    return br.reshape(t, D_MODEL), bi.reshape(t, D_MODEL)


FFT_PAIR = 2


def _fft2_kernel(*refs):
    brs, bis, xs = (refs[i * LANE_SLABS:(i + 1) * LANE_SLABS] for i in range(3))
    (w2r_ref, w2i_ref, cc_ref, sc_ref, wout_ref, mg_ref, wrh_ref, wrl_ref,
     x1t_ref, lg_ref, f_scr, x_scr) = refs[3 * LANE_SLABS:]
    n2 = FFT_N2
    for c in range(LANE_SLABS):
        x_scr[c] = xs[c][...].reshape(n2 * SUB, LANES)
    for jp in range(SUB // FFT_PAIR):
        for q in range(FFT_PAIR):
            j = jp * FFT_PAIR + q
            bj = jnp.concatenate([_every_8th_row(brs, j, n2).astype(BF16),
                                  _every_8th_row(bis, j, n2).astype(BF16)], axis=0)
            yr = _dot(w2r_ref[...], bj).astype(BF16)
            yi = _dot(w2i_ref[...], bj).astype(BF16)
            for g in range(C_GROUPS):
                cols = slice(g * C_GROUP_DIM, (g + 1) * C_GROUP_DIM)
                f = _dot(yr[:, cols], cc_ref[...]) + _dot(yi[:, cols], sc_ref[...])
                f_scr[q * n2:(q + 1) * n2, cols] = f.astype(BF16)
        y = _dot(f_scr[...], wout_ref[...])
        for q in range(FFT_PAIR):
            j = jp * FFT_PAIR + q
            x1 = _every_8th_row(x_scr, j, n2) + y[q * n2:(q + 1) * n2, :]
            x1t_ref[:, j * SUB:(j + 1) * SUB, :] = x1.reshape(n2, SUB, LANES)
            lg_ref[:, j * n2:(j + 1) * n2] = _router_logits(x1, mg_ref, wrh_ref, wrl_ref)


def _fft2(br, bi, x, nb, n1, wout, mg, wr):
    t = x.shape[0]
    n2 = FFT_N2
    c2, s2 = _dft_mats(n2, 1.0 / math.sqrt(n2))
    w2r = jnp.concatenate([c2, s2], axis=1)
    w2i = jnp.concatenate([-s2, c2], axis=1)
    cc, sc = _dft_mats(C_GROUP_DIM, 1.0 / math.sqrt(C_GROUP_DIM))
    nkc = n1 // SUB
    rows = SUB * n2
    xv = x.reshape(nb * n2, n1, D_MODEL)
    full = lambda shape: pl.BlockSpec(shape, lambda b, k: (0,) * len(shape))
    b_slabs = _lane_slab_specs((rows,), lambda b, k: (b * nkc + k,))
    x_slabs = _lane_slab_specs((n2, SUB), lambda b, k: (b, k))
    x1, lg = pl.pallas_call(
        _fft2_kernel,
        grid=(nb, nkc),
        in_specs=b_slabs * 2 + x_slabs
        + [full((n2, 2 * n2)), full((n2, 2 * n2)),
           full((C_GROUP_DIM, C_GROUP_DIM)), full((C_GROUP_DIM, C_GROUP_DIM)),
           full((D_MODEL, D_MODEL)), full((1, D_MODEL)),
           full((N_EXPERTS, D_MODEL)), full((N_EXPERTS, D_MODEL))],
        out_specs=[pl.BlockSpec((n2, SUB * SUB, LANES), lambda b, k: (b, k, 0)),
                   pl.BlockSpec((N_EXPERTS, rows), lambda b, k: (0, b * nkc + k))],
        out_shape=[jax.ShapeDtypeStruct((nb * n2, n1 * SUB, LANES), F32),
                   jax.ShapeDtypeStruct((N_EXPERTS, t), F32)],
        scratch_shapes=[pltpu.VMEM((FFT_PAIR * n2, D_MODEL), BF16),
                        pltpu.VMEM((LANE_SLABS, rows, LANES), F32)],
        compiler_params=_cparams(("arbitrary", "arbitrary")),
        name="fft2",
    )(*([br] * LANE_SLABS), *([bi] * LANE_SLABS), *([xv] * LANE_SLABS),
      w2r, w2i, cc, sc, wout, mg, *wr)
    lg = lg.reshape(N_EXPERTS, nb, n1, n2).transpose(0, 1, 3, 2).reshape(N_EXPERTS, t)
    return x1.reshape(t * SUB, LANES), lg


def _moe(x1, logits_t, mg, p, layer, g_final=None):
    t = x1.shape[0] // SUB
    cap = CAPACITY_FACTOR * t // N_EXPERTS
    idx, gate, cb, pos = _route(logits_t, cap)
    ye = _ffn(idx, gate, x1, mg, p["wg"], p["wu"], p["wd"], layer)
    return _combine(cb, pos, x1, ye, g_final)


def _trunk(x3, p):
    nb, s, _ = x3.shape
    x = x3.reshape(nb * s, D_MODEL)
    row = lambda v: v.reshape(1, -1)
    outa, glu = _front(x, row(p["mix_norm"][0]), p["w_in"], row(p["ab_v_norm_g"][0]),
                       row(p["ab_v_norm_b"][0]), p["w_spatial"], p["b_spatial"])
    mg0, mg1 = row(p["moe_norm"][0]), row(p["moe_norm"][1])
    x1, lg = _back(x, outa, glu, p["ab_conv_w"][0], row(p["ab_conv_b"][0]),
                   row(p["ab_conv_norm_g"][0]), row(p["ab_conv_norm_b"][0]), p["ab_w_out"],
                   mg0, p["router"][0], s)
    x2 = _moe(x1, lg, mg0, p, 0)
    n1 = s // FFT_N2
    br, bi = _fft1(x2, row(p["mix_norm"][1]), nb, n1)
    x1, lg = _fft2(br, bi, x2, nb, n1, p["c_w_out"], mg1, p["router"][1])
    y = _moe(x1, lg, mg1, p, 1, row(p["final_norm"]))
    return y.reshape(nb, s, D_MODEL)


def kernel(x_prompt, x_sample, mix_norm, ab_w_in, ab_v_norm_g, ab_v_norm_b, ab_w_spatial,
           ab_b_spatial, ab_conv_w, ab_conv_b, ab_conv_norm_g, ab_conv_norm_b, ab_w_out,
           c_w_out, moe_norm, moe_router, moe_w_gate, moe_w_up, moe_w_down, final_norm):
    p = dict(
        mix_norm=mix_norm, moe_norm=moe_norm, final_norm=final_norm,
        w_in=ab_w_in[0].astype(BF16),
        ab_v_norm_g=ab_v_norm_g, ab_v_norm_b=ab_v_norm_b,
        w_spatial=ab_w_spatial[0].astype(BF16),
        b_spatial=jnp.repeat(jnp.transpose(ab_b_spatial[0]), A_HEAD_DIM, axis=1),
        ab_conv_w=ab_conv_w, ab_conv_b=ab_conv_b,
        ab_conv_norm_g=ab_conv_norm_g, ab_conv_norm_b=ab_conv_norm_b,
        ab_w_out=ab_w_out[0].astype(BF16),
        c_w_out=c_w_out[0].astype(BF16),
        router=[_split2(jnp.transpose(moe_router[l])) for l in range(moe_router.shape[0])],
        wg=moe_w_gate, wu=moe_w_up, wd=moe_w_down,
    )
    return (_trunk(x_prompt, p), _trunk(x_sample, p))
```

```python
import functools
import math

import numpy as np
import jax
import jax.numpy as jnp
from jax import lax
from jax.experimental import pallas as pl
from jax.experimental.pallas import tpu as pltpu

F32, BF16, I32 = jnp.float32, jnp.bfloat16, jnp.int32

D_MODEL = 1024
D_A = 512
D_B = 512
A_GROUPS = 4
A_HEAD_DIM = 128
CHUNK = 128
CONV_WIDTH = 31
CONV_PAD = 15
C_GROUPS = 4
C_GROUP_DIM = 256
N_EXPERTS = 16
CAPACITY_FACTOR = 2
RMS_EPS = 1e-6
LN_EPS = 1e-5

LANES = 128
SUB = 8
LANE_SLABS = D_MODEL // LANES
HALO = 16
TM = 512
RB = 512
TB = 512
FFT_N2 = 128
VMEM_LIMIT = 48 * 1024 * 1024
BISECT_BITS_STEPS = 31
BISECT_VALUE_STEPS = 30


def _cparams(sem):
    return pltpu.CompilerParams(dimension_semantics=sem, vmem_limit_bytes=VMEM_LIMIT)


def _dot(a, b):
    return jnp.dot(a, b, preferred_element_type=F32)


def _dot_nt(a, b, precision=None):
    return lax.dot_general(a, b, (((1,), (1,)), ((), ())), precision=precision,
                           preferred_element_type=F32)


def _rms(x, g):
    return x * lax.rsqrt(jnp.mean(x * x, axis=-1, keepdims=True) + RMS_EPS) * g


def _ln(x, g, b):
    mu = jnp.mean(x, axis=-1, keepdims=True)
    xc = x - mu
    var = jnp.mean(xc * xc, axis=-1, keepdims=True)
    return xc * lax.rsqrt(var + LN_EPS) * g + b


def _gelu(x):
    return 0.5 * x * (1.0 + lax.erf(x * np.float32(math.sqrt(0.5))))


def _silu(x):
    return x * jax.nn.sigmoid(x)


def _front_kernel(x_ref, g_ref, win_ref, vg_ref, vb_ref, wsp_ref, bsp_ref,
                  outa_ref, glu_ref, zu_scr, vn_scr):
    xn = _rms(x_ref[...], g_ref[...]).astype(BF16)
    zu_scr[...] = _gelu(_dot(xn, win_ref[:, 0:D_A]))
    zv = _gelu(_dot(xn, win_ref[:, D_A:2 * D_A]))
    vn_scr[...] = _ln(zv, vg_ref[...], vb_ref[...]).astype(BF16)
    a_in = _dot(xn, win_ref[:, 2 * D_A:2 * D_A + D_B])
    gate = _dot(xn, win_ref[:, 2 * D_A + D_B:])
    glu_ref[...] = a_in * jax.nn.sigmoid(gate)
    for c in range(TM // CHUNK):
        rows = slice(c * CHUNK, (c + 1) * CHUNK)
        for h in range(A_GROUPS):
            cols = slice(h * A_HEAD_DIM, (h + 1) * A_HEAD_DIM)
            mixed = _dot(wsp_ref[h], vn_scr[rows, cols]) + bsp_ref[:, cols]
            outa_ref[rows, cols] = (zu_scr[rows, cols] * mixed).astype(BF16)


def _front(x, g, w_in, vg, vb, wsp, bsp):
    t = x.shape[0]
    full = lambda shape: pl.BlockSpec(shape, lambda i: (0,) * len(shape))
    return pl.pallas_call(
        _front_kernel,
        grid=(t // TM,),
        in_specs=[
            pl.BlockSpec((TM, D_MODEL), lambda i: (i, 0)),
            full((1, D_MODEL)),
            full((D_MODEL, 2 * D_A + 2 * D_B)),
            full((1, D_A)), full((1, D_A)),
            full((A_GROUPS, CHUNK, CHUNK)),
            full((CHUNK, D_A)),
        ],
        out_specs=[pl.BlockSpec((TM, D_A), lambda i: (i, 0)),
                   pl.BlockSpec((TM, D_B), lambda i: (i, 0))],
        out_shape=[jax.ShapeDtypeStruct((t, D_A), BF16),
                   jax.ShapeDtypeStruct((t, D_B), F32)],
        scratch_shapes=[pltpu.VMEM((TM, D_A), F32), pltpu.VMEM((TM, D_A), BF16)],
        compiler_params=_cparams(("arbitrary",)),
        name="front",
    )(x, g, w_in, vg, vb, wsp, bsp)


CONV_ROWS = 64
SUBLANES = 8
SHIFT_ROWS = TM + (HALO + CONV_PAD) // SUBLANES * SUBLANES


def _split2(a):
    hi = a.astype(BF16)
    return hi, (a - hi.astype(F32)).astype(BF16)


def _store_token_tiles(tiles_ref, x):
    rows = x.shape[0]
    for c in range(LANE_SLABS):
        tiles_ref[pl.ds(c, rows, stride=SUB), :] = x[:, c * LANES:(c + 1) * LANES]


def _load_token_tiles(tiles_ref, rows, lead=()):
    parts = [tiles_ref[lead + (pl.ds(c, rows, stride=SUB), slice(None))] for c in range(LANE_SLABS)]
    return jnp.concatenate(parts, axis=1)


def _router_logits(x1, mg_ref, wrh_ref, wrl_ref):
    xh, xl = _split2(_rms(x1, mg_ref[...]))
    return _dot_nt(wrh_ref[...], xh) + (_dot_nt(wrh_ref[...], xl) + _dot_nt(wrl_ref[...], xh))


def _back_kernel(x_ref, outa_ref, glu_ref, prev_ref, next_ref, cw_ref, cb_ref, ng_ref, nb_ref,
                 wout_ref, mg_ref, wrh_ref, wrl_ref, x1t_ref, lg_ref, ext_scr, sh_scr, ob_scr,
                 *, blocks_per_seq):
    i = pl.program_id(0)
    pos = lax.rem(i, blocks_per_seq)
    ext_scr[0:HALO, :] = jnp.where(pos == 0, 0.0, prev_ref[...])
    ext_scr[HALO:HALO + TM, :] = glu_ref[...]
    ext_scr[HALO + TM:, :] = jnp.where(pos == blocks_per_seq - 1, 0.0, next_ref[...])
    for b in range(SUBLANES):
        sh_scr[b] = ext_scr[b:b + SHIFT_ROWS, :]
    for r in range(TM // CONV_ROWS):
        acc = jnp.zeros((CONV_ROWS, D_B), F32) + cb_ref[...]
        for k in range(CONV_WIDTH):
            a, b = divmod(HALO - CONV_PAD + k, SUBLANES)
            row0 = r * CONV_ROWS + a * SUBLANES
            acc = acc + cw_ref[k:k + 1, :] * sh_scr[b, row0:row0 + CONV_ROWS, :]
        ob = _silu(_ln(acc, ng_ref[...], nb_ref[...]))
        ob_scr[r * CONV_ROWS:(r + 1) * CONV_ROWS, :] = ob.astype(BF16)
    y = _dot(outa_ref[...], wout_ref[0:D_A, :]) + _dot(ob_scr[...], wout_ref[D_A:, :])
    x1 = x_ref[...] + y
    _store_token_tiles(x1t_ref, x1)
    lg_ref[...] = _router_logits(x1, mg_ref, wrh_ref, wrl_ref)


def _back(x, outa, glu, cw, cb, ng, nb_, wout, mg, wr, seq_len):
    t = x.shape[0]
    full = lambda shape: pl.BlockSpec(shape, lambda i: (0,) * len(shape))
    hb = TM // HALO
    return pl.pallas_call(
        functools.partial(_back_kernel, blocks_per_seq=seq_len // TM),
        grid=(t // TM,),
        in_specs=[
            pl.BlockSpec((TM, D_MODEL), lambda i: (i, 0)),
            pl.BlockSpec((TM, D_A), lambda i: (i, 0)),
            pl.BlockSpec((TM, D_B), lambda i: (i, 0)),
            pl.BlockSpec((HALO, D_B), lambda i: (jnp.maximum(i * hb - 1, 0), 0)),
            pl.BlockSpec((HALO, D_B), lambda i: (jnp.minimum((i + 1) * hb, t // HALO - 1), 0)),
            full((CONV_WIDTH, D_B)), full((1, D_B)), full((1, D_B)), full((1, D_B)),
            full((D_A + D_B, D_MODEL)),
            full((1, D_MODEL)),
            full((N_EXPERTS, D_MODEL)), full((N_EXPERTS, D_MODEL)),
        ],
        out_specs=[pl.BlockSpec((TM * SUB, LANES), lambda i: (i, 0)),
                   pl.BlockSpec((N_EXPERTS, TM), lambda i: (0, i))],
        out_shape=[jax.ShapeDtypeStruct((t * SUB, LANES), F32),
                   jax.ShapeDtypeStruct((N_EXPERTS, t), F32)],
        scratch_shapes=[pltpu.VMEM((TM + 2 * HALO, D_B), F32),
                        pltpu.VMEM((SUBLANES, SHIFT_ROWS, D_B), F32),
                        pltpu.VMEM((TM, D_B), BF16)],
        compiler_params=_cparams(("arbitrary",)),
        name="back",
    )(x, outa, glu, glu, glu, cw, cb, ng, nb_, wout, mg, *wr)


def _split3(a):
    a1 = a.astype(BF16)
    r1 = a - a1.astype(F32)
    a2 = r1.astype(BF16)
    a3 = (r1 - a2.astype(F32)).astype(BF16)
    return a1, a2, a3


def _route_kernel(lg_ref, idx_ref, gate_ref, cb_ref, pos_ref, cnt_tab, val_tab, cbr_scr, *, nbk, cap):
    ne = N_EXPERTS
    lg = lg_ref[...]
    ex = jnp.exp(lg - jnp.max(lg, axis=0, keepdims=True))
    aff = ex / jnp.sum(ex, axis=0, keepdims=True)

    def count(m):
        s = jnp.sum(m.astype(F32), axis=1, keepdims=True)
        return jnp.sum(s, axis=2, keepdims=True)

    def update(mid, lo, hi):
        ge = count(aff >= mid) >= cap
        return jnp.where(ge, mid, lo), jnp.where(ge, hi, mid)

    def bits_body(_, c):
        lo, hi = c
        lo_i = lax.bitcast_convert_type(lo, I32)
        hi_i = lax.bitcast_convert_type(hi, I32)
        mid = lax.bitcast_convert_type(lo_i + lax.shift_right_logical(hi_i - lo_i, 1), F32)
        return update(mid, lo, hi)

    def value_body(_, c):
        lo, hi = c
        return update(lo + (hi - lo) * 0.5, lo, hi)

    lo0 = jnp.zeros((ne, 1, 1), F32)
    hi0 = jnp.full((ne, 1, 1), jnp.inf, F32)
    c = lax.fori_loop(0, BISECT_BITS_STEPS, bits_body, (lo0, hi0))
    lo, hi = lax.fori_loop(0, BISECT_VALUE_STEPS, value_body, c)

    li = lax.broadcasted_iota(I32, (LANES, LANES), 0)
    lj = lax.broadcasted_iota(I32, (LANES, LANES), 1)
    ut = (li <= lj).astype(BF16)
    ones = jnp.ones((LANES, LANES), BF16)
    bi = lax.broadcasted_iota(I32, (nbk, nbk), 0)
    bj = lax.broadcasted_iota(I32, (nbk, nbk), 1)
    slt = (bj < bi).astype(BF16)
    utb = (bi <= bj).astype(BF16)
    ones8 = jnp.ones((8, LANES), BF16)

    def block_cum(m):
        m2 = m.astype(BF16).reshape(ne * nbk, LANES)
        loc = _dot(m2, ut).reshape(ne, nbk, LANES)
        tot = _dot(m2, ones).astype(BF16).reshape(ne, nbk, LANES)
        off = jnp.stack([_dot(slt, tot[e]) for e in range(ne)])
        return loc, tot, off

    gt = aff >= hi
    eq = (aff >= lo) & (aff < hi)
    need = np.float32(cap) - count(gt)
    eloc, _, eoff = block_cum(eq)
    mask = gt | (eq & (eloc + eoff <= need))

    mloc, mtot, moff = block_cum(mask)
    pos_ref[...] = jnp.where(mask, mloc + moff - 1.0, -1.0).astype(I32)
    cnt_tab[:, :, 0:LANES] = jnp.ones((ne, nbk, LANES), BF16)
    cnt_tab[:, :, LANES:] = mtot
    val_tab[:, :, 0:LANES] = mloc.astype(BF16)
    for k, piece in enumerate(_split3(aff)):
        val_tab[:, :, (k + 1) * LANES:(k + 2) * LANES] = piece
    maskb = mask.astype(BF16)
    for e in range(ne):
        cnt = _dot_nt(ones8, maskb[e])
        cbr_scr[e] = _dot(cnt.astype(BF16), utb)
    cb_ref[...] = cbr_scr[:, 0, :].astype(I32)

    s_col = lax.broadcasted_iota(I32, (cap, 1), 0).astype(F32)
    lane_b = lax.broadcasted_iota(I32, (cap, nbk), 1).astype(F32)
    lane_l = lax.broadcasted_iota(I32, (cap, LANES), 1).astype(F32)
    lane_e = lax.broadcasted_iota(I32, (cap, LANES), 1)

    def per_expert(e, _):
        cbrow = cbr_scr[e][0:1, :]
        le = (cbrow <= s_col).astype(BF16)
        bs = _dot(le, cnt_tab[e])
        blk_c, start_c = bs[:, 0:LANES], bs[:, LANES:]
        oh = (lane_b == blk_c[:, 0:nbk]).astype(BF16)
        vals = _dot(oh, val_tab[e])
        le2 = (vals[:, 0:LANES] <= s_col - start_c).astype(BF16)
        within_c = _dot(le2, ones)
        mine = lane_e == e
        idx_ref[...] = jnp.where(mine, (blk_c * LANES + within_c).astype(I32), idx_ref[...])
        aff_row = (vals[:, LANES:2 * LANES] + vals[:, 2 * LANES:3 * LANES]) + vals[:, 3 * LANES:]
        g_col = jnp.sum(jnp.where(lane_l == within_c, aff_row, 0.0), axis=1, keepdims=True)
        gate_ref[...] = jnp.where(mine, g_col, gate_ref[...])
        return 0

    idx_ref[...] = jnp.zeros_like(idx_ref)
    gate_ref[...] = jnp.zeros_like(gate_ref)

    lax.fori_loop(0, ne, per_expert, 0)


def _route(logits_t, cap):
    ne, t = logits_t.shape
    nbk = t // LANES
    lg3 = logits_t.reshape(ne, nbk, LANES)
    full = lambda shape: pl.BlockSpec(shape, lambda i: (0,) * len(shape))
    idx, gate, cb, pos = pl.pallas_call(
        functools.partial(_route_kernel, nbk=nbk, cap=cap),
        grid=(1,),
        in_specs=[full((ne, nbk, LANES))],
        out_specs=[full((cap, LANES)), full((cap, LANES)), full((ne, nbk)), full((ne, nbk, LANES))],
        out_shape=[jax.ShapeDtypeStruct((cap, LANES), I32),
                   jax.ShapeDtypeStruct((cap, LANES), F32),
                   jax.ShapeDtypeStruct((ne, nbk), I32),
                   jax.ShapeDtypeStruct((ne, nbk, LANES), I32)],
        scratch_shapes=[pltpu.VMEM((ne, nbk, 2 * LANES), BF16), pltpu.VMEM((ne, nbk, 4 * LANES), BF16),
                        pltpu.VMEM((ne, 8, nbk), F32)],
        compiler_params=_cparams(("arbitrary",)),
        name="route",
    )(lg3)
    return jnp.transpose(idx[:, :ne]), jnp.transpose(gate[:, :ne]), cb, pos


def _ffn_kernel(idx_ref, idxn_ref, gate_ref, x_hbm, x2d_hbm, mg_ref, wg_hbm, wu_hbm, wd_hbm, ye_ref,
                xbuf, wst, wbf, sem, wsem, *, layer, ne, nsteps, rb):
    e = pl.program_id(0)
    step = e * pl.num_programs(1) + pl.program_id(1)
    slot = lax.rem(step, 2)

    def weight_copies(ex):
        return [pltpu.make_async_copy(w_hbm.at[layer, ex], wst.at[m], wsem.at[m])
                for m, w_hbm in enumerate((wg_hbm, wu_hbm, wd_hbm))]

    def issue(ids_ref, s):
        for r in range(rb):
            pltpu.make_async_copy(x_hbm.at[ids_ref[0, 0, r]], xbuf.at[s, pl.ds(r * SUB, SUB)],
                                  sem.at[s]).start()

    def wait(s):
        pltpu.make_async_copy(x2d_hbm.at[pl.ds(0, rb * SUB)], xbuf.at[s], sem.at[s]).wait()

    @pl.when(step == 0)
    def _():
        for cp in weight_copies(0):
            cp.start(priority=1)
        issue(idx_ref, 0)

    @pl.when(pl.program_id(1) == 0)
    def _():
        for m, cp in enumerate(weight_copies(e)):
            cp.wait()
            wbf[m] = wst[m].astype(BF16)

        @pl.when(e + 1 < ne)
        def _():
            for cp in weight_copies(e + 1):
                cp.start(priority=1)

    issue(idxn_ref, 1 - slot)
    wait(slot)

    xe = _rms(_load_token_tiles(xbuf, rb, (slot,)), mg_ref[...]).astype(BF16)
    hid = (_silu(_dot(xe, wbf[0])) * _dot(xe, wbf[1])).astype(BF16)
    ye = _dot(hid, wbf[2])
    ri = lax.broadcasted_iota(I32, (LANES, LANES), 0)
    ci = lax.broadcasted_iota(I32, (LANES, LANES), 1)
    gate = gate_ref[0]
    gcol = jnp.concatenate(
        [jnp.sum(jnp.where(ri == ci, gate[:, q:q + LANES], 0.0), axis=1, keepdims=True)
         for q in range(0, rb, LANES)], axis=0)
    ye_ref[...] = (ye * gcol).astype(BF16)

    @pl.when(step == nsteps - 1)
    def _():
        wait(1 - slot)


def _ffn(idx, gate, x1t, mg, wg, wu, wd, layer):
    ne, cap = idx.shape
    t = x1t.shape[0] // SUB
    rb = min(RB, cap)
    nblk = cap // rb
    nsteps = ne * nblk
    idx3 = idx.reshape(nsteps, 1, rb)
    gate3 = gate.reshape(nsteps, 1, rb)
    hbm = pl.BlockSpec(memory_space=pl.ANY)
    return pl.pallas_call(
        functools.partial(_ffn_kernel, layer=layer, ne=ne, nsteps=nsteps, rb=rb),
        grid=(ne, nblk),
        in_specs=[
            pl.BlockSpec((1, 1, rb), lambda e, j: (e * nblk + j, 0, 0), memory_space=pltpu.SMEM),
            pl.BlockSpec((1, 1, rb), lambda e, j: (jnp.minimum(e * nblk + j + 1, nsteps - 1), 0, 0),
                         memory_space=pltpu.SMEM),
            pl.BlockSpec((1, 1, rb), lambda e, j: (e * nblk + j, 0, 0)),
            hbm, hbm,
            pl.BlockSpec((1, D_MODEL), lambda e, j: (0, 0)),
            hbm, hbm, hbm,
        ],
        out_specs=pl.BlockSpec((rb, D_MODEL), lambda e, j: (e * nblk + j, 0)),
        out_shape=jax.ShapeDtypeStruct((ne * cap, D_MODEL), BF16),
        scratch_shapes=[pltpu.VMEM((2, rb * SUB, LANES), F32),
                        pltpu.VMEM((3, D_MODEL, D_MODEL), F32),
                        pltpu.VMEM((3, D_MODEL, D_MODEL), BF16),
                        pltpu.SemaphoreType.DMA((2,)), pltpu.SemaphoreType.DMA((3,))],
        compiler_params=_cparams(("arbitrary", "arbitrary")),
        name="ffn",
    )(idx3, idx3, gate3, x1t.reshape(t, SUB, LANES), x1t, mg, wg, wu, wd)


CHUNK_ROWS = 16
MAIN_CHUNKS = 96
MAIN_ROWS = CHUNK_ROWS * MAIN_CHUNKS
MAX_CHUNKS = -(-(N_EXPERTS * TB // CHUNK_ROWS + 2 * N_EXPERTS) // MAIN_CHUNKS) * MAIN_CHUNKS
NO_SLOT = 1 << 30


def _combine_kernel(cb_ref, x1t_ref, pos_ref, ye_hbm, *rest, nbk, cap, ntb, final):
    if final:
        g_ref, out_ref, ybuf, acc_scr, sem, ie_scr, ir_scr, n_scr = rest
    else:
        out_ref, ybuf, acc_scr, sem, ie_scr, ir_scr, n_scr = rest
    tb = pl.program_id(0)
    s = lax.rem(tb, 2)
    bpt = TB // LANES
    shift = CHUNK_ROWS.bit_length() - 1

    def build(tbx, par):
        base = par * MAX_CHUNKS
        n = jnp.int32(0)
        for e in range(N_EXPERTS):
            b0 = e * nbk + tbx * bpt
            lo = jnp.where(tbx == 0, 0, cb_ref[jnp.maximum(b0 - 1, 0)])
            hi = cb_ref[b0 + bpt - 1]
            start = lax.shift_left(lax.shift_right_logical(lo, shift), shift)
            nch = jnp.where(hi > lo,
                            lax.shift_right_logical(hi - start + (CHUNK_ROWS - 1), shift), 0)

            def push(c, k, e=e, start=start):
                ie_scr[base + k] = e
                ir_scr[base + k] = start + c * CHUNK_ROWS
                return k + 1

            n = lax.fori_loop(0, nch, push, n)
        nsg = jnp.maximum(lax.div(n + (MAIN_CHUNKS - 1), MAIN_CHUNKS), 1)

        def pad(k, _):
            ie_scr[base + k] = 0
            ir_scr[base + k] = NO_SLOT
            return 0

        lax.fori_loop(n, nsg * MAIN_CHUNKS, pad, 0)
        n_scr[par] = n
        return n

    def chunk_copy(par, k, k0):
        base = par * MAX_CHUNKS
        row0 = pl.multiple_of(ie_scr[base + k] * cap + ir_scr[base + k], CHUNK_ROWS)
        dst0 = pl.multiple_of((k - k0) * CHUNK_ROWS, CHUNK_ROWS)
        return pltpu.make_async_copy(ye_hbm.at[pl.ds(row0, CHUNK_ROWS)],
                                     ybuf.at[par, pl.ds(dst0, CHUNK_ROWS)], sem.at[par])

    def issue(par, k0, k1):
        def body(k, _):
            chunk_copy(par, k, k0).start()
            return 0
        lax.fori_loop(k0, k1, body, 0)

    def drain(par, k0, k1):
        def body(k, _):
            chunk_copy(par, k, k0).wait()
            return 0
        lax.fori_loop(k0, k1, body, 0)

    @pl.when(tb == 0)
    def _():
        ybuf[...] = jnp.zeros_like(ybuf)
        n0 = build(0, 0)
        issue(0, 0, jnp.minimum(n0, MAIN_CHUNKS))

    @pl.when(tb + 1 < ntb)
    def _():
        n1 = build(tb + 1, 1 - s)
        issue(1 - s, 0, jnp.minimum(n1, MAIN_CHUNKS))

    jrow = lax.broadcasted_iota(I32, (CHUNK_ROWS, TB), 0)

    def scatter(sg):
        base = s * MAX_CHUNKS + sg * MAIN_CHUNKS
        pieces = []
        for c in range(MAIN_CHUNKS):
            prow = pos_ref[pl.ds(ie_scr[base + c], 1), :]
            pieces.append((prow == jrow + ir_scr[base + c]).astype(BF16))
        onehot_t = jnp.concatenate(pieces, axis=0)
        return lax.dot_general(onehot_t, ybuf[s], (((0,), (0,)), ((), ())),
                               preferred_element_type=F32)

    def finish(x2):
        return _rms(x2, g_ref[...]) if final else x2

    n = n_scr[s]
    drain(s, 0, jnp.minimum(n, MAIN_CHUNKS))
    nsg = lax.div(n + (MAIN_CHUNKS - 1), MAIN_CHUNKS)

    @pl.when(nsg <= 1)
    def _():
        out_ref[...] = finish(_load_token_tiles(x1t_ref, TB) + scatter(0))

    @pl.when(nsg > 1)
    def _():
        acc_scr[...] = _load_token_tiles(x1t_ref, TB) + scatter(0)

        def extra(sg, _):
            k0 = sg * MAIN_CHUNKS
            k1 = jnp.minimum(n, k0 + MAIN_CHUNKS)
            issue(s, k0, k1)
            drain(s, k0, k1)
            acc_scr[...] += scatter(sg)
            return 0

        lax.fori_loop(1, nsg, extra, 0)
        out_ref[...] = finish(acc_scr[...])


def _combine(cb, pos, x1t, ye, g_final=None):
    t = x1t.shape[0] // SUB
    ne, nbk = cb.shape
    cap = ye.shape[0] // ne
    final = g_final is not None
    blk = pl.BlockSpec((TB, D_MODEL), lambda i, cb: (i, 0))
    in_specs = [pl.BlockSpec((TB * SUB, LANES), lambda i, cb: (i, 0)),
                pl.BlockSpec((ne, TB), lambda i, cb: (0, i)), pl.BlockSpec(memory_space=pl.ANY)]
    args = [cb.reshape(-1), x1t, pos.reshape(ne, t), ye]
    if final:
        in_specs.append(pl.BlockSpec((1, D_MODEL), lambda i, cb: (0, 0)))
        args.append(g_final)
    grid_spec = pltpu.PrefetchScalarGridSpec(
        num_scalar_prefetch=1,
        grid=(t // TB,),
        in_specs=in_specs,
        out_specs=blk,
        scratch_shapes=[pltpu.VMEM((2, MAIN_ROWS, D_MODEL), BF16),
                        pltpu.VMEM((TB, D_MODEL), F32),
                        pltpu.SemaphoreType.DMA((2,)),
                        pltpu.SMEM((2 * MAX_CHUNKS,), I32), pltpu.SMEM((2 * MAX_CHUNKS,), I32),
                        pltpu.SMEM((2,), I32)],
    )
    return pl.pallas_call(
        functools.partial(_combine_kernel, nbk=nbk, cap=cap, ntb=t // TB, final=final),
        grid_spec=grid_spec,
        out_shape=jax.ShapeDtypeStruct((t, D_MODEL), F32),
        compiler_params=_cparams(("arbitrary",)),
        name="combine_final" if final else "combine",
    )(*args)


def _dft_mats(n, scale):
    k = np.arange(n)
    ang = 2.0 * np.pi * ((k[:, None] * k[None, :]) % n) / n
    return (jnp.asarray(np.cos(ang) * scale, F32).astype(BF16),
            jnp.asarray(np.sin(ang) * scale, F32).astype(BF16))


def _lane_slab_specs(block, index_map):
    return [pl.BlockSpec(block + (LANES,), functools.partial(lambda *i, c: index_map(*i) + (c,), c=c))
            for c in range(LANE_SLABS)]


def _every_8th_row(slabs, j, rows):
    parts = [slabs[c, pl.ds(j, rows, stride=SUB), :] if not isinstance(slabs, (list, tuple))
             else slabs[c][pl.ds(j, rows, stride=SUB), :] for c in range(LANE_SLABS)]
    return jnp.concatenate(parts, axis=1)


def _fft1_kernel(*refs):
    xs = refs[:LANE_SLABS]
    g_ref, cs1_ref, twc_ref, tws_ref, br_ref, bi_ref, x_scr = refs[LANE_SLABS:]
    n1 = cs1_ref.shape[1]
    for c in range(LANE_SLABS):
        x_scr[c] = xs[c][...].reshape(n1 * SUB, LANES)
    for j in range(SUB):
        xj = _rms(_every_8th_row(x_scr, j, n1), g_ref[...]).astype(BF16)
        a = _dot(cs1_ref[...], xj)
        ar, ai = a[:n1], a[n1:]
        c = twc_ref[0, :, j:j + 1]
        s = tws_ref[0, :, j:j + 1]
        br_ref[:, j] = (ar * c + ai * s).reshape(n1 // SUB, SUB, D_MODEL)
        bi_ref[:, j] = (ai * c - ar * s).reshape(n1 // SUB, SUB, D_MODEL)


def _fft1(x, g, nb, n1):
    t = x.shape[0]
    n = n1 * FFT_N2
    c1, s1 = _dft_mats(n1, 1.0 / math.sqrt(n1))
    cs1 = jnp.concatenate([c1, -s1], axis=0)
    k1 = np.arange(n1)[:, None]
    n2 = np.arange(FFT_N2)[None, :]
    ang = 2.0 * np.pi * ((k1 * n2) % n) / n
    nj = FFT_N2 // SUB
    nkc = n1 // SUB
    to_tab = lambda a: jnp.asarray(a.reshape(n1, nj, SUB).transpose(1, 0, 2), F32)
    twc, tws = to_tab(np.cos(ang)), to_tab(np.sin(ang))
    xv = x.reshape(nb * n1, FFT_N2, D_MODEL)
    out_blk = pl.BlockSpec((nkc, SUB, SUB, D_MODEL), lambda b, j: (b, j, 0, 0))
    out_shape = jax.ShapeDtypeStruct((nb * nkc, FFT_N2, SUB, D_MODEL), F32)
    tws_spec = pl.BlockSpec((1, n1, SUB), lambda b, j: (j, 0, 0))
    mat = pl.BlockSpec((2 * n1, n1), lambda b, j: (0, 0))
    br, bi = pl.pallas_call(
        _fft1_kernel,
        grid=(nb, nj),
        in_specs=_lane_slab_specs((n1, SUB), lambda b, j: (b, j))
        + [pl.BlockSpec((1, D_MODEL), lambda b, j: (0, 0)), mat, tws_spec, tws_spec],
        out_specs=[out_blk, out_blk],
        out_shape=[out_shape, out_shape],
        scratch_shapes=[pltpu.VMEM((LANE_SLABS, n1 * SUB, LANES), F32)],
        compiler_params=_cparams(("arbitrary", "arbitrary")),
        name="fft1",
    )(*([xv] * LANE_SLABS), g, cs1, twc, tws)
    return br.reshape(t, D_MODEL), bi.reshape(t, D_MODEL)


FFT_PAIR = 2


def _fft2_kernel(*refs):
    brs, bis, xs = (refs[i * LANE_SLABS:(i + 1) * LANE_SLABS] for i in range(3))
    (w2r_ref, w2i_ref, cc_ref, sc_ref, wout_ref, mg_ref, wrh_ref, wrl_ref,
     x1t_ref, lg_ref, f_scr, x_scr) = refs[3 * LANE_SLABS:]
    n2 = FFT_N2
    for c in range(LANE_SLABS):
        x_scr[c] = xs[c][...].reshape(n2 * SUB, LANES)
    for jp in range(SUB // FFT_PAIR):
        for q in range(FFT_PAIR):
            j = jp * FFT_PAIR + q
            bj = jnp.concatenate([_every_8th_row(brs, j, n2).astype(BF16),
                                  _every_8th_row(bis, j, n2).astype(BF16)], axis=0)
            yr = _dot(w2r_ref[...], bj).astype(BF16)
            yi = _dot(w2i_ref[...], bj).astype(BF16)
            for g in range(C_GROUPS):
                cols = slice(g * C_GROUP_DIM, (g + 1) * C_GROUP_DIM)
                f = _dot(yr[:, cols], cc_ref[...]) + _dot(yi[:, cols], sc_ref[...])
                f_scr[q * n2:(q + 1) * n2, cols] = f.astype(BF16)
        y = _dot(f_scr[...], wout_ref[...])
        for q in range(FFT_PAIR):
            j = jp * FFT_PAIR + q
            x1 = _every_8th_row(x_scr, j, n2) + y[q * n2:(q + 1) * n2, :]
            x1t_ref[:, j * SUB:(j + 1) * SUB, :] = x1.reshape(n2, SUB, LANES)
            lg_ref[:, j * n2:(j + 1) * n2] = _router_logits(x1, mg_ref, wrh_ref, wrl_ref)


def _fft2(br, bi, x, nb, n1, wout, mg, wr):
    t = x.shape[0]
    n2 = FFT_N2
    c2, s2 = _dft_mats(n2, 1.0 / math.sqrt(n2))
    w2r = jnp.concatenate([c2, s2], axis=1)
    w2i = jnp.concatenate([-s2, c2], axis=1)
    cc, sc = _dft_mats(C_GROUP_DIM, 1.0 / math.sqrt(C_GROUP_DIM))
    nkc = n1 // SUB
    rows = SUB * n2
    xv = x.reshape(nb * n2, n1, D_MODEL)
    full = lambda shape: pl.BlockSpec(shape, lambda b, k: (0,) * len(shape))
    b_slabs = _lane_slab_specs((rows,), lambda b, k: (b * nkc + k,))
    x_slabs = _lane_slab_specs((n2, SUB), lambda b, k: (b, k))
    x1, lg = pl.pallas_call(
        _fft2_kernel,
        grid=(nb, nkc),
        in_specs=b_slabs * 2 + x_slabs
        + [full((n2, 2 * n2)), full((n2, 2 * n2)),
           full((C_GROUP_DIM, C_GROUP_DIM)), full((C_GROUP_DIM, C_GROUP_DIM)),
           full((D_MODEL, D_MODEL)), full((1, D_MODEL)),
           full((N_EXPERTS, D_MODEL)), full((N_EXPERTS, D_MODEL))],
        out_specs=[pl.BlockSpec((n2, SUB * SUB, LANES), lambda b, k: (b, k, 0)),
                   pl.BlockSpec((N_EXPERTS, rows), lambda b, k: (0, b * nkc + k))],
        out_shape=[jax.ShapeDtypeStruct((nb * n2, n1 * SUB, LANES), F32),
                   jax.ShapeDtypeStruct((N_EXPERTS, t), F32)],
        scratch_shapes=[pltpu.VMEM((FFT_PAIR * n2, D_MODEL), BF16),
                        pltpu.VMEM((LANE_SLABS, rows, LANES), F32)],
        compiler_params=_cparams(("arbitrary", "arbitrary")),
        name="fft2",
    )(*([br] * LANE_SLABS), *([bi] * LANE_SLABS), *([xv] * LANE_SLABS),
      w2r, w2i, cc, sc, wout, mg, *wr)
    lg = lg.reshape(N_EXPERTS, nb, n1, n2).transpose(0, 1, 3, 2).reshape(N_EXPERTS, t)
    return x1.reshape(t * SUB, LANES), lg


def _moe(x1, logits_t, mg, p, layer, g_final=None):
    t = x1.shape[0] // SUB
    cap = CAPACITY_FACTOR * t // N_EXPERTS
    idx, gate, cb, pos = _route(logits_t, cap)
    ye = _ffn(idx, gate, x1, mg, p["wg"], p["wu"], p["wd"], layer)
    return _combine(cb, pos, x1, ye, g_final)


def _trunk(x3, p):
    nb, s, _ = x3.shape
    x = x3.reshape(nb * s, D_MODEL)
    row = lambda v: v.reshape(1, -1)
    outa, glu = _front(x, row(p["mix_norm"][0]), p["w_in"], row(p["ab_v_norm_g"][0]),
                       row(p["ab_v_norm_b"][0]), p["w_spatial"], p["b_spatial"])
    mg0, mg1 = row(p["moe_norm"][0]), row(p["moe_norm"][1])
    x1, lg = _back(x, outa, glu, p["ab_conv_w"][0], row(p["ab_conv_b"][0]),
                   row(p["ab_conv_norm_g"][0]), row(p["ab_conv_norm_b"][0]), p["ab_w_out"],
                   mg0, p["router"][0], s)
    x2 = _moe(x1, lg, mg0, p, 0)
    n1 = s // FFT_N2
    br, bi = _fft1(x2, row(p["mix_norm"][1]), nb, n1)
    x1, lg = _fft2(br, bi, x2, nb, n1, p["c_w_out"], mg1, p["router"][1])
    y = _moe(x1, lg, mg1, p, 1, row(p["final_norm"]))
    return y.reshape(nb, s, D_MODEL)


def kernel(x_prompt, x_sample, mix_norm, ab_w_in, ab_v_norm_g, ab_v_norm_b, ab_w_spatial,
           ab_b_spatial, ab_conv_w, ab_conv_b, ab_conv_norm_g, ab_conv_norm_b, ab_w_out,
           c_w_out, moe_norm, moe_router, moe_w_gate, moe_w_up, moe_w_down, final_norm):
    p = dict(
        mix_norm=mix_norm, moe_norm=moe_norm, final_norm=final_norm,
        w_in=ab_w_in[0].astype(BF16),
        ab_v_norm_g=ab_v_norm_g, ab_v_norm_b=ab_v_norm_b,
        w_spatial=ab_w_spatial[0].astype(BF16),
        b_spatial=jnp.repeat(jnp.transpose(ab_b_spatial[0]), A_HEAD_DIM, axis=1),
        ab_conv_w=ab_conv_w, ab_conv_b=ab_conv_b,
        ab_conv_norm_g=ab_conv_norm_g, ab_conv_norm_b=ab_conv_norm_b,
        ab_w_out=ab_w_out[0].astype(BF16),
        c_w_out=c_w_out[0].astype(BF16),
        router=[_split2(jnp.transpose(moe_router[l])) for l in range(moe_router.shape[0])],
        wg=moe_w_gate, wu=moe_w_up, wd=moe_w_down,
    )
    return (_trunk(x_prompt, p), _trunk(x_sample, p))
```

```python
import functools
import math

import numpy as np
import jax
import jax.numpy as jnp
from jax import lax
from jax.experimental import pallas as pl
from jax.experimental.pallas import tpu as pltpu

F32, BF16, I32 = jnp.float32, jnp.bfloat16, jnp.int32

D_MODEL = 1024
D_A = 512
D_B = 512
A_GROUPS = 4
A_HEAD_DIM = 128
CHUNK = 128
CONV_WIDTH = 31
CONV_PAD = 15
C_GROUPS = 4
C_GROUP_DIM = 256
N_EXPERTS = 16
CAPACITY_FACTOR = 2
RMS_EPS = 1e-6
LN_EPS = 1e-5

LANES = 128
SUB = 8
LANE_SLABS = D_MODEL // LANES
HALO = 16
TM = 512
RB = 512
TB = 512
FFT_N2 = 128
VMEM_LIMIT = 48 * 1024 * 1024
BISECT_BITS_STEPS = 31
BISECT_VALUE_STEPS = 30


def _cparams(sem):
    return pltpu.CompilerParams(dimension_semantics=sem, vmem_limit_bytes=VMEM_LIMIT)


def _dot(a, b):
    return jnp.dot(a, b, preferred_element_type=F32)


def _dot_nt(a, b, precision=None):
    return lax.dot_general(a, b, (((1,), (1,)), ((), ())), precision=precision,
                           preferred_element_type=F32)


def _rms(x, g):
    return x * lax.rsqrt(jnp.mean(x * x, axis=-1, keepdims=True) + RMS_EPS) * g


def _ln(x, g, b):
    mu = jnp.mean(x, axis=-1, keepdims=True)
    xc = x - mu
    var = jnp.mean(xc * xc, axis=-1, keepdims=True)
    return xc * lax.rsqrt(var + LN_EPS) * g + b


def _gelu(x):
    return 0.5 * x * (1.0 + lax.erf(x * np.float32(math.sqrt(0.5))))


def _silu(x):
    return x * jax.nn.sigmoid(x)


def _front_kernel(x_ref, g_ref, win_ref, vg_ref, vb_ref, wsp_ref, bsp_ref,
                  outa_ref, glu_ref, zu_scr, vn_scr):
    xn = _rms(x_ref[...], g_ref[...]).astype(BF16)
    zu_scr[...] = _gelu(_dot(xn, win_ref[:, 0:D_A]))
    zv = _gelu(_dot(xn, win_ref[:, D_A:2 * D_A]))
    vn_scr[...] = _ln(zv, vg_ref[...], vb_ref[...]).astype(BF16)
    a_in = _dot(xn, win_ref[:, 2 * D_A:2 * D_A + D_B])
    gate = _dot(xn, win_ref[:, 2 * D_A + D_B:])
    glu_ref[...] = a_in * jax.nn.sigmoid(gate)
    for c in range(TM // CHUNK):
        rows = slice(c * CHUNK, (c + 1) * CHUNK)
        for h in range(A_GROUPS):
            cols = slice(h * A_HEAD_DIM, (h + 1) * A_HEAD_DIM)
            mixed = _dot(wsp_ref[h], vn_scr[rows, cols]) + bsp_ref[:, cols]
            outa_ref[rows, cols] = (zu_scr[rows, cols] * mixed).astype(BF16)


def _front(x, g, w_in, vg, vb, wsp, bsp):
    t = x.shape[0]
    full = lambda shape: pl.BlockSpec(shape, lambda i: (0,) * len(shape))
    return pl.pallas_call(
        _front_kernel,
        grid=(t // TM,),
        in_specs=[
            pl.BlockSpec((TM, D_MODEL), lambda i: (i, 0)),
            full((1, D_MODEL)),
            full((D_MODEL, 2 * D_A + 2 * D_B)),
            full((1, D_A)), full((1, D_A)),
            full((A_GROUPS, CHUNK, CHUNK)),
            full((CHUNK, D_A)),
        ],
        out_specs=[pl.BlockSpec((TM, D_A), lambda i: (i, 0)),
                   pl.BlockSpec((TM, D_B), lambda i: (i, 0))],
        out_shape=[jax.ShapeDtypeStruct((t, D_A), BF16),
                   jax.ShapeDtypeStruct((t, D_B), F32)],
        scratch_shapes=[pltpu.VMEM((TM, D_A), F32), pltpu.VMEM((TM, D_A), BF16)],
        compiler_params=_cparams(("arbitrary",)),
        name="front",
    )(x, g, w_in, vg, vb, wsp, bsp)


CONV_ROWS = 64
SUBLANES = 8
SHIFT_ROWS = TM + (HALO + CONV_PAD) // SUBLANES * SUBLANES


def _split2(a):
    hi = a.astype(BF16)
    return hi, (a - hi.astype(F32)).astype(BF16)


def _store_token_tiles(tiles_ref, x):
    rows = x.shape[0]
    for c in range(LANE_SLABS):
        tiles_ref[pl.ds(c, rows, stride=SUB), :] = x[:, c * LANES:(c + 1) * LANES]


def _load_token_tiles(tiles_ref, rows, lead=()):
    parts = [tiles_ref[lead + (pl.ds(c, rows, stride=SUB), slice(None))] for c in range(LANE_SLABS)]
    return jnp.concatenate(parts, axis=1)


def _router_logits(x1, mg_ref, wrh_ref, wrl_ref):
    xh, xl = _split2(_rms(x1, mg_ref[...]))
    return _dot_nt(wrh_ref[...], xh) + (_dot_nt(wrh_ref[...], xl) + _dot_nt(wrl_ref[...], xh))


def _back_kernel(x_ref, outa_ref, glu_ref, prev_ref, next_ref, cw_ref, cb_ref, ng_ref, nb_ref,
                 wout_ref, mg_ref, wrh_ref, wrl_ref, x1t_ref, lg_ref, ext_scr, sh_scr, ob_scr,
                 *, blocks_per_seq):
    i = pl.program_id(0)
    pos = lax.rem(i, blocks_per_seq)
    ext_scr[0:HALO, :] = jnp.where(pos == 0, 0.0, prev_ref[...])
    ext_scr[HALO:HALO + TM, :] = glu_ref[...]
    ext_scr[HALO + TM:, :] = jnp.where(pos == blocks_per_seq - 1, 0.0, next_ref[...])
    for b in range(SUBLANES):
        sh_scr[b] = ext_scr[b:b + SHIFT_ROWS, :]
    for r in range(TM // CONV_ROWS):
        acc = jnp.zeros((CONV_ROWS, D_B), F32) + cb_ref[...]
        for k in range(CONV_WIDTH):
            a, b = divmod(HALO - CONV_PAD + k, SUBLANES)
            row0 = r * CONV_ROWS + a * SUBLANES
            acc = acc + cw_ref[k:k + 1, :] * sh_scr[b, row0:row0 + CONV_ROWS, :]
        ob = _silu(_ln(acc, ng_ref[...], nb_ref[...]))
        ob_scr[r * CONV_ROWS:(r + 1) * CONV_ROWS, :] = ob.astype(BF16)
    y = _dot(outa_ref[...], wout_ref[0:D_A, :]) + _dot(ob_scr[...], wout_ref[D_A:, :])
    x1 = x_ref[...] + y
    _store_token_tiles(x1t_ref, x1)
    lg_ref[...] = _router_logits(x1, mg_ref, wrh_ref, wrl_ref)


def _back(x, outa, glu, cw, cb, ng, nb_, wout, mg, wr, seq_len):
    t = x.shape[0]
    full = lambda shape: pl.BlockSpec(shape, lambda i: (0,) * len(shape))
    hb = TM // HALO
    return pl.pallas_call(
        functools.partial(_back_kernel, blocks_per_seq=seq_len // TM),
        grid=(t // TM,),
        in_specs=[
            pl.BlockSpec((TM, D_MODEL), lambda i: (i, 0)),
            pl.BlockSpec((TM, D_A), lambda i: (i, 0)),
            pl.BlockSpec((TM, D_B), lambda i: (i, 0)),
            pl.BlockSpec((HALO, D_B), lambda i: (jnp.maximum(i * hb - 1, 0), 0)),
            pl.BlockSpec((HALO, D_B), lambda i: (jnp.minimum((i + 1) * hb, t // HALO - 1), 0)),
            full((CONV_WIDTH, D_B)), full((1, D_B)), full((1, D_B)), full((1, D_B)),
            full((D_A + D_B, D_MODEL)),
            full((1, D_MODEL)),
            full((N_EXPERTS, D_MODEL)), full((N_EXPERTS, D_MODEL)),
        ],
        out_specs=[pl.BlockSpec((TM * SUB, LANES), lambda i: (i, 0)),
                   pl.BlockSpec((N_EXPERTS, TM), lambda i: (0, i))],
        out_shape=[jax.ShapeDtypeStruct((t * SUB, LANES), F32),
                   jax.ShapeDtypeStruct((N_EXPERTS, t), F32)],
        scratch_shapes=[pltpu.VMEM((TM + 2 * HALO, D_B), F32),
                        pltpu.VMEM((SUBLANES, SHIFT_ROWS, D_B), F32),
                        pltpu.VMEM((TM, D_B), BF16)],
        compiler_params=_cparams(("arbitrary",)),
        name="back",
    )(x, outa, glu, glu, glu, cw, cb, ng, nb_, wout, mg, *wr)


def _split3(a):
    a1 = a.astype(BF16)
    r1 = a - a1.astype(F32)
    a2 = r1.astype(BF16)
    a3 = (r1 - a2.astype(F32)).astype(BF16)
    return a1, a2, a3


def _route_kernel(lg_ref, idx_ref, gate_ref, cb_ref, pos_ref, cnt_tab, val_tab, cbr_scr, *, nbk, cap):
    ne = N_EXPERTS
    lg = lg_ref[...]
    ex = jnp.exp(lg - jnp.max(lg, axis=0, keepdims=True))
    aff = ex / jnp.sum(ex, axis=0, keepdims=True)

    def count(m):
        s = jnp.sum(m.astype(F32), axis=1, keepdims=True)
        return jnp.sum(s, axis=2, keepdims=True)

    def update(mid, lo, hi):
        ge = count(aff >= mid) >= cap
        return jnp.where(ge, mid, lo), jnp.where(ge, hi, mid)

    def bits_body(_, c):
        lo, hi = c
        lo_i = lax.bitcast_convert_type(lo, I32)
        hi_i = lax.bitcast_convert_type(hi, I32)
        mid = lax.bitcast_convert_type(lo_i + lax.shift_right_logical(hi_i - lo_i, 1), F32)
        return update(mid, lo, hi)

    def value_body(_, c):
        lo, hi = c
        return update(lo + (hi - lo) * 0.5, lo, hi)

    lo0 = jnp.zeros((ne, 1, 1), F32)
    hi0 = jnp.full((ne, 1, 1), jnp.inf, F32)
    c = lax.fori_loop(0, BISECT_BITS_STEPS, bits_body, (lo0, hi0))
    lo, hi = lax.fori_loop(0, BISECT_VALUE_STEPS, value_body, c)

    li = lax.broadcasted_iota(I32, (LANES, LANES), 0)
    lj = lax.broadcasted_iota(I32, (LANES, LANES), 1)
    ut = (li <= lj).astype(BF16)
    ones = jnp.ones((LANES, LANES), BF16)
    bi = lax.broadcasted_iota(I32, (nbk, nbk), 0)
    bj = lax.broadcasted_iota(I32, (nbk, nbk), 1)
    slt = (bj < bi).astype(BF16)
    utb = (bi <= bj).astype(BF16)
    ones8 = jnp.ones((8, LANES), BF16)

    def block_cum(m):
        m2 = m.astype(BF16).reshape(ne * nbk, LANES)
        loc = _dot(m2, ut).reshape(ne, nbk, LANES)
        tot = _dot(m2, ones).astype(BF16).reshape(ne, nbk, LANES)
        off = jnp.stack([_dot(slt, tot[e]) for e in range(ne)])
        return loc, tot, off

    gt = aff >= hi
    eq = (aff >= lo) & (aff < hi)
    need = np.float32(cap) - count(gt)
    eloc, _, eoff = block_cum(eq)
    mask = gt | (eq & (eloc + eoff <= need))

    mloc, mtot, moff = block_cum(mask)
    pos_ref[...] = jnp.where(mask, mloc + moff - 1.0, -1.0).astype(I32)
    cnt_tab[:, :, 0:LANES] = jnp.ones((ne, nbk, LANES), BF16)
    cnt_tab[:, :, LANES:] = mtot
    val_tab[:, :, 0:LANES] = mloc.astype(BF16)
    for k, piece in enumerate(_split3(aff)):
        val_tab[:, :, (k + 1) * LANES:(k + 2) * LANES] = piece
    maskb = mask.astype(BF16)
    for e in range(ne):
        cnt = _dot_nt(ones8, maskb[e])
        cbr_scr[e] = _dot(cnt.astype(BF16), utb)
    cb_ref[...] = cbr_scr[:, 0, :].astype(I32)

    s_col = lax.broadcasted_iota(I32, (cap, 1), 0).astype(F32)
    lane_b = lax.broadcasted_iota(I32, (cap, nbk), 1).astype(F32)
    lane_l = lax.broadcasted_iota(I32, (cap, LANES), 1).astype(F32)
    lane_e = lax.broadcasted_iota(I32, (cap, LANES), 1)

    def per_expert(e, _):
        cbrow = cbr_scr[e][0:1, :]
        le = (cbrow <= s_col).astype(BF16)
        bs = _dot(le, cnt_tab[e])
        blk_c, start_c = bs[:, 0:LANES], bs[:, LANES:]
        oh = (lane_b == blk_c[:, 0:nbk]).astype(BF16)
        vals = _dot(oh, val_tab[e])
        le2 = (vals[:, 0:LANES] <= s_col - start_c).astype(BF16)
        within_c = _dot(le2, ones)
        mine = lane_e == e
        idx_ref[...] = jnp.where(mine, (blk_c * LANES + within_c).astype(I32), idx_ref[...])
        aff_row = (vals[:, LANES:2 * LANES] + vals[:, 2 * LANES:3 * LANES]) + vals[:, 3 * LANES:]
        g_col = jnp.sum(jnp.where(lane_l == within_c, aff_row, 0.0), axis=1, keepdims=True)
        gate_ref[...] = jnp.where(mine, g_col, gate_ref[...])
        return 0

    idx_ref[...] = jnp.zeros_like(idx_ref)
    gate_ref[...] = jnp.zeros_like(gate_ref)

    lax.fori_loop(0, ne, per_expert, 0)


def _route(logits_t, cap):
    ne, t = logits_t.shape
    nbk = t // LANES
    lg3 = logits_t.reshape(ne, nbk, LANES)
    full = lambda shape: pl.BlockSpec(shape, lambda i: (0,) * len(shape))
    idx, gate, cb, pos = pl.pallas_call(
        functools.partial(_route_kernel, nbk=nbk, cap=cap),
        grid=(1,),
        in_specs=[full((ne, nbk, LANES))],
        out_specs=[full((cap, LANES)), full((cap, LANES)), full((ne, nbk)), full((ne, nbk, LANES))],
        out_shape=[jax.ShapeDtypeStruct((cap, LANES), I32),
                   jax.ShapeDtypeStruct((cap, LANES), F32),
                   jax.ShapeDtypeStruct((ne, nbk), I32),
                   jax.ShapeDtypeStruct((ne, nbk, LANES), I32)],
        scratch_shapes=[pltpu.VMEM((ne, nbk, 2 * LANES), BF16), pltpu.VMEM((ne, nbk, 4 * LANES), BF16),
                        pltpu.VMEM((ne, 8, nbk), F32)],
        compiler_params=_cparams(("arbitrary",)),
        name="route",
    )(lg3)
    return jnp.transpose(idx[:, :ne]), jnp.transpose(gate[:, :ne]), cb, pos


def _ffn_kernel(idx_ref, idxn_ref, gate_ref, x_hbm, x2d_hbm, mg_ref, wg_hbm, wu_hbm, wd_hbm, ye_ref,
                xbuf, wst, wbf, sem, wsem, *, layer, ne, nsteps, rb):
    e = pl.program_id(0)
    step = e * pl.num_programs(1) + pl.program_id(1)
    slot = lax.rem(step, 2)

    def weight_copies(ex):
        return [pltpu.make_async_copy(w_hbm.at[layer, ex], wst.at[m], wsem.at[m])
                for m, w_hbm in enumerate((wg_hbm, wu_hbm, wd_hbm))]

    def issue(ids_ref, s):
        for r in range(rb):
            pltpu.make_async_copy(x_hbm.at[ids_ref[0, 0, r]], xbuf.at[s, pl.ds(r * SUB, SUB)],
                                  sem.at[s]).start()

    def wait(s):
        pltpu.make_async_copy(x2d_hbm.at[pl.ds(0, rb * SUB)], xbuf.at[s], sem.at[s]).wait()

    @pl.when(step == 0)
    def _():
        for cp in weight_copies(0):
            cp.start(priority=1)
        issue(idx_ref, 0)

    @pl.when(pl.program_id(1) == 0)
    def _():
        for m, cp in enumerate(weight_copies(e)):
            cp.wait()
            wbf[m] = wst[m].astype(BF16)

        @pl.when(e + 1 < ne)
        def _():
            for cp in weight_copies(e + 1):
                cp.start(priority=1)

    issue(idxn_ref, 1 - slot)
    wait(slot)

    xe = _rms(_load_token_tiles(xbuf, rb, (slot,)), mg_ref[...]).astype(BF16)
    hid = (_silu(_dot(xe, wbf[0])) * _dot(xe, wbf[1])).astype(BF16)
    ye = _dot(hid, wbf[2])
    ri = lax.broadcasted_iota(I32, (LANES, LANES), 0)
    ci = lax.broadcasted_iota(I32, (LANES, LANES), 1)
    gate = gate_ref[0]
    gcol = jnp.concatenate(
        [jnp.sum(jnp.where(ri == ci, gate[:, q:q + LANES], 0.0), axis=1, keepdims=True)
         for q in range(0, rb, LANES)], axis=0)
    ye_ref[...] = (ye * gcol).astype(BF16)

    @pl.when(step == nsteps - 1)
    def _():
        wait(1 - slot)


def _ffn(idx, gate, x1t, mg, wg, wu, wd, layer):
    ne, cap = idx.shape
    t = x1t.shape[0] // SUB
    rb = min(RB, cap)
    nblk = cap // rb
    nsteps = ne * nblk
    idx3 = idx.reshape(nsteps, 1, rb)
    gate3 = gate.reshape(nsteps, 1, rb)
    hbm = pl.BlockSpec(memory_space=pl.ANY)
    return pl.pallas_call(
        functools.partial(_ffn_kernel, layer=layer, ne=ne, nsteps=nsteps, rb=rb),
        grid=(ne, nblk),
        in_specs=[
            pl.BlockSpec((1, 1, rb), lambda e, j: (e * nblk + j, 0, 0), memory_space=pltpu.SMEM),
            pl.BlockSpec((1, 1, rb), lambda e, j: (jnp.minimum(e * nblk + j + 1, nsteps - 1), 0, 0),
                         memory_space=pltpu.SMEM),
            pl.BlockSpec((1, 1, rb), lambda e, j: (e * nblk + j, 0, 0)),
            hbm, hbm,
            pl.BlockSpec((1, D_MODEL), lambda e, j: (0, 0)),
            hbm, hbm, hbm,
        ],
        out_specs=pl.BlockSpec((rb, D_MODEL), lambda e, j: (e * nblk + j, 0)),
        out_shape=jax.ShapeDtypeStruct((ne * cap, D_MODEL), BF16),
        scratch_shapes=[pltpu.VMEM((2, rb * SUB, LANES), F32),
                        pltpu.VMEM((3, D_MODEL, D_MODEL), F32),
                        pltpu.VMEM((3, D_MODEL, D_MODEL), BF16),
                        pltpu.SemaphoreType.DMA((2,)), pltpu.SemaphoreType.DMA((3,))],
        compiler_params=_cparams(("arbitrary", "arbitrary")),
        name="ffn",
    )(idx3, idx3, gate3, x1t.reshape(t, SUB, LANES), x1t, mg, wg, wu, wd)


CHUNK_ROWS = 16
MAIN_CHUNKS = 96
MAIN_ROWS = CHUNK_ROWS * MAIN_CHUNKS
MAX_CHUNKS = -(-(N_EXPERTS * TB // CHUNK_ROWS + 2 * N_EXPERTS) // MAIN_CHUNKS) * MAIN_CHUNKS
NO_SLOT = 1 << 30


def _combine_kernel(cb_ref, x1t_ref, pos_ref, ye_hbm, *rest, nbk, cap, ntb, final):
    if final:
        g_ref, out_ref, ybuf, acc_scr, sem, ie_scr, ir_scr, ic_scr, n_scr = rest
    else:
        out_ref, ybuf, acc_scr, sem, ie_scr, ir_scr, ic_scr, n_scr = rest
    tb = pl.program_id(0)
    s = lax.rem(tb, 2)
    bpt = TB // LANES
    shift = CHUNK_ROWS.bit_length() - 1

    def build(tbx, par):
        base = par * MAX_CHUNKS
        n = jnp.int32(0)
        for e in range(N_EXPERTS):
            b0 = e * nbk + tbx * bpt
            lo = jnp.where(tbx == 0, 0, cb_ref[jnp.maximum(b0 - 1, 0)])
            hi = cb_ref[b0 + bpt - 1]
            start = lax.shift_left(lax.shift_right_logical(lo, shift), shift)
            nch = jnp.where(hi > lo,
                            lax.shift_right_logical(hi - start + (CHUNK_ROWS - 1), shift), 0)

            def push(c, k, e=e, start=start):
                ie_scr[base + k] = e
                ir_scr[base + k] = e * cap + start + c * CHUNK_ROWS
                ic_scr[base + k] = start + c * CHUNK_ROWS
                return k + 1

            n = lax.fori_loop(0, nch, push, n)
        nsg = jnp.maximum(lax.div(n + (MAIN_CHUNKS - 1), MAIN_CHUNKS), 1)

        def pad(k, _):
            ie_scr[base + k] = 0
            ir_scr[base + k] = 0
            ic_scr[base + k] = NO_SLOT
            return 0

        lax.fori_loop(n, nsg * MAIN_CHUNKS, pad, 0)
        n_scr[par] = n
        return n

    def chunk_copy(par, k, k0):
        base = par * MAX_CHUNKS
        row0 = pl.multiple_of(ir_scr[base + k], CHUNK_ROWS)
        dst0 = pl.multiple_of((k - k0) * CHUNK_ROWS, CHUNK_ROWS)
        return pltpu.make_async_copy(ye_hbm.at[pl.ds(row0, CHUNK_ROWS)],
                                     ybuf.at[par, pl.ds(dst0, CHUNK_ROWS)], sem.at[par])

    def issue_main(par):
        for k in range(MAIN_CHUNKS):
            chunk_copy(par, k, 0).start()

    def wait_main(par):
        pltpu.make_async_copy(ye_hbm.at[pl.ds(0, MAIN_ROWS)], ybuf.at[par], sem.at[par]).wait()

    def issue(par, k0, k1):
        def body(k, _):
            chunk_copy(par, k, k0).start()
            return 0
        lax.fori_loop(k0, k1, body, 0)

    def drain(par, k0, k1):
        def body(k, _):
            chunk_copy(par, k, k0).wait()
            return 0
        lax.fori_loop(k0, k1, body, 0)

    @pl.when(tb == 0)
    def _():
        build(0, 0)
        issue_main(0)

    @pl.when(tb + 1 < ntb)
    def _():
        build(tb + 1, 1 - s)
        issue_main(1 - s)

    jrow = lax.broadcasted_iota(I32, (CHUNK_ROWS, TB), 0)

    def scatter(sg):
        base = s * MAX_CHUNKS + sg * MAIN_CHUNKS
        pieces = []
        for c in range(MAIN_CHUNKS):
            prow = pos_ref[pl.ds(ie_scr[base + c], 1), :]
            pieces.append((prow == jrow + ic_scr[base + c]).astype(BF16))
        onehot_t = jnp.concatenate(pieces, axis=0)
        return lax.dot_general(onehot_t, ybuf[s], (((0,), (0,)), ((), ())),
                               preferred_element_type=F32)

    def finish(x2):
        return _rms(x2, g_ref[...]) if final else x2

    n = n_scr[s]
    wait_main(s)
    nsg = lax.div(n + (MAIN_CHUNKS - 1), MAIN_CHUNKS)

    @pl.when(nsg <= 1)
    def _():
        out_ref[...] = finish(_load_token_tiles(x1t_ref, TB) + scatter(0))

    @pl.when(nsg > 1)
    def _():
        acc_scr[...] = _load_token_tiles(x1t_ref, TB) + scatter(0)

        def extra(sg, _):
            k0 = sg * MAIN_CHUNKS
            k1 = jnp.minimum(n, k0 + MAIN_CHUNKS)
            issue(s, k0, k1)
            drain(s, k0, k1)
            acc_scr[...] += scatter(sg)
            return 0

        lax.fori_loop(1, nsg, extra, 0)
        out_ref[...] = finish(acc_scr[...])


def _combine(cb, pos, x1t, ye, g_final=None):
    t = x1t.shape[0] // SUB
    ne, nbk = cb.shape
    cap = ye.shape[0] // ne
    final = g_final is not None
    blk = pl.BlockSpec((TB, D_MODEL), lambda i, cb: (i, 0))
    in_specs = [pl.BlockSpec((TB * SUB, LANES), lambda i, cb: (i, 0)),
                pl.BlockSpec((ne, TB), lambda i, cb: (0, i)), pl.BlockSpec(memory_space=pl.ANY)]
    args = [cb.reshape(-1), x1t, pos.reshape(ne, t), ye]
    if final:
        in_specs.append(pl.BlockSpec((1, D_MODEL), lambda i, cb: (0, 0)))
        args.append(g_final)
    grid_spec = pltpu.PrefetchScalarGridSpec(
        num_scalar_prefetch=1,
        grid=(t // TB,),
        in_specs=in_specs,
        out_specs=blk,
        scratch_shapes=[pltpu.VMEM((2, MAIN_ROWS, D_MODEL), BF16),
                        pltpu.VMEM((TB, D_MODEL), F32),
                        pltpu.SemaphoreType.DMA((2,)),
                        pltpu.SMEM((2 * MAX_CHUNKS,), I32), pltpu.SMEM((2 * MAX_CHUNKS,), I32),
                        pltpu.SMEM((2 * MAX_CHUNKS,), I32), pltpu.SMEM((2,), I32)],
    )
    return pl.pallas_call(
        functools.partial(_combine_kernel, nbk=nbk, cap=cap, ntb=t // TB, final=final),
        grid_spec=grid_spec,
        out_shape=jax.ShapeDtypeStruct((t, D_MODEL), F32),
        compiler_params=_cparams(("arbitrary",)),
        name="combine_final" if final else "combine",
    )(*args)


def _dft_mats(n, scale):
    k = np.arange(n)
    ang = 2.0 * np.pi * ((k[:, None] * k[None, :]) % n) / n
    return (jnp.asarray(np.cos(ang) * scale, F32).astype(BF16),
            jnp.asarray(np.sin(ang) * scale, F32).astype(BF16))


def _lane_slab_specs(block, index_map):
    return [pl.BlockSpec(block + (LANES,), functools.partial(lambda *i, c: index_map(*i) + (c,), c=c))
            for c in range(LANE_SLABS)]


def _every_8th_row(slabs, j, rows):
    parts = [slabs[c, pl.ds(j, rows, stride=SUB), :] if not isinstance(slabs, (list, tuple))
             else slabs[c][pl.ds(j, rows, stride=SUB), :] for c in range(LANE_SLABS)]
    return jnp.concatenate(parts, axis=1)


def _fft1_kernel(*refs):
    xs = refs[:LANE_SLABS]
    g_ref, cs1_ref, twc_ref, tws_ref, br_ref, bi_ref, x_scr = refs[LANE_SLABS:]
    n1 = cs1_ref.shape[1]
    for c in range(LANE_SLABS):
        x_scr[c] = xs[c][...].reshape(n1 * SUB, LANES)
    for j in range(SUB):
        xj = _rms(_every_8th_row(x_scr, j, n1), g_ref[...]).astype(BF16)
        a = _dot(cs1_ref[...], xj)
        ar, ai = a[:n1], a[n1:]
        c = twc_ref[0, :, j:j + 1]
        s = tws_ref[0, :, j:j + 1]
        br_ref[:, j] = (ar * c + ai * s).reshape(n1 // SUB, SUB, D_MODEL)
        bi_ref[:, j] = (ai * c - ar * s).reshape(n1 // SUB, SUB, D_MODEL)


def _fft1(x, g, nb, n1):
    t = x.shape[0]
    n = n1 * FFT_N2
    c1, s1 = _dft_mats(n1, 1.0 / math.sqrt(n1))
    cs1 = jnp.concatenate([c1, -s1], axis=0)
    k1 = np.arange(n1)[:, None]
    n2 = np.arange(FFT_N2)[None, :]
    ang = 2.0 * np.pi * ((k1 * n2) % n) / n
    nj = FFT_N2 // SUB
    nkc = n1 // SUB
    to_tab = lambda a: jnp.asarray(a.reshape(n1, nj, SUB).transpose(1, 0, 2), F32)
    twc, tws = to_tab(np.cos(ang)), to_tab(np.sin(ang))
    xv = x.reshape(nb * n1, FFT_N2, D_MODEL)
    out_blk = pl.BlockSpec((nkc, SUB, SUB, D_MODEL), lambda b, j: (b, j, 0, 0))
    out_shape = jax.ShapeDtypeStruct((nb * nkc, FFT_N2, SUB, D_MODEL), F32)
    tws_spec = pl.BlockSpec((1, n1, SUB), lambda b, j: (j, 0, 0))
    mat = pl.BlockSpec((2 * n1, n1), lambda b, j: (0, 0))
    br, bi = pl.pallas_call(
        _fft1_kernel,
        grid=(nb, nj),
        in_specs=_lane_slab_specs((n1, SUB), lambda b, j: (b, j))
        + [pl.BlockSpec((1, D_MODEL), lambda b, j: (0, 0)), mat, tws_spec, tws_spec],
        out_specs=[out_blk, out_blk],
        out_shape=[out_shape, out_shape],
        scratch_shapes=[pltpu.VMEM((LANE_SLABS, n1 * SUB, LANES), F32)],
        compiler_params=_cparams(("arbitrary", "arbitrary")),
        name="fft1",
    )(*([xv] * LANE_SLABS), g, cs1, twc, tws)
    return br.reshape(t, D_MODEL), bi.reshape(t, D_MODEL)


FFT_PAIR = 2


def _fft2_kernel(*refs):
    brs, bis, xs = (refs[i * LANE_SLABS:(i + 1) * LANE_SLABS] for i in range(3))
    (w2r_ref, w2i_ref, cc_ref, sc_ref, wout_ref, mg_ref, wrh_ref, wrl_ref,
     x1t_ref, lg_ref, f_scr, x_scr) = refs[3 * LANE_SLABS:]
    n2 = FFT_N2
    for c in range(LANE_SLABS):
        x_scr[c] = xs[c][...].reshape(n2 * SUB, LANES)
    for jp in range(SUB // FFT_PAIR):
        for q in range(FFT_PAIR):
            j = jp * FFT_PAIR + q
            bj = jnp.concatenate([_every_8th_row(brs, j, n2).astype(BF16),
                                  _every_8th_row(bis, j, n2).astype(BF16)], axis=0)
            yr = _dot(w2r_ref[...], bj).astype(BF16)
            yi = _dot(w2i_ref[...], bj).astype(BF16)
            for g in range(C_GROUPS):
                cols = slice(g * C_GROUP_DIM, (g + 1) * C_GROUP_DIM)
                f = _dot(yr[:, cols], cc_ref[...]) + _dot(yi[:, cols], sc_ref[...])
                f_scr[q * n2:(q + 1) * n2, cols] = f.astype(BF16)
        y = _dot(f_scr[...], wout_ref[...])
        for q in range(FFT_PAIR):
            j = jp * FFT_PAIR + q
            x1 = _every_8th_row(x_scr, j, n2) + y[q * n2:(q + 1) * n2, :]
            x1t_ref[:, j * SUB:(j + 1) * SUB, :] = x1.reshape(n2, SUB, LANES)
            lg_ref[:, j * n2:(j + 1) * n2] = _router_logits(x1, mg_ref, wrh_ref, wrl_ref)


def _fft2(br, bi, x, nb, n1, wout, mg, wr):
    t = x.shape[0]
    n2 = FFT_N2
    c2, s2 = _dft_mats(n2, 1.0 / math.sqrt(n2))
    w2r = jnp.concatenate([c2, s2], axis=1)
    w2i = jnp.concatenate([-s2, c2], axis=1)
    cc, sc = _dft_mats(C_GROUP_DIM, 1.0 / math.sqrt(C_GROUP_DIM))
    nkc = n1 // SUB
    rows = SUB * n2
    xv = x.reshape(nb * n2, n1, D_MODEL)
    full = lambda shape: pl.BlockSpec(shape, lambda b, k: (0,) * len(shape))
    b_slabs = _lane_slab_specs((rows,), lambda b, k: (b * nkc + k,))
    x_slabs = _lane_slab_specs((n2, SUB), lambda b, k: (b, k))
    x1, lg = pl.pallas_call(
        _fft2_kernel,
        grid=(nb, nkc),
        in_specs=b_slabs * 2 + x_slabs
        + [full((n2, 2 * n2)), full((n2, 2 * n2)),
           full((C_GROUP_DIM, C_GROUP_DIM)), full((C_GROUP_DIM, C_GROUP_DIM)),
           full((D_MODEL, D_MODEL)), full((1, D_MODEL)),
           full((N_EXPERTS, D_MODEL)), full((N_EXPERTS, D_MODEL))],
        out_specs=[pl.BlockSpec((n2, SUB * SUB, LANES), lambda b, k: (b, k, 0)),
                   pl.BlockSpec((N_EXPERTS, rows), lambda b, k: (0, b * nkc + k))],
        out_shape=[jax.ShapeDtypeStruct((nb * n2, n1 * SUB, LANES), F32),
                   jax.ShapeDtypeStruct((N_EXPERTS, t), F32)],
        scratch_shapes=[pltpu.VMEM((FFT_PAIR * n2, D_MODEL), BF16),
                        pltpu.VMEM((LANE_SLABS, rows, LANES), F32)],
        compiler_params=_cparams(("arbitrary", "arbitrary")),
        name="fft2",
    )(*([br] * LANE_SLABS), *([bi] * LANE_SLABS), *([xv] * LANE_SLABS),
      w2r, w2i, cc, sc, wout, mg, *wr)
    lg = lg.reshape(N_EXPERTS, nb, n1, n2).transpose(0, 1, 3, 2).reshape(N_EXPERTS, t)
    return x1.reshape(t * SUB, LANES), lg


def _moe(x1, logits_t, mg, p, layer, g_final=None):
    t = x1.shape[0] // SUB
    cap = CAPACITY_FACTOR * t // N_EXPERTS
    idx, gate, cb, pos = _route(logits_t, cap)
    ye = _ffn(idx, gate, x1, mg, p["wg"], p["wu"], p["wd"], layer)
    return _combine(cb, pos, x1, ye, g_final)


def _trunk(x3, p):
    nb, s, _ = x3.shape
    x = x3.reshape(nb * s, D_MODEL)
    row = lambda v: v.reshape(1, -1)
    outa, glu = _front(x, row(p["mix_norm"][0]), p["w_in"], row(p["ab_v_norm_g"][0]),
                       row(p["ab_v_norm_b"][0]), p["w_spatial"], p["b_spatial"])
    mg0, mg1 = row(p["moe_norm"][0]), row(p["moe_norm"][1])
    x1, lg = _back(x, outa, glu, p["ab_conv_w"][0], row(p["ab_conv_b"][0]),
                   row(p["ab_conv_norm_g"][0]), row(p["ab_conv_norm_b"][0]), p["ab_w_out"],
                   mg0, p["router"][0], s)
    x2 = _moe(x1, lg, mg0, p, 0)
    n1 = s // FFT_N2
    br, bi = _fft1(x2, row(p["mix_norm"][1]), nb, n1)
    x1, lg = _fft2(br, bi, x2, nb, n1, p["c_w_out"], mg1, p["router"][1])
    y = _moe(x1, lg, mg1, p, 1, row(p["final_norm"]))
    return y.reshape(nb, s, D_MODEL)


def kernel(x_prompt, x_sample, mix_norm, ab_w_in, ab_v_norm_g, ab_v_norm_b, ab_w_spatial,
           ab_b_spatial, ab_conv_w, ab_conv_b, ab_conv_norm_g, ab_conv_norm_b, ab_w_out,
           c_w_out, moe_norm, moe_router, moe_w_gate, moe_w_up, moe_w_down, final_norm):
    p = dict(
        mix_norm=mix_norm, moe_norm=moe_norm, final_norm=final_norm,
        w_in=ab_w_in[0].astype(BF16),
        ab_v_norm_g=ab_v_norm_g, ab_v_norm_b=ab_v_norm_b,
        w_spatial=ab_w_spatial[0].astype(BF16),
        b_spatial=jnp.repeat(jnp.transpose(ab_b_spatial[0]), A_HEAD_DIM, axis=1),
        ab_conv_w=ab_conv_w, ab_conv_b=ab_conv_b,
        ab_conv_norm_g=ab_conv_norm_g, ab_conv_norm_b=ab_conv_norm_b,
        ab_w_out=ab_w_out[0].astype(BF16),
        c_w_out=c_w_out[0].astype(BF16),
        router=[_split2(jnp.transpose(moe_router[l])) for l in range(moe_router.shape[0])],
        wg=moe_w_gate, wu=moe_w_up, wd=moe_w_down,
    )
    return (_trunk(x_prompt, p), _trunk(x_sample, p))
```

```python
import functools
import math

import numpy as np
import jax
import jax.numpy as jnp
from jax import lax
from jax.experimental import pallas as pl
from jax.experimental.pallas import tpu as pltpu

F32, BF16, I32 = jnp.float32, jnp.bfloat16, jnp.int32

D_MODEL = 1024
D_A = 512
D_B = 512
A_GROUPS = 4
A_HEAD_DIM = 128
CHUNK = 128
CONV_WIDTH = 31
CONV_PAD = 15
C_GROUPS = 4
C_GROUP_DIM = 256
N_EXPERTS = 16
CAPACITY_FACTOR = 2
RMS_EPS = 1e-6
LN_EPS = 1e-5

LANES = 128
SUB = 8
LANE_SLABS = D_MODEL // LANES
HALO = 16
TM = 512
RB = 512
TB = 512
FFT_N2 = 128
VMEM_LIMIT = 48 * 1024 * 1024
BISECT_BITS_STEPS = 31
BISECT_VALUE_STEPS = 30


def _cparams(sem):
    return pltpu.CompilerParams(dimension_semantics=sem, vmem_limit_bytes=VMEM_LIMIT)


def _dot(a, b):
    return jnp.dot(a, b, preferred_element_type=F32)


def _dot_nt(a, b, precision=None):
    return lax.dot_general(a, b, (((1,), (1,)), ((), ())), precision=precision,
                           preferred_element_type=F32)


def _rms(x, g):
    return x * lax.rsqrt(jnp.mean(x * x, axis=-1, keepdims=True) + RMS_EPS) * g


def _ln(x, g, b):
    mu = jnp.mean(x, axis=-1, keepdims=True)
    xc = x - mu
    var = jnp.mean(xc * xc, axis=-1, keepdims=True)
    return xc * lax.rsqrt(var + LN_EPS) * g + b


def _gelu(x):
    return 0.5 * x * (1.0 + lax.erf(x * np.float32(math.sqrt(0.5))))


def _silu(x):
    return x * jax.nn.sigmoid(x)


def _front_kernel(x_ref, g_ref, win_ref, vg_ref, vb_ref, wsp_ref, bsp_ref,
                  outa_ref, glu_ref, zu_scr, vn_scr):
    xn = _rms(x_ref[...], g_ref[...]).astype(BF16)
    zu_scr[...] = _gelu(_dot(xn, win_ref[:, 0:D_A]))
    zv = _gelu(_dot(xn, win_ref[:, D_A:2 * D_A]))
    vn_scr[...] = _ln(zv, vg_ref[...], vb_ref[...]).astype(BF16)
    a_in = _dot(xn, win_ref[:, 2 * D_A:2 * D_A + D_B])
    gate = _dot(xn, win_ref[:, 2 * D_A + D_B:])
    glu_ref[...] = a_in * jax.nn.sigmoid(gate)
    for c in range(TM // CHUNK):
        rows = slice(c * CHUNK, (c + 1) * CHUNK)
        for h in range(A_GROUPS):
            cols = slice(h * A_HEAD_DIM, (h + 1) * A_HEAD_DIM)
            mixed = _dot(wsp_ref[h], vn_scr[rows, cols]) + bsp_ref[:, cols]
            outa_ref[rows, cols] = (zu_scr[rows, cols] * mixed).astype(BF16)


def _front(x, g, w_in, vg, vb, wsp, bsp):
    t = x.shape[0]
    full = lambda shape: pl.BlockSpec(shape, lambda i: (0,) * len(shape))
    return pl.pallas_call(
        _front_kernel,
        grid=(t // TM,),
        in_specs=[
            pl.BlockSpec((TM, D_MODEL), lambda i: (i, 0)),
            full((1, D_MODEL)),
            full((D_MODEL, 2 * D_A + 2 * D_B)),
            full((1, D_A)), full((1, D_A)),
            full((A_GROUPS, CHUNK, CHUNK)),
            full((CHUNK, D_A)),
        ],
        out_specs=[pl.BlockSpec((TM, D_A), lambda i: (i, 0)),
                   pl.BlockSpec((TM, D_B), lambda i: (i, 0))],
        out_shape=[jax.ShapeDtypeStruct((t, D_A), BF16),
                   jax.ShapeDtypeStruct((t, D_B), F32)],
        scratch_shapes=[pltpu.VMEM((TM, D_A), F32), pltpu.VMEM((TM, D_A), BF16)],
        compiler_params=_cparams(("arbitrary",)),
        name="front",
    )(x, g, w_in, vg, vb, wsp, bsp)


CONV_ROWS = 64
SUBLANES = 8
SHIFT_ROWS = TM + (HALO + CONV_PAD) // SUBLANES * SUBLANES


def _split2(a):
    hi = a.astype(BF16)
    return hi, (a - hi.astype(F32)).astype(BF16)


def _store_token_tiles(tiles_ref, x):
    rows = x.shape[0]
    for c in range(LANE_SLABS):
        tiles_ref[pl.ds(c, rows, stride=SUB), :] = x[:, c * LANES:(c + 1) * LANES]


def _load_token_tiles(tiles_ref, rows, lead=()):
    parts = [tiles_ref[lead + (pl.ds(c, rows, stride=SUB), slice(None))] for c in range(LANE_SLABS)]
    return jnp.concatenate(parts, axis=1)


def _router_logits(x1, mg_ref, wrh_ref, wrl_ref):
    xh, xl = _split2(_rms(x1, mg_ref[...]))
    return _dot_nt(wrh_ref[...], xh) + (_dot_nt(wrh_ref[...], xl) + _dot_nt(wrl_ref[...], xh))


def _back_kernel(x_ref, outa_ref, glu_ref, prev_ref, next_ref, cw_ref, cb_ref, ng_ref, nb_ref,
                 wout_ref, mg_ref, wrh_ref, wrl_ref, x1t_ref, lg_ref, ext_scr, sh_scr, ob_scr,
                 *, blocks_per_seq):
    i = pl.program_id(0)
    pos = lax.rem(i, blocks_per_seq)
    ext_scr[0:HALO, :] = jnp.where(pos == 0, 0.0, prev_ref[...])
    ext_scr[HALO:HALO + TM, :] = glu_ref[...]
    ext_scr[HALO + TM:, :] = jnp.where(pos == blocks_per_seq - 1, 0.0, next_ref[...])
    for b in range(SUBLANES):
        sh_scr[b] = ext_scr[b:b + SHIFT_ROWS, :]
    for r in range(TM // CONV_ROWS):
        acc = jnp.zeros((CONV_ROWS, D_B), F32) + cb_ref[...]
        for k in range(CONV_WIDTH):
            a, b = divmod(HALO - CONV_PAD + k, SUBLANES)
            row0 = r * CONV_ROWS + a * SUBLANES
            acc = acc + cw_ref[k:k + 1, :] * sh_scr[b, row0:row0 + CONV_ROWS, :]
        ob = _silu(_ln(acc, ng_ref[...], nb_ref[...]))
        ob_scr[r * CONV_ROWS:(r + 1) * CONV_ROWS, :] = ob.astype(BF16)
    y = _dot(outa_ref[...], wout_ref[0:D_A, :]) + _dot(ob_scr[...], wout_ref[D_A:, :])
    x1 = x_ref[...] + y
    _store_token_tiles(x1t_ref, x1)
    lg_ref[...] = _router_logits(x1, mg_ref, wrh_ref, wrl_ref)


def _back(x, outa, glu, cw, cb, ng, nb_, wout, mg, wr, seq_len):
    t = x.shape[0]
    full = lambda shape: pl.BlockSpec(shape, lambda i: (0,) * len(shape))
    hb = TM // HALO
    return pl.pallas_call(
        functools.partial(_back_kernel, blocks_per_seq=seq_len // TM),
        grid=(t // TM,),
        in_specs=[
            pl.BlockSpec((TM, D_MODEL), lambda i: (i, 0)),
            pl.BlockSpec((TM, D_A), lambda i: (i, 0)),
            pl.BlockSpec((TM, D_B), lambda i: (i, 0)),
            pl.BlockSpec((HALO, D_B), lambda i: (jnp.maximum(i * hb - 1, 0), 0)),
            pl.BlockSpec((HALO, D_B), lambda i: (jnp.minimum((i + 1) * hb, t // HALO - 1), 0)),
            full((CONV_WIDTH, D_B)), full((1, D_B)), full((1, D_B)), full((1, D_B)),
            full((D_A + D_B, D_MODEL)),
            full((1, D_MODEL)),
            full((N_EXPERTS, D_MODEL)), full((N_EXPERTS, D_MODEL)),
        ],
        out_specs=[pl.BlockSpec((TM * SUB, LANES), lambda i: (i, 0)),
                   pl.BlockSpec((N_EXPERTS, TM), lambda i: (0, i))],
        out_shape=[jax.ShapeDtypeStruct((t * SUB, LANES), F32),
                   jax.ShapeDtypeStruct((N_EXPERTS, t), F32)],
        scratch_shapes=[pltpu.VMEM((TM + 2 * HALO, D_B), F32),
                        pltpu.VMEM((SUBLANES, SHIFT_ROWS, D_B), F32),
                        pltpu.VMEM((TM, D_B), BF16)],
        compiler_params=_cparams(("arbitrary",)),
        name="back",
    )(x, outa, glu, glu, glu, cw, cb, ng, nb_, wout, mg, *wr)


def _split3(a):
    a1 = a.astype(BF16)
    r1 = a - a1.astype(F32)
    a2 = r1.astype(BF16)
    a3 = (r1 - a2.astype(F32)).astype(BF16)
    return a1, a2, a3


def _route_kernel(lg_ref, idx_ref, gate_ref, cb_ref, pos_ref, cnt_tab, val_tab, cbr_scr, *, nbk, cap):
    ne = N_EXPERTS
    lg = lg_ref[...]
    ex = jnp.exp(lg - jnp.max(lg, axis=0, keepdims=True))
    aff = ex / jnp.sum(ex, axis=0, keepdims=True)

    def count(m):
        s = jnp.sum(m.astype(F32), axis=1, keepdims=True)
        return jnp.sum(s, axis=2, keepdims=True)

    def update(mid, lo, hi):
        ge = count(aff >= mid) >= cap
        return jnp.where(ge, mid, lo), jnp.where(ge, hi, mid)

    def bits_body(_, c):
        lo, hi = c
        lo_i = lax.bitcast_convert_type(lo, I32)
        hi_i = lax.bitcast_convert_type(hi, I32)
        mid = lax.bitcast_convert_type(lo_i + lax.shift_right_logical(hi_i - lo_i, 1), F32)
        return update(mid, lo, hi)

    def value_body(_, c):
        lo, hi = c
        return update(lo + (hi - lo) * 0.5, lo, hi)

    lo0 = jnp.zeros((ne, 1, 1), F32)
    hi0 = jnp.full((ne, 1, 1), jnp.inf, F32)
    c = lax.fori_loop(0, BISECT_BITS_STEPS, bits_body, (lo0, hi0))
    lo, hi = lax.fori_loop(0, BISECT_VALUE_STEPS, value_body, c)

    li = lax.broadcasted_iota(I32, (LANES, LANES), 0)
    lj = lax.broadcasted_iota(I32, (LANES, LANES), 1)
    ut = (li <= lj).astype(BF16)
    ones = jnp.ones((LANES, LANES), BF16)
    bi = lax.broadcasted_iota(I32, (nbk, nbk), 0)
    bj = lax.broadcasted_iota(I32, (nbk, nbk), 1)
    slt = (bj < bi).astype(BF16)
    utb = (bi <= bj).astype(BF16)
    ones8 = jnp.ones((8, LANES), BF16)

    def block_cum(m):
        m2 = m.astype(BF16).reshape(ne * nbk, LANES)
        loc = _dot(m2, ut).reshape(ne, nbk, LANES)
        tot = _dot(m2, ones).astype(BF16).reshape(ne, nbk, LANES)
        off = jnp.stack([_dot(slt, tot[e]) for e in range(ne)])
        return loc, tot, off

    gt = aff >= hi
    eq = (aff >= lo) & (aff < hi)
    need = np.float32(cap) - count(gt)
    eloc, _, eoff = block_cum(eq)
    mask = gt | (eq & (eloc + eoff <= need))

    mloc, mtot, moff = block_cum(mask)
    pos_ref[...] = jnp.where(mask, mloc + moff - 1.0, -1.0).astype(I32)
    cnt_tab[:, :, 0:LANES] = jnp.ones((ne, nbk, LANES), BF16)
    cnt_tab[:, :, LANES:] = mtot
    val_tab[:, :, 0:LANES] = mloc.astype(BF16)
    for k, piece in enumerate(_split3(aff)):
        val_tab[:, :, (k + 1) * LANES:(k + 2) * LANES] = piece
    maskb = mask.astype(BF16)
    for e in range(ne):
        cnt = _dot_nt(ones8, maskb[e])
        cbr_scr[e] = _dot(cnt.astype(BF16), utb)
    cb_ref[...] = cbr_scr[:, 0, :].astype(I32)

    s_col = lax.broadcasted_iota(I32, (cap, 1), 0).astype(F32)
    lane_b = lax.broadcasted_iota(I32, (cap, nbk), 1).astype(F32)
    lane_l = lax.broadcasted_iota(I32, (cap, LANES), 1).astype(F32)
    lane_e = lax.broadcasted_iota(I32, (cap, LANES), 1)

    def per_expert(e, _):
        cbrow = cbr_scr[e][0:1, :]
        le = (cbrow <= s_col).astype(BF16)
        bs = _dot(le, cnt_tab[e])
        blk_c, start_c = bs[:, 0:LANES], bs[:, LANES:]
        oh = (lane_b == blk_c[:, 0:nbk]).astype(BF16)
        vals = _dot(oh, val_tab[e])
        le2 = (vals[:, 0:LANES] <= s_col - start_c).astype(BF16)
        within_c = _dot(le2, ones)
        mine = lane_e == e
        idx_ref[...] = jnp.where(mine, (blk_c * LANES + within_c).astype(I32), idx_ref[...])
        aff_row = (vals[:, LANES:2 * LANES] + vals[:, 2 * LANES:3 * LANES]) + vals[:, 3 * LANES:]
        g_col = jnp.sum(jnp.where(lane_l == within_c, aff_row, 0.0), axis=1, keepdims=True)
        gate_ref[...] = jnp.where(mine, g_col, gate_ref[...])
        return 0

    idx_ref[...] = jnp.zeros_like(idx_ref)
    gate_ref[...] = jnp.zeros_like(gate_ref)

    lax.fori_loop(0, ne, per_expert, 0)


def _route(logits_t, cap):
    ne, t = logits_t.shape
    nbk = t // LANES
    lg3 = logits_t.reshape(ne, nbk, LANES)
    full = lambda shape: pl.BlockSpec(shape, lambda i: (0,) * len(shape))
    idx, gate, cb, pos = pl.pallas_call(
        functools.partial(_route_kernel, nbk=nbk, cap=cap),
        grid=(1,),
        in_specs=[full((ne, nbk, LANES))],
        out_specs=[full((cap, LANES)), full((cap, LANES)), full((ne, nbk)), full((ne, nbk, LANES))],
        out_shape=[jax.ShapeDtypeStruct((cap, LANES), I32),
                   jax.ShapeDtypeStruct((cap, LANES), F32),
                   jax.ShapeDtypeStruct((ne, nbk), I32),
                   jax.ShapeDtypeStruct((ne, nbk, LANES), I32)],
        scratch_shapes=[pltpu.VMEM((ne, nbk, 2 * LANES), BF16), pltpu.VMEM((ne, nbk, 4 * LANES), BF16),
                        pltpu.VMEM((ne, 8, nbk), F32)],
        compiler_params=_cparams(("arbitrary",)),
        name="route",
    )(lg3)
    return jnp.transpose(idx[:, :ne]), jnp.transpose(gate[:, :ne]), cb, pos


def _ffn_kernel(idx_ref, idxn_ref, gate_ref, x_hbm, x2d_hbm, mg_ref, wg_hbm, wu_hbm, wd_hbm, ye_ref,
                xbuf, wst, wbf, sem, wsem, *, layer, ne, nsteps, rb):
    e = pl.program_id(0)
    step = e * pl.num_programs(1) + pl.program_id(1)
    slot = lax.rem(step, 2)

    def weight_copies(ex):
        return [pltpu.make_async_copy(w_hbm.at[layer, ex], wst.at[m], wsem.at[m])
                for m, w_hbm in enumerate((wg_hbm, wu_hbm, wd_hbm))]

    def issue(ids_ref, s):
        for r in range(rb):
            pltpu.make_async_copy(x_hbm.at[ids_ref[0, 0, r]], xbuf.at[s, pl.ds(r * SUB, SUB)],
                                  sem.at[s]).start()

    def wait(s):
        pltpu.make_async_copy(x2d_hbm.at[pl.ds(0, rb * SUB)], xbuf.at[s], sem.at[s]).wait()

    @pl.when(step == 0)
    def _():
        for cp in weight_copies(0):
            cp.start(priority=1)
        issue(idx_ref, 0)

    @pl.when(pl.program_id(1) == 0)
    def _():
        for m, cp in enumerate(weight_copies(e)):
            cp.wait()
            wbf[m] = wst[m].astype(BF16)

        @pl.when(e + 1 < ne)
        def _():
            for cp in weight_copies(e + 1):
                cp.start(priority=1)

    issue(idxn_ref, 1 - slot)
    wait(slot)

    xe = _rms(_load_token_tiles(xbuf, rb, (slot,)), mg_ref[...]).astype(BF16)
    hid = (_silu(_dot(xe, wbf[0])) * _dot(xe, wbf[1])).astype(BF16)
    ye = _dot(hid, wbf[2])
    ri = lax.broadcasted_iota(I32, (LANES, LANES), 0)
    ci = lax.broadcasted_iota(I32, (LANES, LANES), 1)
    gate = gate_ref[0]
    gcol = jnp.concatenate(
        [jnp.sum(jnp.where(ri == ci, gate[:, q:q + LANES], 0.0), axis=1, keepdims=True)
         for q in range(0, rb, LANES)], axis=0)
    ye_ref[...] = (ye * gcol).astype(BF16)

    @pl.when(step == nsteps - 1)
    def _():
        wait(1 - slot)


def _ffn(idx, gate, x1t, mg, wg, wu, wd, layer):
    ne, cap = idx.shape
    t = x1t.shape[0] // SUB
    rb = min(RB, cap)
    nblk = cap // rb
    nsteps = ne * nblk
    idx3 = idx.reshape(nsteps, 1, rb)
    gate3 = gate.reshape(nsteps, 1, rb)
    hbm = pl.BlockSpec(memory_space=pl.ANY)
    return pl.pallas_call(
        functools.partial(_ffn_kernel, layer=layer, ne=ne, nsteps=nsteps, rb=rb),
        grid=(ne, nblk),
        in_specs=[
            pl.BlockSpec((1, 1, rb), lambda e, j: (e * nblk + j, 0, 0), memory_space=pltpu.SMEM),
            pl.BlockSpec((1, 1, rb), lambda e, j: (jnp.minimum(e * nblk + j + 1, nsteps - 1), 0, 0),
                         memory_space=pltpu.SMEM),
            pl.BlockSpec((1, 1, rb), lambda e, j: (e * nblk + j, 0, 0)),
            hbm, hbm,
            pl.BlockSpec((1, D_MODEL), lambda e, j: (0, 0)),
            hbm, hbm, hbm,
        ],
        out_specs=pl.BlockSpec((rb, D_MODEL), lambda e, j: (e * nblk + j, 0)),
        out_shape=jax.ShapeDtypeStruct((ne * cap, D_MODEL), BF16),
        scratch_shapes=[pltpu.VMEM((2, rb * SUB, LANES), F32),
                        pltpu.VMEM((3, D_MODEL, D_MODEL), F32),
                        pltpu.VMEM((3, D_MODEL, D_MODEL), BF16),
                        pltpu.SemaphoreType.DMA((2,)), pltpu.SemaphoreType.DMA((3,))],
        compiler_params=_cparams(("arbitrary", "arbitrary")),
        name="ffn",
    )(idx3, idx3, gate3, x1t.reshape(t, SUB, LANES), x1t, mg, wg, wu, wd)


CHUNK_ROWS = 16
MAIN_CHUNKS = 96
MAIN_ROWS = CHUNK_ROWS * MAIN_CHUNKS
MAX_CHUNKS = -(-(N_EXPERTS * TB // CHUNK_ROWS + 2 * N_EXPERTS) // MAIN_CHUNKS) * MAIN_CHUNKS
NO_SLOT = 1 << 30


def _combine_kernel(cb_ref, x1t_ref, pos_ref, ye_hbm, *rest, nbk, cap, ntb, final):
    if final:
        g_ref, out_ref, ybuf, acc_scr, sem, ie_scr, ir_scr, ic_scr, n_scr = rest
    else:
        out_ref, ybuf, acc_scr, sem, ie_scr, ir_scr, ic_scr, n_scr = rest
    tb = pl.program_id(0)
    s = lax.rem(tb, 2)
    bpt = TB // LANES
    shift = CHUNK_ROWS.bit_length() - 1

    def build(tbx, par):
        base = par * MAX_CHUNKS
        n = jnp.int32(0)
        for e in range(N_EXPERTS):
            b0 = e * nbk + tbx * bpt
            lo = jnp.where(tbx == 0, 0, cb_ref[jnp.maximum(b0 - 1, 0)])
            hi = cb_ref[b0 + bpt - 1]
            start = lax.shift_left(lax.shift_right_logical(lo, shift), shift)
            nch = jnp.where(hi > lo,
                            lax.shift_right_logical(hi - start + (CHUNK_ROWS - 1), shift), 0)

            def push(c, k, e=e, start=start):
                ie_scr[base + k] = e
                ir_scr[base + k] = e * cap + start + c * CHUNK_ROWS
                ic_scr[base + k] = start + c * CHUNK_ROWS
                return k + 1

            n = lax.fori_loop(0, nch, push, n)
        nsg = jnp.maximum(lax.div(n + (MAIN_CHUNKS - 1), MAIN_CHUNKS), 1)

        def pad(k, _):
            ie_scr[base + k] = 0
            ir_scr[base + k] = 0
            ic_scr[base + k] = NO_SLOT
            return 0

        lax.fori_loop(n, nsg * MAIN_CHUNKS, pad, 0)
        n_scr[par] = n
        return n

    def chunk_copy(par, k, k0):
        base = par * MAX_CHUNKS
        row0 = pl.multiple_of(ir_scr[base + k], CHUNK_ROWS)
        dst0 = pl.multiple_of((k - k0) * CHUNK_ROWS, CHUNK_ROWS)
        return pltpu.make_async_copy(ye_hbm.at[pl.ds(row0, CHUNK_ROWS)],
                                     ybuf.at[par, pl.ds(dst0, CHUNK_ROWS)], sem.at[par])

    def issue_main(par):
        for k in range(MAIN_CHUNKS):
            chunk_copy(par, k, 0).start()

    def wait_main(par):
        pltpu.make_async_copy(ye_hbm.at[pl.ds(0, MAIN_ROWS)], ybuf.at[par], sem.at[par]).wait()

    def issue(par, k0, k1):
        def body(k, _):
            chunk_copy(par, k, k0).start()
            return 0
        lax.fori_loop(k0, k1, body, 0)

    def drain(par, k0, k1):
        def body(k, _):
            chunk_copy(par, k, k0).wait()
            return 0
        lax.fori_loop(k0, k1, body, 0)

    @pl.when(tb == 0)
    def _():
        build(0, 0)
        issue_main(0)

    @pl.when(tb + 1 < ntb)
    def _():
        build(tb + 1, 1 - s)
        issue_main(1 - s)

    jrow = lax.broadcasted_iota(I32, (CHUNK_ROWS, TB), 0)

    def scatter(sg):
        base = s * MAX_CHUNKS + sg * MAIN_CHUNKS
        pieces = []
        for c in range(MAIN_CHUNKS):
            prow = pos_ref[pl.ds(ie_scr[base + c], 1), :]
            pieces.append((prow == jrow + ic_scr[base + c]).astype(BF16))
        onehot_t = jnp.concatenate(pieces, axis=0)
        return lax.dot_general(onehot_t, ybuf[s], (((0,), (0,)), ((), ())),
                               preferred_element_type=F32)

    def finish(x2):
        return _rms(x2, g_ref[...]) if final else x2

    n = n_scr[s]
    wait_main(s)
    nsg = lax.div(n + (MAIN_CHUNKS - 1), MAIN_CHUNKS)

    @pl.when(nsg <= 1)
    def _():
        out_ref[...] = finish(_load_token_tiles(x1t_ref, TB) + scatter(0))

    @pl.when(nsg > 1)
    def _():
        acc_scr[...] = _load_token_tiles(x1t_ref, TB) + scatter(0)

        def extra(sg, _):
            k0 = sg * MAIN_CHUNKS
            k1 = jnp.minimum(n, k0 + MAIN_CHUNKS)
            issue(s, k0, k1)
            drain(s, k0, k1)
            acc_scr[...] += scatter(sg)
            return 0

        lax.fori_loop(1, nsg, extra, 0)
        out_ref[...] = finish(acc_scr[...])


def _combine(cb, pos, x1t, ye, g_final=None):
    t = x1t.shape[0] // SUB
    ne, nbk = cb.shape
    cap = ye.shape[0] // ne
    final = g_final is not None
    blk = pl.BlockSpec((TB, D_MODEL), lambda i, cb: (i, 0))
    in_specs = [pl.BlockSpec((TB * SUB, LANES), lambda i, cb: (i, 0)),
                pl.BlockSpec((ne, TB), lambda i, cb: (0, i)), pl.BlockSpec(memory_space=pl.ANY)]
    args = [cb.reshape(-1), x1t, pos.reshape(ne, t), ye]
    if final:
        in_specs.append(pl.BlockSpec((1, D_MODEL), lambda i, cb: (0, 0)))
        args.append(g_final)
    grid_spec = pltpu.PrefetchScalarGridSpec(
        num_scalar_prefetch=1,
        grid=(t // TB,),
        in_specs=in_specs,
        out_specs=blk,
        scratch_shapes=[pltpu.VMEM((2, MAIN_ROWS, D_MODEL), BF16),
                        pltpu.VMEM((TB, D_MODEL), F32),
                        pltpu.SemaphoreType.DMA((2,)),
                        pltpu.SMEM((2 * MAX_CHUNKS,), I32), pltpu.SMEM((2 * MAX_CHUNKS,), I32),
                        pltpu.SMEM((2 * MAX_CHUNKS,), I32), pltpu.SMEM((2,), I32)],
    )
    return pl.pallas_call(
        functools.partial(_combine_kernel, nbk=nbk, cap=cap, ntb=t // TB, final=final),
        grid_spec=grid_spec,
        out_shape=jax.ShapeDtypeStruct((t, D_MODEL), F32),
        compiler_params=_cparams(("arbitrary",)),
        name="combine_final" if final else "combine",
    )(*args)


def _dft_mats(n, scale):
    k = np.arange(n)
    ang = 2.0 * np.pi * ((k[:, None] * k[None, :]) % n) / n
    return (jnp.asarray(np.cos(ang) * scale, F32).astype(BF16),
            jnp.asarray(np.sin(ang) * scale, F32).astype(BF16))


def _lane_slab_specs(block, index_map):
    return [pl.BlockSpec(block + (LANES,), functools.partial(lambda *i, c: index_map(*i) + (c,), c=c))
            for c in range(LANE_SLABS)]


def _every_8th_row(slabs, j, rows):
    parts = [slabs[c, pl.ds(j, rows, stride=SUB), :] if not isinstance(slabs, (list, tuple))
             else slabs[c][pl.ds(j, rows, stride=SUB), :] for c in range(LANE_SLABS)]
    return jnp.concatenate(parts, axis=1)


def _fft1_kernel(*refs):
    xs = refs[:LANE_SLABS]
    g_ref, cs1_ref, twc_ref, tws_ref, br_ref, bi_ref, x_scr = refs[LANE_SLABS:]
    n1 = cs1_ref.shape[1]
    for c in range(LANE_SLABS):
        x_scr[c] = xs[c][...].reshape(n1 * SUB, LANES)
    for j in range(SUB):
        xj = _rms(_every_8th_row(x_scr, j, n1), g_ref[...]).astype(BF16)
        a = _dot(cs1_ref[...], xj)
        ar, ai = a[:n1], a[n1:]
        c = twc_ref[0, :, j:j + 1]
        s = tws_ref[0, :, j:j + 1]
        br_ref[:, j] = (ar * c + ai * s).reshape(n1 // SUB, SUB, D_MODEL)
        bi_ref[:, j] = (ai * c - ar * s).reshape(n1 // SUB, SUB, D_MODEL)


def _fft1(x, g, nb, n1):
    t = x.shape[0]
    n = n1 * FFT_N2
    c1, s1 = _dft_mats(n1, 1.0 / math.sqrt(n1))
    cs1 = jnp.concatenate([c1, -s1], axis=0)
    k1 = np.arange(n1)[:, None]
    n2 = np.arange(FFT_N2)[None, :]
    ang = 2.0 * np.pi * ((k1 * n2) % n) / n
    nj = FFT_N2 // SUB
    nkc = n1 // SUB
    to_tab = lambda a: jnp.asarray(a.reshape(n1, nj, SUB).transpose(1, 0, 2), F32)
    twc, tws = to_tab(np.cos(ang)), to_tab(np.sin(ang))
    xv = x.reshape(nb * n1, FFT_N2, D_MODEL)
    out_blk = pl.BlockSpec((nkc, SUB, SUB, D_MODEL), lambda b, j: (b, j, 0, 0))
    out_shape = jax.ShapeDtypeStruct((nb * nkc, FFT_N2, SUB, D_MODEL), F32)
    tws_spec = pl.BlockSpec((1, n1, SUB), lambda b, j: (j, 0, 0))
    mat = pl.BlockSpec((2 * n1, n1), lambda b, j: (0, 0))
    br, bi = pl.pallas_call(
        _fft1_kernel,
        grid=(nb, nj),
        in_specs=_lane_slab_specs((n1, SUB), lambda b, j: (b, j))
        + [pl.BlockSpec((1, D_MODEL), lambda b, j: (0, 0)), mat, tws_spec, tws_spec],
        out_specs=[out_blk, out_blk],
        out_shape=[out_shape, out_shape],
        scratch_shapes=[pltpu.VMEM((LANE_SLABS, n1 * SUB, LANES), F32)],
        compiler_params=_cparams(("arbitrary", "arbitrary")),
        name="fft1",
    )(*([xv] * LANE_SLABS), g, cs1, twc, tws)
    return br.reshape(t, D_MODEL), bi.reshape(t, D_MODEL)


FFT_GROUP = 4


def _fft2_kernel(*refs):
    brs, bis, xs = (refs[i * LANE_SLABS:(i + 1) * LANE_SLABS] for i in range(3))
    (w2r_ref, w2i_ref, cc_ref, sc_ref, wout_ref, mg_ref, wrh_ref, wrl_ref,
     x1t_ref, lg_ref, f_scr, x_scr) = refs[3 * LANE_SLABS:]
    n2 = FFT_N2
    for c in range(LANE_SLABS):
        x_scr[c] = xs[c][...].reshape(n2 * SUB, LANES)
    for jp in range(SUB // FFT_GROUP):
        for q in range(FFT_GROUP):
            j = jp * FFT_GROUP + q
            bj = jnp.concatenate([_every_8th_row(brs, j, n2).astype(BF16),
                                  _every_8th_row(bis, j, n2).astype(BF16)], axis=0)
            yr = _dot(w2r_ref[...], bj).astype(BF16)
            yi = _dot(w2i_ref[...], bj).astype(BF16)
            for g in range(C_GROUPS):
                cols = slice(g * C_GROUP_DIM, (g + 1) * C_GROUP_DIM)
                f = _dot(yr[:, cols], cc_ref[...]) + _dot(yi[:, cols], sc_ref[...])
                f_scr[q * n2:(q + 1) * n2, cols] = f.astype(BF16)
        y = _dot(f_scr[...], wout_ref[...])
        for q in range(FFT_GROUP):
            j = jp * FFT_GROUP + q
            x1 = _every_8th_row(x_scr, j, n2) + y[q * n2:(q + 1) * n2, :]
            x1t_ref[:, j * SUB:(j + 1) * SUB, :] = x1.reshape(n2, SUB, LANES)
            lg_ref[:, j * n2:(j + 1) * n2] = _router_logits(x1, mg_ref, wrh_ref, wrl_ref)


def _fft2(br, bi, x, nb, n1, wout, mg, wr):
    t = x.shape[0]
    n2 = FFT_N2
    c2, s2 = _dft_mats(n2, 1.0 / math.sqrt(n2))
    w2r = jnp.concatenate([c2, s2], axis=1)
    w2i = jnp.concatenate([-s2, c2], axis=1)
    cc, sc = _dft_mats(C_GROUP_DIM, 1.0 / math.sqrt(C_GROUP_DIM))
    nkc = n1 // SUB
    rows = SUB * n2
    xv = x.reshape(nb * n2, n1, D_MODEL)
    full = lambda shape: pl.BlockSpec(shape, lambda b, k: (0,) * len(shape))
    b_slabs = _lane_slab_specs((rows,), lambda b, k: (b * nkc + k,))
    x_slabs = _lane_slab_specs((n2, SUB), lambda b, k: (b, k))
    x1, lg = pl.pallas_call(
        _fft2_kernel,
        grid=(nb, nkc),
        in_specs=b_slabs * 2 + x_slabs
        + [full((n2, 2 * n2)), full((n2, 2 * n2)),
           full((C_GROUP_DIM, C_GROUP_DIM)), full((C_GROUP_DIM, C_GROUP_DIM)),
           full((D_MODEL, D_MODEL)), full((1, D_MODEL)),
           full((N_EXPERTS, D_MODEL)), full((N_EXPERTS, D_MODEL))],
        out_specs=[pl.BlockSpec((n2, SUB * SUB, LANES), lambda b, k: (b, k, 0)),
                   pl.BlockSpec((N_EXPERTS, rows), lambda b, k: (0, b * nkc + k))],
        out_shape=[jax.ShapeDtypeStruct((nb * n2, n1 * SUB, LANES), F32),
                   jax.ShapeDtypeStruct((N_EXPERTS, t), F32)],
        scratch_shapes=[pltpu.VMEM((FFT_GROUP * n2, D_MODEL), BF16),
                        pltpu.VMEM((LANE_SLABS, rows, LANES), F32)],
        compiler_params=_cparams(("arbitrary", "arbitrary")),
        name="fft2",
    )(*([br] * LANE_SLABS), *([bi] * LANE_SLABS), *([xv] * LANE_SLABS),
      w2r, w2i, cc, sc, wout, mg, *wr)
    lg = lg.reshape(N_EXPERTS, nb, n1, n2).transpose(0, 1, 3, 2).reshape(N_EXPERTS, t)
    return x1.reshape(t * SUB, LANES), lg


def _moe(x1, logits_t, mg, p, layer, g_final=None):
    t = x1.shape[0] // SUB
    cap = CAPACITY_FACTOR * t // N_EXPERTS
    idx, gate, cb, pos = _route(logits_t, cap)
    ye = _ffn(idx, gate, x1, mg, p["wg"], p["wu"], p["wd"], layer)
    return _combine(cb, pos, x1, ye, g_final)


def _trunk(x3, p):
    nb, s, _ = x3.shape
    x = x3.reshape(nb * s, D_MODEL)
    row = lambda v: v.reshape(1, -1)
    outa, glu = _front(x, row(p["mix_norm"][0]), p["w_in"], row(p["ab_v_norm_g"][0]),
                       row(p["ab_v_norm_b"][0]), p["w_spatial"], p["b_spatial"])
    mg0, mg1 = row(p["moe_norm"][0]), row(p["moe_norm"][1])
    x1, lg = _back(x, outa, glu, p["ab_conv_w"][0], row(p["ab_conv_b"][0]),
                   row(p["ab_conv_norm_g"][0]), row(p["ab_conv_norm_b"][0]), p["ab_w_out"],
                   mg0, p["router"][0], s)
    x2 = _moe(x1, lg, mg0, p, 0)
    n1 = s // FFT_N2
    br, bi = _fft1(x2, row(p["mix_norm"][1]), nb, n1)
    x1, lg = _fft2(br, bi, x2, nb, n1, p["c_w_out"], mg1, p["router"][1])
    y = _moe(x1, lg, mg1, p, 1, row(p["final_norm"]))
    return y.reshape(nb, s, D_MODEL)


def kernel(x_prompt, x_sample, mix_norm, ab_w_in, ab_v_norm_g, ab_v_norm_b, ab_w_spatial,
           ab_b_spatial, ab_conv_w, ab_conv_b, ab_conv_norm_g, ab_conv_norm_b, ab_w_out,
           c_w_out, moe_norm, moe_router, moe_w_gate, moe_w_up, moe_w_down, final_norm):
    p = dict(
        mix_norm=mix_norm, moe_norm=moe_norm, final_norm=final_norm,
        w_in=ab_w_in[0].astype(BF16),
        ab_v_norm_g=ab_v_norm_g, ab_v_norm_b=ab_v_norm_b,
        w_spatial=ab_w_spatial[0].astype(BF16),
        b_spatial=jnp.repeat(jnp.transpose(ab_b_spatial[0]), A_HEAD_DIM, axis=1),
        ab_conv_w=ab_conv_w, ab_conv_b=ab_conv_b,
        ab_conv_norm_g=ab_conv_norm_g, ab_conv_norm_b=ab_conv_norm_b,
        ab_w_out=ab_w_out[0].astype(BF16),
        c_w_out=c_w_out[0].astype(BF16),
        router=[_split2(jnp.transpose(moe_router[l])) for l in range(moe_router.shape[0])],
        wg=moe_w_gate, wu=moe_w_up, wd=moe_w_down,
    )
    return (_trunk(x_prompt, p), _trunk(x_sample, p))
```

```python
import functools
import math

import numpy as np
import jax
import jax.numpy as jnp
from jax import lax
from jax.experimental import pallas as pl
from jax.experimental.pallas import tpu as pltpu

F32, BF16, I32 = jnp.float32, jnp.bfloat16, jnp.int32

D_MODEL = 1024
D_A = 512
D_B = 512
A_GROUPS = 4
A_HEAD_DIM = 128
CHUNK = 128
CONV_WIDTH = 31
CONV_PAD = 15
C_GROUPS = 4
C_GROUP_DIM = 256
N_EXPERTS = 16
CAPACITY_FACTOR = 2
RMS_EPS = 1e-6
LN_EPS = 1e-5

LANES = 128
SUB = 8
LANE_SLABS = D_MODEL // LANES
HALO = 16
TM = 512
RB = 512
TB = 512
FFT_N2 = 128
VMEM_LIMIT = 48 * 1024 * 1024
BISECT_BITS_STEPS = 31
BISECT_VALUE_STEPS = 30


def _cparams(sem):
    return pltpu.CompilerParams(dimension_semantics=sem, vmem_limit_bytes=VMEM_LIMIT)


def _dot(a, b):
    return jnp.dot(a, b, preferred_element_type=F32)


def _dot_nt(a, b, precision=None):
    return lax.dot_general(a, b, (((1,), (1,)), ((), ())), precision=precision,
                           preferred_element_type=F32)


def _rms(x, g):
    return x * lax.rsqrt(jnp.mean(x * x, axis=-1, keepdims=True) + RMS_EPS) * g


def _ln(x, g, b):
    mu = jnp.mean(x, axis=-1, keepdims=True)
    xc = x - mu
    var = jnp.mean(xc * xc, axis=-1, keepdims=True)
    return xc * lax.rsqrt(var + LN_EPS) * g + b


def _gelu(x):
    return 0.5 * x * (1.0 + lax.erf(x * np.float32(math.sqrt(0.5))))


def _silu(x):
    return x * jax.nn.sigmoid(x)


def _front_kernel(x_ref, g_ref, win_ref, vg_ref, vb_ref, wsp_ref, bsp_ref,
                  outa_ref, glu_ref, zu_scr, vn_scr):
    xn = _rms(x_ref[...], g_ref[...]).astype(BF16)
    zu_scr[...] = _gelu(_dot(xn, win_ref[:, 0:D_A]))
    zv = _gelu(_dot(xn, win_ref[:, D_A:2 * D_A]))
    vn_scr[...] = _ln(zv, vg_ref[...], vb_ref[...]).astype(BF16)
    a_in = _dot(xn, win_ref[:, 2 * D_A:2 * D_A + D_B])
    gate = _dot(xn, win_ref[:, 2 * D_A + D_B:])
    glu_ref[...] = a_in * jax.nn.sigmoid(gate)
    for c in range(TM // CHUNK):
        rows = slice(c * CHUNK, (c + 1) * CHUNK)
        for h in range(A_GROUPS):
            cols = slice(h * A_HEAD_DIM, (h + 1) * A_HEAD_DIM)
            mixed = _dot(wsp_ref[h], vn_scr[rows, cols]) + bsp_ref[:, cols]
            outa_ref[rows, cols] = (zu_scr[rows, cols] * mixed).astype(BF16)


def _front(x, g, w_in, vg, vb, wsp, bsp):
    t = x.shape[0]
    full = lambda shape: pl.BlockSpec(shape, lambda i: (0,) * len(shape))
    return pl.pallas_call(
        _front_kernel,
        grid=(t // TM,),
        in_specs=[
            pl.BlockSpec((TM, D_MODEL), lambda i: (i, 0)),
            full((1, D_MODEL)),
            full((D_MODEL, 2 * D_A + 2 * D_B)),
            full((1, D_A)), full((1, D_A)),
            full((A_GROUPS, CHUNK, CHUNK)),
            full((CHUNK, D_A)),
        ],
        out_specs=[pl.BlockSpec((TM, D_A), lambda i: (i, 0)),
                   pl.BlockSpec((TM, D_B), lambda i: (i, 0))],
        out_shape=[jax.ShapeDtypeStruct((t, D_A), BF16),
                   jax.ShapeDtypeStruct((t, D_B), F32)],
        scratch_shapes=[pltpu.VMEM((TM, D_A), F32), pltpu.VMEM((TM, D_A), BF16)],
        compiler_params=_cparams(("arbitrary",)),
        name="front",
    )(x, g, w_in, vg, vb, wsp, bsp)


CONV_ROWS = 256
SUBLANES = 8
SHIFT_ROWS = TM + (HALO + CONV_PAD) // SUBLANES * SUBLANES


def _split2(a):
    hi = a.astype(BF16)
    return hi, (a - hi.astype(F32)).astype(BF16)


def _store_token_tiles(tiles_ref, x):
    rows = x.shape[0]
    for c in range(LANE_SLABS):
        tiles_ref[pl.ds(c, rows, stride=SUB), :] = x[:, c * LANES:(c + 1) * LANES]


def _load_token_tiles(tiles_ref, rows, lead=()):
    parts = [tiles_ref[lead + (pl.ds(c, rows, stride=SUB), slice(None))] for c in range(LANE_SLABS)]
    return jnp.concatenate(parts, axis=1)


def _router_logits(x1, mg_ref, wrh_ref, wrl_ref):
    xh, xl = _split2(_rms(x1, mg_ref[...]))
    return _dot_nt(wrh_ref[...], xh) + (_dot_nt(wrh_ref[...], xl) + _dot_nt(wrl_ref[...], xh))


def _back_kernel(x_ref, outa_ref, glu_ref, prev_ref, next_ref, cw_ref, cb_ref, ng_ref, nb_ref,
                 wout_ref, mg_ref, wrh_ref, wrl_ref, x1t_ref, lg_ref, ext_scr, sh_scr, ob_scr,
                 *, blocks_per_seq):
    i = pl.program_id(0)
    pos = lax.rem(i, blocks_per_seq)
    ext_scr[0:HALO, :] = jnp.where(pos == 0, 0.0, prev_ref[...])
    ext_scr[HALO:HALO + TM, :] = glu_ref[...]
    ext_scr[HALO + TM:, :] = jnp.where(pos == blocks_per_seq - 1, 0.0, next_ref[...])
    for b in range(SUBLANES):
        sh_scr[b] = ext_scr[b:b + SHIFT_ROWS, :]
    for r in range(TM // CONV_ROWS):
        acc = jnp.zeros((CONV_ROWS, D_B), F32) + cb_ref[...]
        for k in range(CONV_WIDTH):
            a, b = divmod(HALO - CONV_PAD + k, SUBLANES)
            row0 = r * CONV_ROWS + a * SUBLANES
            acc = acc + cw_ref[k:k + 1, :] * sh_scr[b, row0:row0 + CONV_ROWS, :]
        ob = _silu(_ln(acc, ng_ref[...], nb_ref[...]))
        ob_scr[r * CONV_ROWS:(r + 1) * CONV_ROWS, :] = ob.astype(BF16)
    y = _dot(outa_ref[...], wout_ref[0:D_A, :]) + _dot(ob_scr[...], wout_ref[D_A:, :])
    x1 = x_ref[...] + y
    _store_token_tiles(x1t_ref, x1)
    lg_ref[...] = _router_logits(x1, mg_ref, wrh_ref, wrl_ref)


def _back(x, outa, glu, cw, cb, ng, nb_, wout, mg, wr, seq_len):
    t = x.shape[0]
    full = lambda shape: pl.BlockSpec(shape, lambda i: (0,) * len(shape))
    hb = TM // HALO
    return pl.pallas_call(
        functools.partial(_back_kernel, blocks_per_seq=seq_len // TM),
        grid=(t // TM,),
        in_specs=[
            pl.BlockSpec((TM, D_MODEL), lambda i: (i, 0)),
            pl.BlockSpec((TM, D_A), lambda i: (i, 0)),
            pl.BlockSpec((TM, D_B), lambda i: (i, 0)),
            pl.BlockSpec((HALO, D_B), lambda i: (jnp.maximum(i * hb - 1, 0), 0)),
            pl.BlockSpec((HALO, D_B), lambda i: (jnp.minimum((i + 1) * hb, t // HALO - 1), 0)),
            full((CONV_WIDTH, D_B)), full((1, D_B)), full((1, D_B)), full((1, D_B)),
            full((D_A + D_B, D_MODEL)),
            full((1, D_MODEL)),
            full((N_EXPERTS, D_MODEL)), full((N_EXPERTS, D_MODEL)),
        ],
        out_specs=[pl.BlockSpec((TM * SUB, LANES), lambda i: (i, 0)),
                   pl.BlockSpec((N_EXPERTS, TM), lambda i: (0, i))],
        out_shape=[jax.ShapeDtypeStruct((t * SUB, LANES), F32),
                   jax.ShapeDtypeStruct((N_EXPERTS, t), F32)],
        scratch_shapes=[pltpu.VMEM((TM + 2 * HALO, D_B), F32),
                        pltpu.VMEM((SUBLANES, SHIFT_ROWS, D_B), F32),
                        pltpu.VMEM((TM, D_B), BF16)],
        compiler_params=_cparams(("arbitrary",)),
        name="back",
    )(x, outa, glu, glu, glu, cw, cb, ng, nb_, wout, mg, *wr)


def _split3(a):
    a1 = a.astype(BF16)
    r1 = a - a1.astype(F32)
    a2 = r1.astype(BF16)
    a3 = (r1 - a2.astype(F32)).astype(BF16)
    return a1, a2, a3


def _route_kernel(lg_ref, idx_ref, gate_ref, cb_ref, pos_ref, cnt_tab, val_tab, cbr_scr, *, nbk, cap):
    ne = N_EXPERTS
    lg = lg_ref[...]
    ex = jnp.exp(lg - jnp.max(lg, axis=0, keepdims=True))
    aff = ex / jnp.sum(ex, axis=0, keepdims=True)

    def count(m):
        s = jnp.sum(m.astype(F32), axis=1, keepdims=True)
        return jnp.sum(s, axis=2, keepdims=True)

    def update(mid, lo, hi):
        ge = count(aff >= mid) >= cap
        return jnp.where(ge, mid, lo), jnp.where(ge, hi, mid)

    def bits_body(_, c):
        lo, hi = c
        lo_i = lax.bitcast_convert_type(lo, I32)
        hi_i = lax.bitcast_convert_type(hi, I32)
        mid = lax.bitcast_convert_type(lo_i + lax.shift_right_logical(hi_i - lo_i, 1), F32)
        return update(mid, lo, hi)

    def value_body(_, c):
        lo, hi = c
        return update(lo + (hi - lo) * 0.5, lo, hi)

    lo0 = jnp.zeros((ne, 1, 1), F32)
    hi0 = jnp.full((ne, 1, 1), jnp.inf, F32)
    c = lax.fori_loop(0, BISECT_BITS_STEPS, bits_body, (lo0, hi0))
    lo, hi = lax.fori_loop(0, BISECT_VALUE_STEPS, value_body, c)

    li = lax.broadcasted_iota(I32, (LANES, LANES), 0)
    lj = lax.broadcasted_iota(I32, (LANES, LANES), 1)
    ut = (li <= lj).astype(BF16)
    ones = jnp.ones((LANES, LANES), BF16)
    bi = lax.broadcasted_iota(I32, (nbk, nbk), 0)
    bj = lax.broadcasted_iota(I32, (nbk, nbk), 1)
    slt = (bj < bi).astype(BF16)
    utb = (bi <= bj).astype(BF16)
    ones8 = jnp.ones((8, LANES), BF16)

    def block_cum(m):
        m2 = m.astype(BF16).reshape(ne * nbk, LANES)
        loc = _dot(m2, ut).reshape(ne, nbk, LANES)
        tot = _dot(m2, ones).astype(BF16).reshape(ne, nbk, LANES)
        off = jnp.stack([_dot(slt, tot[e]) for e in range(ne)])
        return loc, tot, off

    gt = aff >= hi
    eq = (aff >= lo) & (aff < hi)
    need = np.float32(cap) - count(gt)
    eloc, _, eoff = block_cum(eq)
    mask = gt | (eq & (eloc + eoff <= need))

    mloc, mtot, moff = block_cum(mask)
    pos_ref[...] = jnp.where(mask, mloc + moff - 1.0, -1.0).astype(I32)
    cnt_tab[:, :, 0:LANES] = jnp.ones((ne, nbk, LANES), BF16)
    cnt_tab[:, :, LANES:] = mtot
    val_tab[:, :, 0:LANES] = mloc.astype(BF16)
    for k, piece in enumerate(_split3(aff)):
        val_tab[:, :, (k + 1) * LANES:(k + 2) * LANES] = piece
    maskb = mask.astype(BF16)
    for e in range(ne):
        cnt = _dot_nt(ones8, maskb[e])
        cbr_scr[e] = _dot(cnt.astype(BF16), utb)
    cb_ref[...] = cbr_scr[:, 0, :].astype(I32)

    s_col = lax.broadcasted_iota(I32, (cap, 1), 0).astype(F32)
    lane_b = lax.broadcasted_iota(I32, (cap, nbk), 1).astype(F32)
    lane_l = lax.broadcasted_iota(I32, (cap, LANES), 1).astype(F32)
    lane_e = lax.broadcasted_iota(I32, (cap, LANES), 1)

    def per_expert(e, _):
        cbrow = cbr_scr[e][0:1, :]
        le = (cbrow <= s_col).astype(BF16)
        bs = _dot(le, cnt_tab[e])
        blk_c, start_c = bs[:, 0:LANES], bs[:, LANES:]
        oh = (lane_b == blk_c[:, 0:nbk]).astype(BF16)
        vals = _dot(oh, val_tab[e])
        le2 = (vals[:, 0:LANES] <= s_col - start_c).astype(BF16)
        within_c = _dot(le2, ones)
        mine = lane_e == e
        idx_ref[...] = jnp.where(mine, (blk_c * LANES + within_c).astype(I32), idx_ref[...])
        aff_row = (vals[:, LANES:2 * LANES] + vals[:, 2 * LANES:3 * LANES]) + vals[:, 3 * LANES:]
        g_col = jnp.sum(jnp.where(lane_l == within_c, aff_row, 0.0), axis=1, keepdims=True)
        gate_ref[...] = jnp.where(mine, g_col, gate_ref[...])
        return 0

    idx_ref[...] = jnp.zeros_like(idx_ref)
    gate_ref[...] = jnp.zeros_like(gate_ref)

    lax.fori_loop(0, ne, per_expert, 0)


def _route(logits_t, cap):
    ne, t = logits_t.shape
    nbk = t // LANES
    lg3 = logits_t.reshape(ne, nbk, LANES)
    full = lambda shape: pl.BlockSpec(shape, lambda i: (0,) * len(shape))
    idx, gate, cb, pos = pl.pallas_call(
        functools.partial(_route_kernel, nbk=nbk, cap=cap),
        grid=(1,),
        in_specs=[full((ne, nbk, LANES))],
        out_specs=[full((cap, LANES)), full((cap, LANES)), full((ne, nbk)), full((ne, nbk, LANES))],
        out_shape=[jax.ShapeDtypeStruct((cap, LANES), I32),
                   jax.ShapeDtypeStruct((cap, LANES), F32),
                   jax.ShapeDtypeStruct((ne, nbk), I32),
                   jax.ShapeDtypeStruct((ne, nbk, LANES), I32)],
        scratch_shapes=[pltpu.VMEM((ne, nbk, 2 * LANES), BF16), pltpu.VMEM((ne, nbk, 4 * LANES), BF16),
                        pltpu.VMEM((ne, 8, nbk), F32)],
        compiler_params=_cparams(("arbitrary",)),
        name="route",
    )(lg3)
    return jnp.transpose(idx[:, :ne]), jnp.transpose(gate[:, :ne]), cb, pos


def _ffn_kernel(idx_ref, idxn_ref, gate_ref, x_hbm, x2d_hbm, mg_ref, wg_hbm, wu_hbm, wd_hbm, ye_ref,
                xbuf, wst, wbf, sem, wsem, *, layer, ne, nsteps, rb):
    e = pl.program_id(0)
    step = e * pl.num_programs(1) + pl.program_id(1)
    slot = lax.rem(step, 2)

    def weight_copies(ex):
        return [pltpu.make_async_copy(w_hbm.at[layer, ex], wst.at[m], wsem.at[m])
                for m, w_hbm in enumerate((wg_hbm, wu_hbm, wd_hbm))]

    def issue(ids_ref, s):
        for r in range(rb):
            pltpu.make_async_copy(x_hbm.at[ids_ref[0, 0, r]], xbuf.at[s, pl.ds(r * SUB, SUB)],
                                  sem.at[s]).start()

    def wait(s):
        pltpu.make_async_copy(x2d_hbm.at[pl.ds(0, rb * SUB)], xbuf.at[s], sem.at[s]).wait()

    @pl.when(step == 0)
    def _():
        for cp in weight_copies(0):
            cp.start(priority=1)
        issue(idx_ref, 0)

    @pl.when(pl.program_id(1) == 0)
    def _():
        for m, cp in enumerate(weight_copies(e)):
            cp.wait()
            wbf[m] = wst[m].astype(BF16)

        @pl.when(e + 1 < ne)
        def _():
            for cp in weight_copies(e + 1):
                cp.start(priority=1)

    issue(idxn_ref, 1 - slot)
    wait(slot)

    xe = _rms(_load_token_tiles(xbuf, rb, (slot,)), mg_ref[...]).astype(BF16)
    hid = (_silu(_dot(xe, wbf[0])) * _dot(xe, wbf[1])).astype(BF16)
    ye = _dot(hid, wbf[2])
    ri = lax.broadcasted_iota(I32, (LANES, LANES), 0)
    ci = lax.broadcasted_iota(I32, (LANES, LANES), 1)
    gate = gate_ref[0]
    gcol = jnp.concatenate(
        [jnp.sum(jnp.where(ri == ci, gate[:, q:q + LANES], 0.0), axis=1, keepdims=True)
         for q in range(0, rb, LANES)], axis=0)
    ye_ref[...] = (ye * gcol).astype(BF16)

    @pl.when(step == nsteps - 1)
    def _():
        wait(1 - slot)


def _ffn(idx, gate, x1t, mg, wg, wu, wd, layer):
    ne, cap = idx.shape
    t = x1t.shape[0] // SUB
    rb = min(RB, cap)
    nblk = cap // rb
    nsteps = ne * nblk
    idx3 = idx.reshape(nsteps, 1, rb)
    gate3 = gate.reshape(nsteps, 1, rb)
    hbm = pl.BlockSpec(memory_space=pl.ANY)
    return pl.pallas_call(
        functools.partial(_ffn_kernel, layer=layer, ne=ne, nsteps=nsteps, rb=rb),
        grid=(ne, nblk),
        in_specs=[
            pl.BlockSpec((1, 1, rb), lambda e, j: (e * nblk + j, 0, 0), memory_space=pltpu.SMEM),
            pl.BlockSpec((1, 1, rb), lambda e, j: (jnp.minimum(e * nblk + j + 1, nsteps - 1), 0, 0),
                         memory_space=pltpu.SMEM),
            pl.BlockSpec((1, 1, rb), lambda e, j: (e * nblk + j, 0, 0)),
            hbm, hbm,
            pl.BlockSpec((1, D_MODEL), lambda e, j: (0, 0)),
            hbm, hbm, hbm,
        ],
        out_specs=pl.BlockSpec((rb, D_MODEL), lambda e, j: (e * nblk + j, 0)),
        out_shape=jax.ShapeDtypeStruct((ne * cap, D_MODEL), BF16),
        scratch_shapes=[pltpu.VMEM((2, rb * SUB, LANES), F32),
                        pltpu.VMEM((3, D_MODEL, D_MODEL), F32),
                        pltpu.VMEM((3, D_MODEL, D_MODEL), BF16),
                        pltpu.SemaphoreType.DMA((2,)), pltpu.SemaphoreType.DMA((3,))],
        compiler_params=_cparams(("arbitrary", "arbitrary")),
        name="ffn",
    )(idx3, idx3, gate3, x1t.reshape(t, SUB, LANES), x1t, mg, wg, wu, wd)


CHUNK_ROWS = 16
MAIN_CHUNKS = 96
MAIN_ROWS = CHUNK_ROWS * MAIN_CHUNKS
MAX_CHUNKS = -(-(N_EXPERTS * TB // CHUNK_ROWS + 2 * N_EXPERTS) // MAIN_CHUNKS) * MAIN_CHUNKS
NO_SLOT = 1 << 30


def _combine_kernel(cb_ref, x1t_ref, pos_ref, ye_hbm, *rest, nbk, cap, ntb, final):
    if final:
        g_ref, out_ref, ybuf, acc_scr, sem, ie_scr, ir_scr, ic_scr, n_scr = rest
    else:
        out_ref, ybuf, acc_scr, sem, ie_scr, ir_scr, ic_scr, n_scr = rest
    tb = pl.program_id(0)
    s = lax.rem(tb, 2)
    bpt = TB // LANES
    shift = CHUNK_ROWS.bit_length() - 1

    def build(tbx, par):
        base = par * MAX_CHUNKS
        n = jnp.int32(0)
        for e in range(N_EXPERTS):
            b0 = e * nbk + tbx * bpt
            lo = jnp.where(tbx == 0, 0, cb_ref[jnp.maximum(b0 - 1, 0)])
            hi = cb_ref[b0 + bpt - 1]
            start = lax.shift_left(lax.shift_right_logical(lo, shift), shift)
            nch = jnp.where(hi > lo,
                            lax.shift_right_logical(hi - start + (CHUNK_ROWS - 1), shift), 0)

            def push(c, k, e=e, start=start):
                ie_scr[base + k] = e
                ir_scr[base + k] = e * cap + start + c * CHUNK_ROWS
                ic_scr[base + k] = start + c * CHUNK_ROWS
                return k + 1

            n = lax.fori_loop(0, nch, push, n)
        nsg = jnp.maximum(lax.div(n + (MAIN_CHUNKS - 1), MAIN_CHUNKS), 1)

        def pad(k, _):
            ie_scr[base + k] = 0
            ir_scr[base + k] = 0
            ic_scr[base + k] = NO_SLOT
            return 0

        lax.fori_loop(n, nsg * MAIN_CHUNKS, pad, 0)
        n_scr[par] = n
        return n

    def chunk_copy(par, k, k0):
        base = par * MAX_CHUNKS
        row0 = pl.multiple_of(ir_scr[base + k], CHUNK_ROWS)
        dst0 = pl.multiple_of((k - k0) * CHUNK_ROWS, CHUNK_ROWS)
        return pltpu.make_async_copy(ye_hbm.at[pl.ds(row0, CHUNK_ROWS)],
                                     ybuf.at[par, pl.ds(dst0, CHUNK_ROWS)], sem.at[par])

    def issue_main(par):
        for k in range(MAIN_CHUNKS):
            chunk_copy(par, k, 0).start()

    def wait_main(par):
        pltpu.make_async_copy(ye_hbm.at[pl.ds(0, MAIN_ROWS)], ybuf.at[par], sem.at[par]).wait()

    def issue(par, k0, k1):
        def body(k, _):
            chunk_copy(par, k, k0).start()
            return 0
        lax.fori_loop(k0, k1, body, 0)

    def drain(par, k0, k1):
        def body(k, _):
            chunk_copy(par, k, k0).wait()
            return 0
        lax.fori_loop(k0, k1, body, 0)

    @pl.when(tb == 0)
    def _():
        build(0, 0)
        issue_main(0)

    @pl.when(tb + 1 < ntb)
    def _():
        build(tb + 1, 1 - s)
        issue_main(1 - s)

    jrow = lax.broadcasted_iota(I32, (CHUNK_ROWS, TB), 0)

    def scatter(sg):
        base = s * MAX_CHUNKS + sg * MAIN_CHUNKS
        pieces = []
        for c in range(MAIN_CHUNKS):
            prow = pos_ref[pl.ds(ie_scr[base + c], 1), :]
            pieces.append((prow == jrow + ic_scr[base + c]).astype(BF16))
        onehot_t = jnp.concatenate(pieces, axis=0)
        return lax.dot_general(onehot_t, ybuf[s], (((0,), (0,)), ((), ())),
                               preferred_element_type=F32)

    def finish(x2):
        return _rms(x2, g_ref[...]) if final else x2

    n = n_scr[s]
    wait_main(s)
    nsg = lax.div(n + (MAIN_CHUNKS - 1), MAIN_CHUNKS)

    @pl.when(nsg <= 1)
    def _():
        out_ref[...] = finish(_load_token_tiles(x1t_ref, TB) + scatter(0))

    @pl.when(nsg > 1)
    def _():
        acc_scr[...] = _load_token_tiles(x1t_ref, TB) + scatter(0)

        def extra(sg, _):
            k0 = sg * MAIN_CHUNKS
            k1 = jnp.minimum(n, k0 + MAIN_CHUNKS)
            issue(s, k0, k1)
            drain(s, k0, k1)
            acc_scr[...] += scatter(sg)
            return 0

        lax.fori_loop(1, nsg, extra, 0)
        out_ref[...] = finish(acc_scr[...])


def _combine(cb, pos, x1t, ye, g_final=None):
    t = x1t.shape[0] // SUB
    ne, nbk = cb.shape
    cap = ye.shape[0] // ne
    final = g_final is not None
    blk = pl.BlockSpec((TB, D_MODEL), lambda i, cb: (i, 0))
    in_specs = [pl.BlockSpec((TB * SUB, LANES), lambda i, cb: (i, 0)),
                pl.BlockSpec((ne, TB), lambda i, cb: (0, i)), pl.BlockSpec(memory_space=pl.ANY)]
    args = [cb.reshape(-1), x1t, pos.reshape(ne, t), ye]
    if final:
        in_specs.append(pl.BlockSpec((1, D_MODEL), lambda i, cb: (0, 0)))
        args.append(g_final)
    grid_spec = pltpu.PrefetchScalarGridSpec(
        num_scalar_prefetch=1,
        grid=(t // TB,),
        in_specs=in_specs,
        out_specs=blk,
        scratch_shapes=[pltpu.VMEM((2, MAIN_ROWS, D_MODEL), BF16),
                        pltpu.VMEM((TB, D_MODEL), F32),
                        pltpu.SemaphoreType.DMA((2,)),
                        pltpu.SMEM((2 * MAX_CHUNKS,), I32), pltpu.SMEM((2 * MAX_CHUNKS,), I32),
                        pltpu.SMEM((2 * MAX_CHUNKS,), I32), pltpu.SMEM((2,), I32)],
    )
    return pl.pallas_call(
        functools.partial(_combine_kernel, nbk=nbk, cap=cap, ntb=t // TB, final=final),
        grid_spec=grid_spec,
        out_shape=jax.ShapeDtypeStruct((t, D_MODEL), F32),
        compiler_params=_cparams(("arbitrary",)),
        name="combine_final" if final else "combine",
    )(*args)


def _dft_mats(n, scale):
    k = np.arange(n)
    ang = 2.0 * np.pi * ((k[:, None] * k[None, :]) % n) / n
    return (jnp.asarray(np.cos(ang) * scale, F32).astype(BF16),
            jnp.asarray(np.sin(ang) * scale, F32).astype(BF16))


def _lane_slab_specs(block, index_map):
    return [pl.BlockSpec(block + (LANES,), functools.partial(lambda *i, c: index_map(*i) + (c,), c=c))
            for c in range(LANE_SLABS)]


def _every_8th_row(slabs, j, rows):
    parts = [slabs[c, pl.ds(j, rows, stride=SUB), :] if not isinstance(slabs, (list, tuple))
             else slabs[c][pl.ds(j, rows, stride=SUB), :] for c in range(LANE_SLABS)]
    return jnp.concatenate(parts, axis=1)


def _fft1_kernel(*refs):
    xs = refs[:LANE_SLABS]
    g_ref, cs1_ref, twc_ref, tws_ref, br_ref, bi_ref, x_scr = refs[LANE_SLABS:]
    n1 = cs1_ref.shape[1]
    for c in range(LANE_SLABS):
        x_scr[c] = xs[c][...].reshape(n1 * SUB, LANES)
    for j in range(SUB):
        xj = _rms(_every_8th_row(x_scr, j, n1), g_ref[...]).astype(BF16)
        a = _dot(cs1_ref[...], xj)
        ar, ai = a[:n1], a[n1:]
        c = twc_ref[0, :, j:j + 1]
        s = tws_ref[0, :, j:j + 1]
        br_ref[:, j] = (ar * c + ai * s).reshape(n1 // SUB, SUB, D_MODEL)
        bi_ref[:, j] = (ai * c - ar * s).reshape(n1 // SUB, SUB, D_MODEL)


def _fft1(x, g, nb, n1):
    t = x.shape[0]
    n = n1 * FFT_N2
    c1, s1 = _dft_mats(n1, 1.0 / math.sqrt(n1))
    cs1 = jnp.concatenate([c1, -s1], axis=0)
    k1 = np.arange(n1)[:, None]
    n2 = np.arange(FFT_N2)[None, :]
    ang = 2.0 * np.pi * ((k1 * n2) % n) / n
    nj = FFT_N2 // SUB
    nkc = n1 // SUB
    to_tab = lambda a: jnp.asarray(a.reshape(n1, nj, SUB).transpose(1, 0, 2), F32)
    twc, tws = to_tab(np.cos(ang)), to_tab(np.sin(ang))
    xv = x.reshape(nb * n1, FFT_N2, D_MODEL)
    out_blk = pl.BlockSpec((nkc, SUB, SUB, D_MODEL), lambda b, j: (b, j, 0, 0))
    out_shape = jax.ShapeDtypeStruct((nb * nkc, FFT_N2, SUB, D_MODEL), F32)
    tws_spec = pl.BlockSpec((1, n1, SUB), lambda b, j: (j, 0, 0))
    mat = pl.BlockSpec((2 * n1, n1), lambda b, j: (0, 0))
    br, bi = pl.pallas_call(
        _fft1_kernel,
        grid=(nb, nj),
        in_specs=_lane_slab_specs((n1, SUB), lambda b, j: (b, j))
        + [pl.BlockSpec((1, D_MODEL), lambda b, j: (0, 0)), mat, tws_spec, tws_spec],
        out_specs=[out_blk, out_blk],
        out_shape=[out_shape, out_shape],
        scratch_shapes=[pltpu.VMEM((LANE_SLABS, n1 * SUB, LANES), F32)],
        compiler_params=_cparams(("arbitrary", "arbitrary")),
        name="fft1",
    )(*([xv] * LANE_SLABS), g, cs1, twc, tws)
    return br.reshape(t, D_MODEL), bi.reshape(t, D_MODEL)


FFT_GROUP = 4


def _fft2_kernel(*refs):
    brs, bis, xs = (refs[i * LANE_SLABS:(i + 1) * LANE_SLABS] for i in range(3))
    (w2r_ref, w2i_ref, cc_ref, sc_ref, wout_ref, mg_ref, wrh_ref, wrl_ref,
     x1t_ref, lg_ref, f_scr, x_scr) = refs[3 * LANE_SLABS:]
    n2 = FFT_N2
    for c in range(LANE_SLABS):
        x_scr[c] = xs[c][...].reshape(n2 * SUB, LANES)
    for jp in range(SUB // FFT_GROUP):
        for q in range(FFT_GROUP):
            j = jp * FFT_GROUP + q
            bj = jnp.concatenate([_every_8th_row(brs, j, n2).astype(BF16),
                                  _every_8th_row(bis, j, n2).astype(BF16)], axis=0)
            yr = _dot(w2r_ref[...], bj).astype(BF16)
            yi = _dot(w2i_ref[...], bj).astype(BF16)
            for g in range(C_GROUPS):
                cols = slice(g * C_GROUP_DIM, (g + 1) * C_GROUP_DIM)
                f = _dot(yr[:, cols], cc_ref[...]) + _dot(yi[:, cols], sc_ref[...])
                f_scr[q * n2:(q + 1) * n2, cols] = f.astype(BF16)
        y = _dot(f_scr[...], wout_ref[...])
        for q in range(FFT_GROUP):
            j = jp * FFT_GROUP + q
            x1 = _every_8th_row(x_scr, j, n2) + y[q * n2:(q + 1) * n2, :]
            x1t_ref[:, j * SUB:(j + 1) * SUB, :] = x1.reshape(n2, SUB, LANES)
            lg_ref[:, j * n2:(j + 1) * n2] = _router_logits(x1, mg_ref, wrh_ref, wrl_ref)


def _fft2(br, bi, x, nb, n1, wout, mg, wr):
    t = x.shape[0]
    n2 = FFT_N2
    c2, s2 = _dft_mats(n2, 1.0 / math.sqrt(n2))
    w2r = jnp.concatenate([c2, s2], axis=1)
    w2i = jnp.concatenate([-s2, c2], axis=1)
    cc, sc = _dft_mats(C_GROUP_DIM, 1.0 / math.sqrt(C_GROUP_DIM))
    nkc = n1 // SUB
    rows = SUB * n2
    xv = x.reshape(nb * n2, n1, D_MODEL)
    full = lambda shape: pl.BlockSpec(shape, lambda b, k: (0,) * len(shape))
    b_slabs = _lane_slab_specs((rows,), lambda b, k: (b * nkc + k,))
    x_slabs = _lane_slab_specs((n2, SUB), lambda b, k: (b, k))
    x1, lg = pl.pallas_call(
        _fft2_kernel,
        grid=(nb, nkc),
        in_specs=b_slabs * 2 + x_slabs
        + [full((n2, 2 * n2)), full((n2, 2 * n2)),
           full((C_GROUP_DIM, C_GROUP_DIM)), full((C_GROUP_DIM, C_GROUP_DIM)),
           full((D_MODEL, D_MODEL)), full((1, D_MODEL)),
           full((N_EXPERTS, D_MODEL)), full((N_EXPERTS, D_MODEL))],
        out_specs=[pl.BlockSpec((n2, SUB * SUB, LANES), lambda b, k: (b, k, 0)),
                   pl.BlockSpec((N_EXPERTS, rows), lambda b, k: (0, b * nkc + k))],
        out_shape=[jax.ShapeDtypeStruct((nb * n2, n1 * SUB, LANES), F32),
                   jax.ShapeDtypeStruct((N_EXPERTS, t), F32)],
        scratch_shapes=[pltpu.VMEM((FFT_GROUP * n2, D_MODEL), BF16),
                        pltpu.VMEM((LANE_SLABS, rows, LANES), F32)],
        compiler_params=_cparams(("arbitrary", "arbitrary")),
        name="fft2",
    )(*([br] * LANE_SLABS), *([bi] * LANE_SLABS), *([xv] * LANE_SLABS),
      w2r, w2i, cc, sc, wout, mg, *wr)
    lg = lg.reshape(N_EXPERTS, nb, n1, n2).transpose(0, 1, 3, 2).reshape(N_EXPERTS, t)
    return x1.reshape(t * SUB, LANES), lg


def _moe(x1, logits_t, mg, p, layer, g_final=None):
    t = x1.shape[0] // SUB
    cap = CAPACITY_FACTOR * t // N_EXPERTS
    idx, gate, cb, pos = _route(logits_t, cap)
    ye = _ffn(idx, gate, x1, mg, p["wg"], p["wu"], p["wd"], layer)
    return _combine(cb, pos, x1, ye, g_final)


def _trunk(x3, p):
    nb, s, _ = x3.shape
    x = x3.reshape(nb * s, D_MODEL)
    row = lambda v: v.reshape(1, -1)
    outa, glu = _front(x, row(p["mix_norm"][0]), p["w_in"], row(p["ab_v_norm_g"][0]),
                       row(p["ab_v_norm_b"][0]), p["w_spatial"], p["b_spatial"])
    mg0, mg1 = row(p["moe_norm"][0]), row(p["moe_norm"][1])
    x1, lg = _back(x, outa, glu, p["ab_conv_w"][0], row(p["ab_conv_b"][0]),
                   row(p["ab_conv_norm_g"][0]), row(p["ab_conv_norm_b"][0]), p["ab_w_out"],
                   mg0, p["router"][0], s)
    x2 = _moe(x1, lg, mg0, p, 0)
    n1 = s // FFT_N2
    br, bi = _fft1(x2, row(p["mix_norm"][1]), nb, n1)
    x1, lg = _fft2(br, bi, x2, nb, n1, p["c_w_out"], mg1, p["router"][1])
    y = _moe(x1, lg, mg1, p, 1, row(p["final_norm"]))
    return y.reshape(nb, s, D_MODEL)


def kernel(x_prompt, x_sample, mix_norm, ab_w_in, ab_v_norm_g, ab_v_norm_b, ab_w_spatial,
           ab_b_spatial, ab_conv_w, ab_conv_b, ab_conv_norm_g, ab_conv_norm_b, ab_w_out,
           c_w_out, moe_norm, moe_router, moe_w_gate, moe_w_up, moe_w_down, final_norm):
    p = dict(
        mix_norm=mix_norm, moe_norm=moe_norm, final_norm=final_norm,
        w_in=ab_w_in[0].astype(BF16),
        ab_v_norm_g=ab_v_norm_g, ab_v_norm_b=ab_v_norm_b,
        w_spatial=ab_w_spatial[0].astype(BF16),
        b_spatial=jnp.repeat(jnp.transpose(ab_b_spatial[0]), A_HEAD_DIM, axis=1),
        ab_conv_w=ab_conv_w, ab_conv_b=ab_conv_b,
        ab_conv_norm_g=ab_conv_norm_g, ab_conv_norm_b=ab_conv_norm_b,
        ab_w_out=ab_w_out[0].astype(BF16),
        c_w_out=c_w_out[0].astype(BF16),
        router=[_split2(jnp.transpose(moe_router[l])) for l in range(moe_router.shape[0])],
        wg=moe_w_gate, wu=moe_w_up, wd=moe_w_down,
    )
    return (_trunk(x_prompt, p), _trunk(x_sample, p))
```

```python
import functools
import math

import numpy as np
import jax
import jax.numpy as jnp
from jax import lax
from jax.experimental import pallas as pl
from jax.experimental.pallas import tpu as pltpu

F32, BF16, I32 = jnp.float32, jnp.bfloat16, jnp.int32

D_MODEL = 1024
D_A = 512
D_B = 512
A_GROUPS = 4
A_HEAD_DIM = 128
CHUNK = 128
CONV_WIDTH = 31
CONV_PAD = 15
C_GROUPS = 4
C_GROUP_DIM = 256
N_EXPERTS = 16
CAPACITY_FACTOR = 2
RMS_EPS = 1e-6
LN_EPS = 1e-5

LANES = 128
SUB = 8
LANE_SLABS = D_MODEL // LANES
HALO = 16
TM = 512
RB = 512
TB = 512
FFT_N2 = 128
VMEM_LIMIT = 48 * 1024 * 1024
BISECT_BITS_STEPS = 31
BISECT_VALUE_STEPS = 30


def _cparams(sem):
    return pltpu.CompilerParams(dimension_semantics=sem, vmem_limit_bytes=VMEM_LIMIT)


def _dot(a, b):
    return jnp.dot(a, b, preferred_element_type=F32)


def _dot_nt(a, b, precision=None):
    return lax.dot_general(a, b, (((1,), (1,)), ((), ())), precision=precision,
                           preferred_element_type=F32)


def _rms(x, g):
    return x * lax.rsqrt(jnp.mean(x * x, axis=-1, keepdims=True) + RMS_EPS) * g


def _ln(x, g, b):
    mu = jnp.mean(x, axis=-1, keepdims=True)
    xc = x - mu
    var = jnp.mean(xc * xc, axis=-1, keepdims=True)
    return xc * lax.rsqrt(var + LN_EPS) * g + b


def _gelu(x):
    return 0.5 * x * (1.0 + lax.erf(x * np.float32(math.sqrt(0.5))))


def _silu(x):
    return x * jax.nn.sigmoid(x)


def _front_kernel(x_ref, g_ref, win_ref, vg_ref, vb_ref, wsp_ref, bsp_ref,
                  outa_ref, glu_ref, zu_scr, vn_scr):
    xn = _rms(x_ref[...], g_ref[...]).astype(BF16)
    zu_scr[...] = _gelu(_dot(xn, win_ref[:, 0:D_A]))
    zv = _gelu(_dot(xn, win_ref[:, D_A:2 * D_A]))
    vn_scr[...] = _ln(zv, vg_ref[...], vb_ref[...]).astype(BF16)
    a_in = _dot(xn, win_ref[:, 2 * D_A:2 * D_A + D_B])
    gate = _dot(xn, win_ref[:, 2 * D_A + D_B:])
    glu_ref[...] = a_in * jax.nn.sigmoid(gate)
    for c in range(TM // CHUNK):
        rows = slice(c * CHUNK, (c + 1) * CHUNK)
        for h in range(A_GROUPS):
            cols = slice(h * A_HEAD_DIM, (h + 1) * A_HEAD_DIM)
            mixed = _dot(wsp_ref[h], vn_scr[rows, cols]) + bsp_ref[:, cols]
            outa_ref[rows, cols] = (zu_scr[rows, cols] * mixed).astype(BF16)


def _front(x, g, w_in, vg, vb, wsp, bsp):
    t = x.shape[0]
    full = lambda shape: pl.BlockSpec(shape, lambda i: (0,) * len(shape))
    return pl.pallas_call(
        _front_kernel,
        grid=(t // TM,),
        in_specs=[
            pl.BlockSpec((TM, D_MODEL), lambda i: (i, 0)),
            full((1, D_MODEL)),
            full((D_MODEL, 2 * D_A + 2 * D_B)),
            full((1, D_A)), full((1, D_A)),
            full((A_GROUPS, CHUNK, CHUNK)),
            full((CHUNK, D_A)),
        ],
        out_specs=[pl.BlockSpec((TM, D_A), lambda i: (i, 0)),
                   pl.BlockSpec((TM, D_B), lambda i: (i, 0))],
        out_shape=[jax.ShapeDtypeStruct((t, D_A), BF16),
                   jax.ShapeDtypeStruct((t, D_B), F32)],
        scratch_shapes=[pltpu.VMEM((TM, D_A), F32), pltpu.VMEM((TM, D_A), BF16)],
        compiler_params=_cparams(("arbitrary",)),
        name="front",
    )(x, g, w_in, vg, vb, wsp, bsp)


CONV_ROWS = 256
SUBLANES = 8
SHIFT_ROWS = TM + (HALO + CONV_PAD) // SUBLANES * SUBLANES


def _split2(a):
    hi = a.astype(BF16)
    return hi, (a - hi.astype(F32)).astype(BF16)


def _store_token_tiles(tiles_ref, x):
    rows = x.shape[0]
    for c in range(LANE_SLABS):
        tiles_ref[pl.ds(c, rows, stride=SUB), :] = x[:, c * LANES:(c + 1) * LANES]


def _load_token_tiles(tiles_ref, rows, lead=()):
    parts = [tiles_ref[lead + (pl.ds(c, rows, stride=SUB), slice(None))] for c in range(LANE_SLABS)]
    return jnp.concatenate(parts, axis=1)


def _router_logits(x1, mg_ref, wrh_ref, wrl_ref):
    xh, xl = _split2(_rms(x1, mg_ref[...]))
    return _dot_nt(wrh_ref[...], xh) + (_dot_nt(wrh_ref[...], xl) + _dot_nt(wrl_ref[...], xh))


def _back_kernel(x_ref, outa_ref, glu_ref, prev_ref, next_ref, cw_ref, cb_ref, ng_ref, nb_ref,
                 wout_ref, mg_ref, wrh_ref, wrl_ref, x1t_ref, lg_ref, ext_scr, sh_scr, ob_scr,
                 *, blocks_per_seq):
    i = pl.program_id(0)
    pos = lax.rem(i, blocks_per_seq)
    ext_scr[0:HALO, :] = jnp.where(pos == 0, 0.0, prev_ref[...])
    ext_scr[HALO:HALO + TM, :] = glu_ref[...]
    ext_scr[HALO + TM:, :] = jnp.where(pos == blocks_per_seq - 1, 0.0, next_ref[...])
    for b in range(SUBLANES):
        sh_scr[b] = ext_scr[b:b + SHIFT_ROWS, :]
    for r in range(TM // CONV_ROWS):
        acc = jnp.zeros((CONV_ROWS, D_B), F32) + cb_ref[...]
        for k in range(CONV_WIDTH):
            a, b = divmod(HALO - CONV_PAD + k, SUBLANES)
            row0 = r * CONV_ROWS + a * SUBLANES
            acc = acc + cw_ref[k:k + 1, :] * sh_scr[b, row0:row0 + CONV_ROWS, :]
        ob = _silu(_ln(acc, ng_ref[...], nb_ref[...]))
        ob_scr[r * CONV_ROWS:(r + 1) * CONV_ROWS, :] = ob.astype(BF16)
    y = _dot(outa_ref[...], wout_ref[0:D_A, :]) + _dot(ob_scr[...], wout_ref[D_A:, :])
    x1 = x_ref[...] + y
    _store_token_tiles(x1t_ref, x1)
    lg_ref[...] = _router_logits(x1, mg_ref, wrh_ref, wrl_ref)


def _back(x, outa, glu, cw, cb, ng, nb_, wout, mg, wr, seq_len):
    t = x.shape[0]
    full = lambda shape: pl.BlockSpec(shape, lambda i: (0,) * len(shape))
    hb = TM // HALO
    return pl.pallas_call(
        functools.partial(_back_kernel, blocks_per_seq=seq_len // TM),
        grid=(t // TM,),
        in_specs=[
            pl.BlockSpec((TM, D_MODEL), lambda i: (i, 0)),
            pl.BlockSpec((TM, D_A), lambda i: (i, 0)),
            pl.BlockSpec((TM, D_B), lambda i: (i, 0)),
            pl.BlockSpec((HALO, D_B), lambda i: (jnp.maximum(i * hb - 1, 0), 0)),
            pl.BlockSpec((HALO, D_B), lambda i: (jnp.minimum((i + 1) * hb, t // HALO - 1), 0)),
            full((CONV_WIDTH, D_B)), full((1, D_B)), full((1, D_B)), full((1, D_B)),
            full((D_A + D_B, D_MODEL)),
            full((1, D_MODEL)),
            full((N_EXPERTS, D_MODEL)), full((N_EXPERTS, D_MODEL)),
        ],
        out_specs=[pl.BlockSpec((TM * SUB, LANES), lambda i: (i, 0)),
                   pl.BlockSpec((N_EXPERTS, TM), lambda i: (0, i))],
        out_shape=[jax.ShapeDtypeStruct((t * SUB, LANES), F32),
                   jax.ShapeDtypeStruct((N_EXPERTS, t), F32)],
        scratch_shapes=[pltpu.VMEM((TM + 2 * HALO, D_B), F32),
                        pltpu.VMEM((SUBLANES, SHIFT_ROWS, D_B), F32),
                        pltpu.VMEM((TM, D_B), BF16)],
        compiler_params=_cparams(("arbitrary",)),
        name="back",
    )(x, outa, glu, glu, glu, cw, cb, ng, nb_, wout, mg, *wr)


def _split3(a):
    a1 = a.astype(BF16)
    r1 = a - a1.astype(F32)
    a2 = r1.astype(BF16)
    a3 = (r1 - a2.astype(F32)).astype(BF16)
    return a1, a2, a3


def _route_kernel(lg_ref, idx_ref, gate_ref, cb_ref, pos_ref, cnt_tab, val_tab, cbr_scr, *, nbk, cap):
    ne = N_EXPERTS
    lg = lg_ref[...]
    ex = jnp.exp(lg - jnp.max(lg, axis=0, keepdims=True))
    aff = ex / jnp.sum(ex, axis=0, keepdims=True)

    def count(m):
        s = jnp.sum(m.astype(F32), axis=1, keepdims=True)
        return jnp.sum(s, axis=2, keepdims=True)

    def update(mid, lo, hi):
        ge = count(aff >= mid) >= cap
        return jnp.where(ge, mid, lo), jnp.where(ge, hi, mid)

    def bits_body(_, c):
        lo, hi = c
        lo_i = lax.bitcast_convert_type(lo, I32)
        hi_i = lax.bitcast_convert_type(hi, I32)
        mid = lax.bitcast_convert_type(lo_i + lax.shift_right_logical(hi_i - lo_i, 1), F32)
        return update(mid, lo, hi)

    def value_body(_, c):
        lo, hi = c
        return update(lo + (hi - lo) * 0.5, lo, hi)

    lo0 = jnp.zeros((ne, 1, 1), F32)
    hi0 = jnp.full((ne, 1, 1), jnp.inf, F32)
    c = lax.fori_loop(0, BISECT_BITS_STEPS, bits_body, (lo0, hi0))
    lo, hi = lax.fori_loop(0, BISECT_VALUE_STEPS, value_body, c)

    li = lax.broadcasted_iota(I32, (LANES, LANES), 0)
    lj = lax.broadcasted_iota(I32, (LANES, LANES), 1)
    ut = (li <= lj).astype(BF16)
    ones = jnp.ones((LANES, LANES), BF16)
    bi = lax.broadcasted_iota(I32, (nbk, nbk), 0)
    bj = lax.broadcasted_iota(I32, (nbk, nbk), 1)
    slt = (bj < bi).astype(BF16)
    utb = (bi <= bj).astype(BF16)
    ones8 = jnp.ones((8, LANES), BF16)

    def block_cum(m):
        m2 = m.astype(BF16).reshape(ne * nbk, LANES)
        loc = _dot(m2, ut).reshape(ne, nbk, LANES)
        tot = _dot(m2, ones).astype(BF16).reshape(ne, nbk, LANES)
        off = jnp.stack([_dot(slt, tot[e]) for e in range(ne)])
        return loc, tot, off

    gt = aff >= hi
    eq = (aff >= lo) & (aff < hi)
    need = np.float32(cap) - count(gt)
    eloc, _, eoff = block_cum(eq)
    mask = gt | (eq & (eloc + eoff <= need))

    mloc, mtot, moff = block_cum(mask)
    pos_ref[...] = jnp.where(mask, mloc + moff - 1.0, -1.0).astype(I32)
    cnt_tab[:, :, 0:LANES] = jnp.ones((ne, nbk, LANES), BF16)
    cnt_tab[:, :, LANES:] = mtot
    val_tab[:, :, 0:LANES] = mloc.astype(BF16)
    for k, piece in enumerate(_split3(aff)):
        val_tab[:, :, (k + 1) * LANES:(k + 2) * LANES] = piece
    maskb = mask.astype(BF16)
    for e in range(ne):
        cnt = _dot_nt(ones8, maskb[e])
        cbr_scr[e] = _dot(cnt.astype(BF16), utb)
    cb_ref[...] = cbr_scr[:, 0, :].astype(I32)

    s_col = lax.broadcasted_iota(I32, (cap, 1), 0).astype(F32)
    lane_b = lax.broadcasted_iota(I32, (cap, nbk), 1).astype(F32)
    lane_l = lax.broadcasted_iota(I32, (cap, LANES), 1).astype(F32)
    lane_e = lax.broadcasted_iota(I32, (cap, LANES), 1)

    def per_expert(e, _):
        cbrow = cbr_scr[e][0:1, :]
        le = (cbrow <= s_col).astype(BF16)
        bs = _dot(le, cnt_tab[e])
        blk_c, start_c = bs[:, 0:LANES], bs[:, LANES:]
        oh = (lane_b == blk_c[:, 0:nbk]).astype(BF16)
        vals = _dot(oh, val_tab[e])
        le2 = (vals[:, 0:LANES] <= s_col - start_c).astype(BF16)
        within_c = _dot(le2, ones)
        mine = lane_e == e
        idx_ref[...] = jnp.where(mine, (blk_c * LANES + within_c).astype(I32), idx_ref[...])
        aff_row = (vals[:, LANES:2 * LANES] + vals[:, 2 * LANES:3 * LANES]) + vals[:, 3 * LANES:]
        g_col = jnp.sum(jnp.where(lane_l == within_c, aff_row, 0.0), axis=1, keepdims=True)
        gate_ref[...] = jnp.where(mine, g_col, gate_ref[...])
        return 0

    idx_ref[...] = jnp.zeros_like(idx_ref)
    gate_ref[...] = jnp.zeros_like(gate_ref)

    lax.fori_loop(0, ne, per_expert, 0)


def _route(logits_t, cap):
    ne, t = logits_t.shape
    nbk = t // LANES
    lg3 = logits_t.reshape(ne, nbk, LANES)
    full = lambda shape: pl.BlockSpec(shape, lambda i: (0,) * len(shape))
    idx, gate, cb, pos = pl.pallas_call(
        functools.partial(_route_kernel, nbk=nbk, cap=cap),
        grid=(1,),
        in_specs=[full((ne, nbk, LANES))],
        out_specs=[full((cap, LANES)), full((cap, LANES)), full((ne, nbk)), full((ne, nbk, LANES))],
        out_shape=[jax.ShapeDtypeStruct((cap, LANES), I32),
                   jax.ShapeDtypeStruct((cap, LANES), F32),
                   jax.ShapeDtypeStruct((ne, nbk), I32),
                   jax.ShapeDtypeStruct((ne, nbk, LANES), I32)],
        scratch_shapes=[pltpu.VMEM((ne, nbk, 2 * LANES), BF16), pltpu.VMEM((ne, nbk, 4 * LANES), BF16),
                        pltpu.VMEM((ne, 8, nbk), F32)],
        compiler_params=_cparams(("arbitrary",)),
        name="route",
    )(lg3)
    return jnp.transpose(idx[:, :ne]), jnp.transpose(gate[:, :ne]), cb, pos


def _ffn_kernel(idx_ref, idxn_ref, gate_ref, x_hbm, x2d_hbm, mg_ref, wg_hbm, wu_hbm, wd_hbm, ye_ref,
                xbuf, wst, wbf, sem, wsem, *, layer, ne, nsteps, rb):
    e = pl.program_id(0)
    step = e * pl.num_programs(1) + pl.program_id(1)
    slot = lax.rem(step, 2)

    def weight_copies(ex):
        return [pltpu.make_async_copy(w_hbm.at[layer, ex], wst.at[m], wsem.at[m])
                for m, w_hbm in enumerate((wg_hbm, wu_hbm, wd_hbm))]

    def issue(ids_ref, s):
        for r in range(rb):
            pltpu.make_async_copy(x_hbm.at[ids_ref[0, 0, r]], xbuf.at[s, pl.ds(r * SUB, SUB)],
                                  sem.at[s]).start(priority=r % 2)

    def wait(s):
        pltpu.make_async_copy(x2d_hbm.at[pl.ds(0, rb * SUB)], xbuf.at[s], sem.at[s]).wait()

    @pl.when(step == 0)
    def _():
        for cp in weight_copies(0):
            cp.start(priority=1)
        issue(idx_ref, 0)

    @pl.when(pl.program_id(1) == 0)
    def _():
        for m, cp in enumerate(weight_copies(e)):
            cp.wait()
            wbf[m] = wst[m].astype(BF16)

        @pl.when(e + 1 < ne)
        def _():
            for cp in weight_copies(e + 1):
                cp.start(priority=1)

    issue(idxn_ref, 1 - slot)
    wait(slot)

    xe = _rms(_load_token_tiles(xbuf, rb, (slot,)), mg_ref[...]).astype(BF16)
    hid = (_silu(_dot(xe, wbf[0])) * _dot(xe, wbf[1])).astype(BF16)
    ye = _dot(hid, wbf[2])
    ri = lax.broadcasted_iota(I32, (LANES, LANES), 0)
    ci = lax.broadcasted_iota(I32, (LANES, LANES), 1)
    gate = gate_ref[0]
    gcol = jnp.concatenate(
        [jnp.sum(jnp.where(ri == ci, gate[:, q:q + LANES], 0.0), axis=1, keepdims=True)
         for q in range(0, rb, LANES)], axis=0)
    ye_ref[...] = (ye * gcol).astype(BF16)

    @pl.when(step == nsteps - 1)
    def _():
        wait(1 - slot)


def _ffn(idx, gate, x1t, mg, wg, wu, wd, layer):
    ne, cap = idx.shape
    t = x1t.shape[0] // SUB
    rb = min(RB, cap)
    nblk = cap // rb
    nsteps = ne * nblk
    idx3 = idx.reshape(nsteps, 1, rb)
    gate3 = gate.reshape(nsteps, 1, rb)
    hbm = pl.BlockSpec(memory_space=pl.ANY)
    return pl.pallas_call(
        functools.partial(_ffn_kernel, layer=layer, ne=ne, nsteps=nsteps, rb=rb),
        grid=(ne, nblk),
        in_specs=[
            pl.BlockSpec((1, 1, rb), lambda e, j: (e * nblk + j, 0, 0), memory_space=pltpu.SMEM),
            pl.BlockSpec((1, 1, rb), lambda e, j: (jnp.minimum(e * nblk + j + 1, nsteps - 1), 0, 0),
                         memory_space=pltpu.SMEM),
            pl.BlockSpec((1, 1, rb), lambda e, j: (e * nblk + j, 0, 0)),
            hbm, hbm,
            pl.BlockSpec((1, D_MODEL), lambda e, j: (0, 0)),
            hbm, hbm, hbm,
        ],
        out_specs=pl.BlockSpec((rb, D_MODEL), lambda e, j: (e * nblk + j, 0)),
        out_shape=jax.ShapeDtypeStruct((ne * cap, D_MODEL), BF16),
        scratch_shapes=[pltpu.VMEM((2, rb * SUB, LANES), F32),
                        pltpu.VMEM((3, D_MODEL, D_MODEL), F32),
                        pltpu.VMEM((3, D_MODEL, D_MODEL), BF16),
                        pltpu.SemaphoreType.DMA((2,)), pltpu.SemaphoreType.DMA((3,))],
        compiler_params=_cparams(("arbitrary", "arbitrary")),
        name="ffn",
    )(idx3, idx3, gate3, x1t.reshape(t, SUB, LANES), x1t, mg, wg, wu, wd)


CHUNK_ROWS = 16
MAIN_CHUNKS = 96
MAIN_ROWS = CHUNK_ROWS * MAIN_CHUNKS
MAX_CHUNKS = -(-(N_EXPERTS * TB // CHUNK_ROWS + 2 * N_EXPERTS) // MAIN_CHUNKS) * MAIN_CHUNKS
NO_SLOT = 1 << 30


def _combine_kernel(cb_ref, x1t_ref, pos_ref, ye_hbm, *rest, nbk, cap, ntb, final):
    if final:
        g_ref, out_ref, ybuf, acc_scr, sem, ie_scr, ir_scr, ic_scr, n_scr = rest
    else:
        out_ref, ybuf, acc_scr, sem, ie_scr, ir_scr, ic_scr, n_scr = rest
    tb = pl.program_id(0)
    s = lax.rem(tb, 2)
    bpt = TB // LANES
    shift = CHUNK_ROWS.bit_length() - 1

    def build(tbx, par):
        base = par * MAX_CHUNKS
        n = jnp.int32(0)
        for e in range(N_EXPERTS):
            b0 = e * nbk + tbx * bpt
            lo = jnp.where(tbx == 0, 0, cb_ref[jnp.maximum(b0 - 1, 0)])
            hi = cb_ref[b0 + bpt - 1]
            start = lax.shift_left(lax.shift_right_logical(lo, shift), shift)
            nch = jnp.where(hi > lo,
                            lax.shift_right_logical(hi - start + (CHUNK_ROWS - 1), shift), 0)

            def push(c, k, e=e, start=start):
                ie_scr[base + k] = e
                ir_scr[base + k] = e * cap + start + c * CHUNK_ROWS
                ic_scr[base + k] = start + c * CHUNK_ROWS
                return k + 1

            n = lax.fori_loop(0, nch, push, n)
        nsg = jnp.maximum(lax.div(n + (MAIN_CHUNKS - 1), MAIN_CHUNKS), 1)

        def pad(k, _):
            ie_scr[base + k] = 0
            ir_scr[base + k] = 0
            ic_scr[base + k] = NO_SLOT
            return 0

        lax.fori_loop(n, nsg * MAIN_CHUNKS, pad, 0)
        n_scr[par] = n
        return n

    def chunk_copy(par, k, k0):
        base = par * MAX_CHUNKS
        row0 = pl.multiple_of(ir_scr[base + k], CHUNK_ROWS)
        dst0 = pl.multiple_of((k - k0) * CHUNK_ROWS, CHUNK_ROWS)
        return pltpu.make_async_copy(ye_hbm.at[pl.ds(row0, CHUNK_ROWS)],
                                     ybuf.at[par, pl.ds(dst0, CHUNK_ROWS)], sem.at[par])

    def issue_main(par):
        for k in range(MAIN_CHUNKS):
            chunk_copy(par, k, 0).start()

    def wait_main(par):
        pltpu.make_async_copy(ye_hbm.at[pl.ds(0, MAIN_ROWS)], ybuf.at[par], sem.at[par]).wait()

    def issue(par, k0, k1):
        def body(k, _):
            chunk_copy(par, k, k0).start()
            return 0
        lax.fori_loop(k0, k1, body, 0)

    def drain(par, k0, k1):
        def body(k, _):
            chunk_copy(par, k, k0).wait()
            return 0
        lax.fori_loop(k0, k1, body, 0)

    @pl.when(tb == 0)
    def _():
        build(0, 0)
        issue_main(0)

    @pl.when(tb + 1 < ntb)
    def _():
        build(tb + 1, 1 - s)
        issue_main(1 - s)

    jrow = lax.broadcasted_iota(I32, (CHUNK_ROWS, TB), 0)

    def scatter(sg):
        base = s * MAX_CHUNKS + sg * MAIN_CHUNKS
        pieces = []
        for c in range(MAIN_CHUNKS):
            prow = pos_ref[pl.ds(ie_scr[base + c], 1), :]
            pieces.append((prow == jrow + ic_scr[base + c]).astype(BF16))
        onehot_t = jnp.concatenate(pieces, axis=0)
        return lax.dot_general(onehot_t, ybuf[s], (((0,), (0,)), ((), ())),
                               preferred_element_type=F32)

    def finish(x2):
        return _rms(x2, g_ref[...]) if final else x2

    n = n_scr[s]
    wait_main(s)
    nsg = lax.div(n + (MAIN_CHUNKS - 1), MAIN_CHUNKS)

    @pl.when(nsg <= 1)
    def _():
        out_ref[...] = finish(_load_token_tiles(x1t_ref, TB) + scatter(0))

    @pl.when(nsg > 1)
    def _():
        acc_scr[...] = _load_token_tiles(x1t_ref, TB) + scatter(0)

        def extra(sg, _):
            k0 = sg * MAIN_CHUNKS
            k1 = jnp.minimum(n, k0 + MAIN_CHUNKS)
            issue(s, k0, k1)
            drain(s, k0, k1)
            acc_scr[...] += scatter(sg)
            return 0

        lax.fori_loop(1, nsg, extra, 0)
        out_ref[...] = finish(acc_scr[...])


def _combine(cb, pos, x1t, ye, g_final=None):
    t = x1t.shape[0] // SUB
    ne, nbk = cb.shape
    cap = ye.shape[0] // ne
    final = g_final is not None
    blk = pl.BlockSpec((TB, D_MODEL), lambda i, cb: (i, 0))
    in_specs = [pl.BlockSpec((TB * SUB, LANES), lambda i, cb: (i, 0)),
                pl.BlockSpec((ne, TB), lambda i, cb: (0, i)), pl.BlockSpec(memory_space=pl.ANY)]
    args = [cb.reshape(-1), x1t, pos.reshape(ne, t), ye]
    if final:
        in_specs.append(pl.BlockSpec((1, D_MODEL), lambda i, cb: (0, 0)))
        args.append(g_final)
    grid_spec = pltpu.PrefetchScalarGridSpec(
        num_scalar_prefetch=1,
        grid=(t // TB,),
        in_specs=in_specs,
        out_specs=blk,
        scratch_shapes=[pltpu.VMEM((2, MAIN_ROWS, D_MODEL), BF16),
                        pltpu.VMEM((TB, D_MODEL), F32),
                        pltpu.SemaphoreType.DMA((2,)),
                        pltpu.SMEM((2 * MAX_CHUNKS,), I32), pltpu.SMEM((2 * MAX_CHUNKS,), I32),
                        pltpu.SMEM((2 * MAX_CHUNKS,), I32), pltpu.SMEM((2,), I32)],
    )
    return pl.pallas_call(
        functools.partial(_combine_kernel, nbk=nbk, cap=cap, ntb=t // TB, final=final),
        grid_spec=grid_spec,
        out_shape=jax.ShapeDtypeStruct((t, D_MODEL), F32),
        compiler_params=_cparams(("arbitrary",)),
        name="combine_final" if final else "combine",
    )(*args)


def _dft_mats(n, scale):
    k = np.arange(n)
    ang = 2.0 * np.pi * ((k[:, None] * k[None, :]) % n) / n
    return (jnp.asarray(np.cos(ang) * scale, F32).astype(BF16),
            jnp.asarray(np.sin(ang) * scale, F32).astype(BF16))


def _lane_slab_specs(block, index_map):
    return [pl.BlockSpec(block + (LANES,), functools.partial(lambda *i, c: index_map(*i) + (c,), c=c))
            for c in range(LANE_SLABS)]


def _every_8th_row(slabs, j, rows):
    parts = [slabs[c, pl.ds(j, rows, stride=SUB), :] if not isinstance(slabs, (list, tuple))
             else slabs[c][pl.ds(j, rows, stride=SUB), :] for c in range(LANE_SLABS)]
    return jnp.concatenate(parts, axis=1)


def _fft1_kernel(*refs):
    xs = refs[:LANE_SLABS]
    g_ref, cs1_ref, twc_ref, tws_ref, br_ref, bi_ref, x_scr = refs[LANE_SLABS:]
    n1 = cs1_ref.shape[1]
    for c in range(LANE_SLABS):
        x_scr[c] = xs[c][...].reshape(n1 * SUB, LANES)
    for j in range(SUB):
        xj = _rms(_every_8th_row(x_scr, j, n1), g_ref[...]).astype(BF16)
        a = _dot(cs1_ref[...], xj)
        ar, ai = a[:n1], a[n1:]
        c = twc_ref[0, :, j:j + 1]
        s = tws_ref[0, :, j:j + 1]
        br_ref[:, j] = (ar * c + ai * s).reshape(n1 // SUB, SUB, D_MODEL)
        bi_ref[:, j] = (ai * c - ar * s).reshape(n1 // SUB, SUB, D_MODEL)


def _fft1(x, g, nb, n1):
    t = x.shape[0]
    n = n1 * FFT_N2
    c1, s1 = _dft_mats(n1, 1.0 / math.sqrt(n1))
    cs1 = jnp.concatenate([c1, -s1], axis=0)
    k1 = np.arange(n1)[:, None]
    n2 = np.arange(FFT_N2)[None, :]
    ang = 2.0 * np.pi * ((k1 * n2) % n) / n
    nj = FFT_N2 // SUB
    nkc = n1 // SUB
    to_tab = lambda a: jnp.asarray(a.reshape(n1, nj, SUB).transpose(1, 0, 2), F32)
    twc, tws = to_tab(np.cos(ang)), to_tab(np.sin(ang))
    xv = x.reshape(nb * n1, FFT_N2, D_MODEL)
    out_blk = pl.BlockSpec((nkc, SUB, SUB, D_MODEL), lambda b, j: (b, j, 0, 0))
    out_shape = jax.ShapeDtypeStruct((nb * nkc, FFT_N2, SUB, D_MODEL), F32)
    tws_spec = pl.BlockSpec((1, n1, SUB), lambda b, j: (j, 0, 0))
    mat = pl.BlockSpec((2 * n1, n1), lambda b, j: (0, 0))
    br, bi = pl.pallas_call(
        _fft1_kernel,
        grid=(nb, nj),
        in_specs=_lane_slab_specs((n1, SUB), lambda b, j: (b, j))
        + [pl.BlockSpec((1, D_MODEL), lambda b, j: (0, 0)), mat, tws_spec, tws_spec],
        out_specs=[out_blk, out_blk],
        out_shape=[out_shape, out_shape],
        scratch_shapes=[pltpu.VMEM((LANE_SLABS, n1 * SUB, LANES), F32)],
        compiler_params=_cparams(("arbitrary", "arbitrary")),
        name="fft1",
    )(*([xv] * LANE_SLABS), g, cs1, twc, tws)
    return br.reshape(t, D_MODEL), bi.reshape(t, D_MODEL)


FFT_GROUP = 4


def _fft2_kernel(*refs):
    brs, bis, xs = (refs[i * LANE_SLABS:(i + 1) * LANE_SLABS] for i in range(3))
    (w2r_ref, w2i_ref, cc_ref, sc_ref, wout_ref, mg_ref, wrh_ref, wrl_ref,
     x1t_ref, lg_ref, f_scr, x_scr) = refs[3 * LANE_SLABS:]
    n2 = FFT_N2
    for c in range(LANE_SLABS):
        x_scr[c] = xs[c][...].reshape(n2 * SUB, LANES)
    for jp in range(SUB // FFT_GROUP):
        for q in range(FFT_GROUP):
            j = jp * FFT_GROUP + q
            bj = jnp.concatenate([_every_8th_row(brs, j, n2).astype(BF16),
                                  _every_8th_row(bis, j, n2).astype(BF16)], axis=0)
            yr = _dot(w2r_ref[...], bj).astype(BF16)
            yi = _dot(w2i_ref[...], bj).astype(BF16)
            for g in range(C_GROUPS):
                cols = slice(g * C_GROUP_DIM, (g + 1) * C_GROUP_DIM)
                f = _dot(yr[:, cols], cc_ref[...]) + _dot(yi[:, cols], sc_ref[...])
                f_scr[q * n2:(q + 1) * n2, cols] = f.astype(BF16)
        y = _dot(f_scr[...], wout_ref[...])
        for q in range(FFT_GROUP):
            j = jp * FFT_GROUP + q
            x1 = _every_8th_row(x_scr, j, n2) + y[q * n2:(q + 1) * n2, :]
            x1t_ref[:, j * SUB:(j + 1) * SUB, :] = x1.reshape(n2, SUB, LANES)
            lg_ref[:, j * n2:(j + 1) * n2] = _router_logits(x1, mg_ref, wrh_ref, wrl_ref)


def _fft2(br, bi, x, nb, n1, wout, mg, wr):
    t = x.shape[0]
    n2 = FFT_N2
    c2, s2 = _dft_mats(n2, 1.0 / math.sqrt(n2))
    w2r = jnp.concatenate([c2, s2], axis=1)
    w2i = jnp.concatenate([-s2, c2], axis=1)
    cc, sc = _dft_mats(C_GROUP_DIM, 1.0 / math.sqrt(C_GROUP_DIM))
    nkc = n1 // SUB
    rows = SUB * n2
    xv = x.reshape(nb * n2, n1, D_MODEL)
    full = lambda shape: pl.BlockSpec(shape, lambda b, k: (0,) * len(shape))
    b_slabs = _lane_slab_specs((rows,), lambda b, k: (b * nkc + k,))
    x_slabs = _lane_slab_specs((n2, SUB), lambda b, k: (b, k))
    x1, lg = pl.pallas_call(
        _fft2_kernel,
        grid=(nb, nkc),
        in_specs=b_slabs * 2 + x_slabs
        + [full((n2, 2 * n2)), full((n2, 2 * n2)),
           full((C_GROUP_DIM, C_GROUP_DIM)), full((C_GROUP_DIM, C_GROUP_DIM)),
           full((D_MODEL, D_MODEL)), full((1, D_MODEL)),
           full((N_EXPERTS, D_MODEL)), full((N_EXPERTS, D_MODEL))],
        out_specs=[pl.BlockSpec((n2, SUB * SUB, LANES), lambda b, k: (b, k, 0)),
                   pl.BlockSpec((N_EXPERTS, rows), lambda b, k: (0, b * nkc + k))],
        out_shape=[jax.ShapeDtypeStruct((nb * n2, n1 * SUB, LANES), F32),
                   jax.ShapeDtypeStruct((N_EXPERTS, t), F32)],
        scratch_shapes=[pltpu.VMEM((FFT_GROUP * n2, D_MODEL), BF16),
                        pltpu.VMEM((LANE_SLABS, rows, LANES), F32)],
        compiler_params=_cparams(("arbitrary", "arbitrary")),
        name="fft2",
    )(*([br] * LANE_SLABS), *([bi] * LANE_SLABS), *([xv] * LANE_SLABS),
      w2r, w2i, cc, sc, wout, mg, *wr)
    lg = lg.reshape(N_EXPERTS, nb, n1, n2).transpose(0, 1, 3, 2).reshape(N_EXPERTS, t)
    return x1.reshape(t * SUB, LANES), lg


def _moe(x1, logits_t, mg, p, layer, g_final=None):
    t = x1.shape[0] // SUB
    cap = CAPACITY_FACTOR * t // N_EXPERTS
    idx, gate, cb, pos = _route(logits_t, cap)
    ye = _ffn(idx, gate, x1, mg, p["wg"], p["wu"], p["wd"], layer)
    return _combine(cb, pos, x1, ye, g_final)


def _trunk(x3, p):
    nb, s, _ = x3.shape
    x = x3.reshape(nb * s, D_MODEL)
    row = lambda v: v.reshape(1, -1)
    outa, glu = _front(x, row(p["mix_norm"][0]), p["w_in"], row(p["ab_v_norm_g"][0]),
                       row(p["ab_v_norm_b"][0]), p["w_spatial"], p["b_spatial"])
    mg0, mg1 = row(p["moe_norm"][0]), row(p["moe_norm"][1])
    x1, lg = _back(x, outa, glu, p["ab_conv_w"][0], row(p["ab_conv_b"][0]),
                   row(p["ab_conv_norm_g"][0]), row(p["ab_conv_norm_b"][0]), p["ab_w_out"],
                   mg0, p["router"][0], s)
    x2 = _moe(x1, lg, mg0, p, 0)
    n1 = s // FFT_N2
    br, bi = _fft1(x2, row(p["mix_norm"][1]), nb, n1)
    x1, lg = _fft2(br, bi, x2, nb, n1, p["c_w_out"], mg1, p["router"][1])
    y = _moe(x1, lg, mg1, p, 1, row(p["final_norm"]))
    return y.reshape(nb, s, D_MODEL)


def kernel(x_prompt, x_sample, mix_norm, ab_w_in, ab_v_norm_g, ab_v_norm_b, ab_w_spatial,
           ab_b_spatial, ab_conv_w, ab_conv_b, ab_conv_norm_g, ab_conv_norm_b, ab_w_out,
           c_w_out, moe_norm, moe_router, moe_w_gate, moe_w_up, moe_w_down, final_norm):
    p = dict(
        mix_norm=mix_norm, moe_norm=moe_norm, final_norm=final_norm,
        w_in=ab_w_in[0].astype(BF16),
        ab_v_norm_g=ab_v_norm_g, ab_v_norm_b=ab_v_norm_b,
        w_spatial=ab_w_spatial[0].astype(BF16),
        b_spatial=jnp.repeat(jnp.transpose(ab_b_spatial[0]), A_HEAD_DIM, axis=1),
        ab_conv_w=ab_conv_w, ab_conv_b=ab_conv_b,
        ab_conv_norm_g=ab_conv_norm_g, ab_conv_norm_b=ab_conv_norm_b,
        ab_w_out=ab_w_out[0].astype(BF16),
        c_w_out=c_w_out[0].astype(BF16),
        router=[_split2(jnp.transpose(moe_router[l])) for l in range(moe_router.shape[0])],
        wg=moe_w_gate, wu=moe_w_up, wd=moe_w_down,
    )
    return (_trunk(x_prompt, p), _trunk(x_sample, p))
```
